```python
import math
import jax, jax.numpy as jnp
from jax import lax
import numpy as np

D_MODEL = 1024
BATCH = 16
SEQ = 2048
DEPTH = 1

RNN_WIDTH = 1024
RNN_BLOCKS = 16
RNN_BLOCK_DIM = RNN_WIDTH // RNN_BLOCKS
CONV_WIDTH = 4
LRU_C = 8.0
N_HEADS = 8
HEAD_DIM = 128
ATTN_WIDTH = N_HEADS * HEAD_DIM
ROPE_DIM = HEAD_DIM // 4
ROPE_THETA = 500000.0
MOBA_BLOCK = 256
MOBA_TOPK = 3
Q_CHUNK = 256
N_EXPERTS = 32
TOP_K = 4
D_EXPERT = 1024
SWIGLU_LIMIT = 7.0
SWIGLU_ALPHA = 1.702
EXPERT_ROWS = 256
EPS = 1e-6
NEG = -1e30

kernel_name = 'hybrid_rglru_moba_moe'


def rms_norm(x, g):
    xf = x.astype(jnp.float32)
    y = xf * lax.rsqrt(jnp.mean(xf * xf, axis=-1, keepdims=True) + EPS)
    return (y * g.astype(jnp.float32)).astype(x.dtype)


def causal_depthwise_conv(x, w, b):
    s = x.shape[1]
    xp = jnp.pad(x, ((0, 0), (CONV_WIDTH - 1, 0), (0, 0)))
    y = b
    for tap in range(CONV_WIDTH):
        y = y + xp[:, tap:tap + s] * w[tap]
    return y


def rg_lru(x, wx, bx, wa, ba, a_param):
    bsz, s, c = x.shape
    xb = x.reshape(bsz, s, RNN_BLOCKS, RNN_BLOCK_DIM)
    gate_x = jax.nn.sigmoid(jnp.einsum('bsgi,gij->bsgj', xb, wx).reshape(bsz, s, c) + bx)
    gate_a = jax.nn.sigmoid(jnp.einsum('bsgi,gij->bsgj', xb, wa).reshape(bsz, s, c) + ba)
    log_a = -LRU_C * gate_a.astype(jnp.float32) * jax.nn.softplus(-a_param.astype(jnp.float32))
    a = jnp.exp(log_a)
    u = jnp.sqrt(-jnp.expm1(2.0 * log_a)) * (gate_x * x).astype(jnp.float32)

    def combine(left, right):
        a_l, b_l = left
        a_r, b_r = right
        return a_l * a_r, a_r * b_l + b_r

    _, h = lax.associative_scan(combine, (a, u), axis=1)
    return h.astype(x.dtype)


def partial_rotary(x):
    s = x.shape[1]
    half = ROPE_DIM // 2
    inv_freq = ROPE_THETA ** (-jnp.arange(half, dtype=jnp.float32) * (2.0 / ROPE_DIM))
    ang = jnp.arange(s, dtype=jnp.float32)[:, None] * inv_freq[None, :]
    cos = jnp.cos(ang)[:, None, :]
    sin = jnp.sin(ang)[:, None, :]
    xf = x.astype(jnp.float32)
    x1 = xf[..., :half]
    x2 = xf[..., half:ROPE_DIM]
    out = jnp.concatenate([x1 * cos - x2 * sin, x2 * cos + x1 * sin, xf[..., ROPE_DIM:]], axis=-1)
    return out.astype(x.dtype)


def moba_attention(q, k, v):
    bsz, s, h, d = q.shape
    nb = -(-s // MOBA_BLOCK)
    s_pad = nb * MOBA_BLOCK
    topk = min(MOBA_TOPK, nb)
    scale = 1.0 / math.sqrt(d)

    def to_heads(t):
        return jnp.pad(t, ((0, 0), (0, s_pad - s), (0, 0), (0, 0))).transpose(0, 2, 1, 3)

    qh, kh, vh = to_heads(q), to_heads(k), to_heads(v)
    qb = qh.reshape(bsz, h, nb, MOBA_BLOCK, d)
    kb = kh.reshape(bsz, h, nb, MOBA_BLOCK, d)
    vb = vh.reshape(bsz, h, nb, MOBA_BLOCK, d)

    s_own = jnp.einsum('bhnqd,bhnkd->bhnqk', qb, kb).astype(jnp.float32) * scale
    causal = jnp.tril(jnp.ones((MOBA_BLOCK, MOBA_BLOCK), dtype=bool))
    s_own = jnp.where(causal, s_own, NEG)
    lse_own = jax.nn.logsumexp(s_own, axis=-1)
    p_own = jnp.exp(s_own - lse_own[..., None]).astype(vb.dtype)
    o_own = jnp.einsum('bhnqk,bhnkd->bhnqd', p_own, vb).reshape(bsz, h, s_pad, d).astype(jnp.float32)
    lse_own = lse_own.reshape(bsz, h, s_pad)

    k_mean = jnp.mean(kb.astype(jnp.float32), axis=3)
    gate = jnp.einsum('bhsd,bhnd->bhsn', qh.astype(jnp.float32), k_mean)
    q_block = jnp.arange(s_pad, dtype=jnp.int32) // MOBA_BLOCK
    past = jnp.arange(nb, dtype=jnp.int32)[None, :] < q_block[:, None]
    gate = jnp.where(past, gate, NEG)
    _, sel = lax.top_k(gate, topk)
    valid = sel < q_block[:, None]
    flat_sel = sel + (jnp.arange(bsz * h, dtype=jnp.int32) * nb).reshape(bsz, h, 1, 1)
    k_flat = kb.reshape(bsz * h * nb, MOBA_BLOCK, d)
    v_flat = vb.reshape(bsz * h * nb, MOBA_BLOCK, d)
    n_chunks = bsz * h * s_pad // Q_CHUNK

    def attend_selected(args):
        qc, ic, vc = args
        kg = k_flat[ic]
        vg = v_flat[ic]
        sc = jnp.einsum('cd,cjkd->cjk', qc, kg).astype(jnp.float32) * scale
        sc = jnp.where(vc[:, :, None], sc, NEG).reshape(Q_CHUNK, topk * MOBA_BLOCK)
        lse = jax.nn.logsumexp(sc, axis=-1)
        p = jnp.exp(sc - lse[:, None]).reshape(Q_CHUNK, topk, MOBA_BLOCK).astype(vg.dtype)
        return jnp.einsum('cjk,cjkd->cd', p, vg).astype(jnp.float32), lse

    o_sel, lse_sel = lax.map(attend_selected, (qh.reshape(n_chunks, Q_CHUNK, d),
                                               flat_sel.reshape(n_chunks, Q_CHUNK, topk),
                                               valid.reshape(n_chunks, Q_CHUNK, topk)))
    o_sel = o_sel.reshape(bsz, h, s_pad, d)
    lse_sel = lse_sel.reshape(bsz, h, s_pad)
    lse = jnp.logaddexp(lse_own, lse_sel)
    o = jnp.exp(lse_own - lse)[..., None] * o_own + jnp.exp(lse_sel - lse)[..., None] * o_sel
    return o[:, :, :s].transpose(0, 2, 1, 3).reshape(bsz, s, h * d).astype(q.dtype)


def moe_ffn(xn, router_w, router_b, w_gu, b_gu, w_down, b_down):
    bsz, s, d = xn.shape
    n_tok = bsz * s
    xt = xn.reshape(n_tok, d)
    logits = (xt @ router_w).astype(jnp.float32) + router_b.astype(jnp.float32)
    top_val, top_idx = lax.top_k(logits, TOP_K)
    gates = jax.nn.softmax(top_val, axis=-1)

    n_assign = n_tok * TOP_K
    e_flat = top_idx.reshape(-1).astype(jnp.int32)
    tok_flat = jnp.arange(n_assign, dtype=jnp.int32) // TOP_K
    w_flat = gates.reshape(-1)
    order = jnp.argsort(e_flat)
    e_s, tok_s, w_s = e_flat[order], tok_flat[order], w_flat[order]
    counts = jnp.zeros((N_EXPERTS,), jnp.int32).at[e_flat].add(1)
    start = jnp.cumsum(counts) - counts
    padded = ((counts + EXPERT_ROWS - 1) // EXPERT_ROWS) * EXPERT_ROWS
    pend = jnp.cumsum(padded)
    pstart = pend - padded
    dest = pstart[e_s] + (jnp.arange(n_assign, dtype=jnp.int32) - start[e_s])
    n_rows = n_assign + N_EXPERTS * EXPERT_ROWS
    n_blk = n_rows // EXPERT_ROWS
    row_tok = jnp.full((n_rows,), n_tok, jnp.int32).at[dest].set(tok_s)
    row_w = jnp.zeros((n_rows,), jnp.float32).at[dest].set(w_s)
    blk_start = jnp.arange(n_blk, dtype=jnp.int32) * EXPERT_ROWS
    blk_expert = jnp.minimum(jnp.sum(blk_start[:, None] >= pend[None, :], axis=1), N_EXPERTS - 1)
    x_rows = jnp.concatenate([xt, jnp.zeros((1, d), xt.dtype)], axis=0)[row_tok]
    x_rows = x_rows.reshape(n_blk, EXPERT_ROWS, d)

    def expert_block(args):
        xb, e = args
        gu = xb @ w_gu[e] + b_gu[e]
        x_glu = jnp.minimum(gu[:, :D_EXPERT], SWIGLU_LIMIT)
        x_lin = jnp.clip(gu[:, D_EXPERT:], -SWIGLU_LIMIT, SWIGLU_LIMIT)
        act = x_glu * jax.nn.sigmoid(SWIGLU_ALPHA * x_glu) * (x_lin + 1.0)
        return act @ w_down[e] + b_down[e]

    y_blocks = lax.map(expert_block, (x_rows, blk_expert))
    y_rows = y_blocks.reshape(n_rows, d) * row_w[:, None].astype(y_blocks.dtype)
    y = jax.ops.segment_sum(y_rows, row_tok, num_segments=n_tok + 1)[:n_tok]
    return y.reshape(bsz, s, d)


def setup_inputs(seed: int = 0) -> dict:
    key = jax.random.key(seed)
    ks = jax.random.split(key, 21)
    f32 = jnp.float32

    def nrm(k, shape, scale):
        return jax.random.normal(k, shape, f32) * scale

    in_cols = 2 * RNN_WIDTH + 3 * ATTN_WIDTH + 2 * D_MODEL
    u = jax.random.uniform(ks[9], (DEPTH, RNN_WIDTH), f32, minval=0.9, maxval=0.999)
    s_lam = u ** (1.0 / LRU_C)
    return {
        'x': nrm(ks[0], (BATCH, SEQ, D_MODEL), 1.0),
        'norm_mix_g': 1.0 + nrm(ks[1], (DEPTH, D_MODEL), 0.02),
        'w_in': nrm(ks[2], (DEPTH, D_MODEL, in_cols), D_MODEL ** -0.5),
        'conv_w': nrm(ks[3], (DEPTH, CONV_WIDTH, RNN_WIDTH), CONV_WIDTH ** -0.5),
        'conv_b': nrm(ks[4], (DEPTH, RNN_WIDTH), 0.01),
        'lru_wx': nrm(ks[5], (DEPTH, RNN_BLOCKS, RNN_BLOCK_DIM, RNN_BLOCK_DIM), RNN_BLOCK_DIM ** -0.5),
        'lru_bx': nrm(ks[6], (DEPTH, RNN_WIDTH), 0.01),
        'lru_wa': nrm(ks[7], (DEPTH, RNN_BLOCKS, RNN_BLOCK_DIM, RNN_BLOCK_DIM), RNN_BLOCK_DIM ** -0.5),
        'lru_ba': nrm(ks[8], (DEPTH, RNN_WIDTH), 0.01),
        'lru_a_param': jnp.log(s_lam) - jnp.log1p(-s_lam),
        'w_proj_rnn': nrm(ks[10], (DEPTH, RNN_WIDTH, D_MODEL), RNN_WIDTH ** -0.5),
        'w_proj_attn': nrm(ks[11], (DEPTH, ATTN_WIDTH, D_MODEL), ATTN_WIDTH ** -0.5),
        'w_out': nrm(ks[12], (DEPTH, D_MODEL, D_MODEL), D_MODEL ** -0.5),
        'norm_ffn_g': 1.0 + nrm(ks[13], (DEPTH, D_MODEL), 0.02),
        'router_w': nrm(ks[14], (DEPTH, D_MODEL, N_EXPERTS), D_MODEL ** -0.5),
        'router_b': nrm(ks[15], (DEPTH, N_EXPERTS), 0.01),
        'expert_w_gu': nrm(ks[16], (DEPTH, N_EXPERTS, D_MODEL, 2 * D_EXPERT), D_MODEL ** -0.5),
        'expert_b_gu': nrm(ks[17], (DEPTH, N_EXPERTS, 2 * D_EXPERT), 0.01),
        'expert_w_down': nrm(ks[18], (DEPTH, N_EXPERTS, D_EXPERT, D_MODEL), D_EXPERT ** -0.5),
        'expert_b_down': nrm(ks[19], (DEPTH, N_EXPERTS, D_MODEL), 0.01),
        'norm_final_g': 1.0 + nrm(ks[20], (D_MODEL,), 0.02),
    }


def reference(x, norm_mix_g, w_in, conv_w, conv_b, lru_wx, lru_bx, lru_wa, lru_ba, lru_a_param,
              w_proj_rnn, w_proj_attn, w_out, norm_ffn_g, router_w, router_b,
              expert_w_gu, expert_b_gu, expert_w_down, expert_b_down, norm_final_g):
    bsz, s, _ = x.shape
    splits = [RNN_WIDTH, 2 * RNN_WIDTH, 2 * RNN_WIDTH + ATTN_WIDTH, 2 * RNN_WIDTH + 2 * ATTN_WIDTH,
              2 * RNN_WIDTH + 3 * ATTN_WIDTH, 2 * RNN_WIDTH + 3 * ATTN_WIDTH + D_MODEL]
    for layer in range(DEPTH):
        xn = rms_norm(x, norm_mix_g[layer])
        z = xn @ w_in[layer]
        rnn_x, rnn_gate, q, k, v, g_rnn, g_attn = jnp.split(z, splits, axis=-1)
        xc = causal_depthwise_conv(rnn_x, conv_w[layer], conv_b[layer])
        hr = rg_lru(xc, lru_wx[layer], lru_bx[layer], lru_wa[layer], lru_ba[layer], lru_a_param[layer])
        y_rnn = (hr * jax.nn.gelu(rnn_gate)) @ w_proj_rnn[layer]
        q = partial_rotary(q.reshape(bsz, s, N_HEADS, HEAD_DIM))
        k = partial_rotary(k.reshape(bsz, s, N_HEADS, HEAD_DIM))
        v = v.reshape(bsz, s, N_HEADS, HEAD_DIM)
        y_attn = moba_attention(q, k, v) @ w_proj_attn[layer]
        mixed = jax.nn.sigmoid(g_rnn) * y_rnn + jax.nn.sigmoid(g_attn) * y_attn
        x = x + mixed @ w_out[layer]
        hn = rms_norm(x, norm_ffn_g[layer])
        x = x + moe_ffn(hn, router_w[layer], router_b[layer], expert_w_gu[layer], expert_b_gu[layer],
                        expert_w_down[layer], expert_b_down[layer])
    return rms_norm(x, norm_final_g)
```

```python
import functools
import math

import jax
import jax.numpy as jnp
from jax import lax
from jax.experimental import pallas as pl
from jax.experimental.pallas import tpu as pltpu

F32 = jnp.float32
BF16 = jnp.bfloat16

D_MODEL = 1024
RNN_WIDTH = 1024
RNN_BLOCKS = 16
RNN_BLOCK_DIM = RNN_WIDTH // RNN_BLOCKS
CONV_WIDTH = 4
LRU_C = 8.0
N_HEADS = 8
HEAD_DIM = 128
ROPE_DIM = HEAD_DIM // 4
ROPE_THETA = 500000.0
MOBA_BLOCK = 256
MOBA_TOPK = 3
N_EXPERTS = 32
TOP_K = 4
D_EXPERT = 1024
SWIGLU_LIMIT = 7.0
SWIGLU_ALPHA = 1.702
EXPERT_ROWS = 256
EPS = 1e-6
NEG = -1e30

V7X_VMEM_BYTES = 64 * 1024 * 1024
V7X_MXU_DIM = 256
SUBLANES = 8
LANES = 128
BF16_SUBLANES = 16

ROW_TILE = 512
GATHER_TILE = 256
LRU_GROUP = V7X_MXU_DIM // RNN_BLOCK_DIM
VMEM_LIMIT = V7X_VMEM_BYTES - 8 * 1024 * 1024


def _nt_dot(a, b, **kw):
    return lax.dot_general(a, b, (((1,), (1,)), ((), ())), preferred_element_type=F32, **kw)


def _gelu_tanh(x):
    return 0.5 * x * (1.0 + jnp.tanh(math.sqrt(2.0 / math.pi) * (x + 0.044715 * (x * x * x))))


def _inproj_body(x_ref, g_ref, w_ref, rot_ref, rx_ref, rg_ref, q_ref, k_ref, v_ref, sr_ref, sa_ref):
    x = x_ref[...]
    ms = jnp.mean(x * x, axis=-1, keepdims=True)
    xn = (x * lax.rsqrt(ms + EPS) * g_ref[...]).astype(BF16)

    def proj(j):
        return jnp.dot(xn, w_ref[:, j * D_MODEL:(j + 1) * D_MODEL], preferred_element_type=F32)

    rx_ref[...] = proj(0).astype(BF16)
    rg_ref[...] = _gelu_tanh(proj(1)).astype(BF16)

    cos, sin_hi, sin_lo = rot_ref[0], rot_ref[1], rot_ref[2]
    half = ROPE_DIM // 2

    def rotary_heads(z, out_ref, scale):
        for h in range(N_HEADS):
            zh = z[:, h * HEAD_DIM:(h + 1) * HEAD_DIM]
            r = zh * cos + pltpu.roll(zh, HEAD_DIM - half, axis=1) * sin_hi + pltpu.roll(zh, half, axis=1) * sin_lo
            out_ref[h] = (r * scale).astype(BF16)

    rotary_heads(proj(2), q_ref, 1.0 / math.sqrt(HEAD_DIM))
    rotary_heads(proj(3), k_ref, 1.0)
    zv = proj(4)
    for h in range(N_HEADS):
        v_ref[h] = zv[:, h * HEAD_DIM:(h + 1) * HEAD_DIM].astype(BF16)
    sr_ref[...] = jax.nn.sigmoid(proj(5)).astype(BF16)
    sa_ref[...] = jax.nn.sigmoid(proj(6)).astype(BF16)


def _rotary_tables(seq):
    half = ROPE_DIM // 2
    inv_freq = ROPE_THETA ** (-jnp.arange(half, dtype=F32) * (2.0 / ROPE_DIM))
    ang = jnp.arange(seq, dtype=F32)[:, None] * inv_freq[None, :]
    cos, sin = jnp.cos(ang), jnp.sin(ang)
    pad = HEAD_DIM - ROPE_DIM
    cos_t = jnp.concatenate([cos, cos, jnp.ones((seq, pad), F32)], axis=1)
    sin_hi = jnp.concatenate([-sin, jnp.zeros((seq, half + pad), F32)], axis=1)
    sin_lo = jnp.concatenate([jnp.zeros((seq, half), F32), sin, jnp.zeros((seq, pad), F32)], axis=1)
    return jnp.stack([cos_t, sin_hi, sin_lo])


def _in_proj(x2, norm_g, w_in, seq):
    n_tok = x2.shape[0]
    tm = ROW_TILE
    tiles_per_seq = seq // tm
    in_cols = w_in.shape[1]
    row = pl.BlockSpec((tm, D_MODEL), lambda i: (i, 0))
    head = pl.BlockSpec((None, N_HEADS, tm, HEAD_DIM), lambda i: (i // tiles_per_seq, 0, i % tiles_per_seq, 0))
    bsz = n_tok // seq
    tok_bf16 = jax.ShapeDtypeStruct((n_tok, D_MODEL), BF16)
    head_bf16 = jax.ShapeDtypeStruct((bsz, N_HEADS, seq, HEAD_DIM), BF16)
    return pl.pallas_call(
        _inproj_body,
        grid=(n_tok // tm,),
        in_specs=[
            row,
            pl.BlockSpec((1, D_MODEL), lambda i: (0, 0)),
            pl.BlockSpec((D_MODEL, in_cols), lambda i: (0, 0), pipeline_mode=pl.Buffered(1)),
            pl.BlockSpec((3, tm, HEAD_DIM), lambda i: (0, i % tiles_per_seq, 0)),
        ],
        out_specs=[row, row, head, head, head, row, row],
        out_shape=[tok_bf16, tok_bf16, head_bf16, head_bf16, head_bf16, tok_bf16, tok_bf16],
        compiler_params=pltpu.CompilerParams(dimension_semantics=("parallel",), vmem_limit_bytes=VMEM_LIMIT),
        name="in_proj",
    )(x2, norm_g.reshape(1, D_MODEL), w_in.astype(BF16), _rotary_tables(seq))


def _rnn_body(rx_ref, halo_ref, rg_ref, cw_ref, cb_ref, wbd_ref, bx_ref, ba_ref, ap_ref, hg_ref,
              a_s, b_s, h_s, hprev_s):
    s = pl.program_id(1)
    ts = rx_ref.shape[0]
    x = rx_ref[...].astype(F32)
    halo = jnp.where(s == 0, 0.0, halo_ref[...].astype(F32))
    n_halo = halo.shape[0]
    xfull = jnp.concatenate([halo, x], axis=0)
    xc = cb_ref[...] + cw_ref[CONV_WIDTH - 1:CONV_WIDTH, :] * x
    for shift in range(1, CONV_WIDTH):
        tap = CONV_WIDTH - 1 - shift
        xc = xc + cw_ref[tap:tap + 1, :] * pltpu.roll(xfull, shift, axis=0)[n_halo:, :]

    xcb = xc.astype(BF16)
    gx, ga = [], []
    for g in range(RNN_WIDTH // V7X_MXU_DIM):
        r = jnp.dot(xcb[:, g * V7X_MXU_DIM:(g + 1) * V7X_MXU_DIM], wbd_ref[g], preferred_element_type=F32)
        gx.append(r[:, :V7X_MXU_DIM])
        ga.append(r[:, V7X_MXU_DIM:])
    gate_x = jax.nn.sigmoid(jnp.concatenate(gx, axis=1) + bx_ref[...])
    gate_a = jax.nn.sigmoid(jnp.concatenate(ga, axis=1) + ba_ref[...])
    z = -ap_ref[...]
    softplus = jnp.maximum(z, 0.0) + jnp.log1p(jnp.exp(-jnp.abs(z)))
    log_a = -LRU_C * gate_a * softplus
    a = jnp.exp(log_a)
    u = jnp.sqrt(-jnp.tanh(log_a) * (a * a + 1.0)) * (gate_x * xc)

    r8 = lax.broadcasted_iota(jnp.int32, (ts, 1), 0) % SUBLANES
    d = 1
    while d < SUBLANES:
        inside = r8 >= d
        a_prev = jnp.where(inside, pltpu.roll(a, d, axis=0), 1.0)
        u_prev = jnp.where(inside, pltpu.roll(u, d, axis=0), 0.0)
        u = a * u_prev + u
        a = a * a_prev
        d *= 2
    a_s[...] = a
    b_s[...] = u

    def chain(g, h_last):
        rows = pl.ds(pl.multiple_of(g * SUBLANES, SUBLANES), SUBLANES)
        h = b_s[rows, :] + a_s[rows, :] * h_last
        h_s[rows, :] = h
        return h[SUBLANES - 1:SUBLANES, :]

    h_first = jnp.where(s == 0, 0.0, hprev_s[...])
    hprev_s[...] = lax.fori_loop(0, ts // SUBLANES, chain, h_first)
    hg_ref[...] = (h_s[...] * rg_ref[...].astype(F32)).astype(BF16)


def _block_diag_gates(wx, wa):
    n_grp = RNN_BLOCKS // LRU_GROUP

    def bd(w):
        w = w.reshape(n_grp, LRU_GROUP, RNN_BLOCK_DIM, RNN_BLOCK_DIM)
        eye = jnp.eye(LRU_GROUP, dtype=w.dtype)
        full = jnp.einsum('gaij,ab->gaibj', w, eye)
        return full.reshape(n_grp, V7X_MXU_DIM, V7X_MXU_DIM)

    return jnp.concatenate([bd(wx), bd(wa)], axis=2).astype(BF16)


def _rnn(rx, rg, conv_w, conv_b, wx, bx, wa, ba, a_param, bsz, seq):
    ts = ROW_TILE
    tiles_per_seq = seq // ts
    halo_rows = BF16_SUBLANES
    halo_per_tile = ts // halo_rows
    row = pl.BlockSpec((ts, RNN_WIDTH), lambda b, s: (b * tiles_per_seq + s, 0))
    halo = pl.BlockSpec((halo_rows, RNN_WIDTH),
                        lambda b, s: (jnp.maximum((b * tiles_per_seq + s) * halo_per_tile - 1, 0), 0))
    vec = pl.BlockSpec((1, RNN_WIDTH), lambda b, s: (0, 0))
    n_grp = RNN_WIDTH // V7X_MXU_DIM
    return pl.pallas_call(
        _rnn_body,
        grid=(bsz, tiles_per_seq),
        in_specs=[row, halo, row,
                  pl.BlockSpec((CONV_WIDTH, RNN_WIDTH), lambda b, s: (0, 0)), vec,
                  pl.BlockSpec((n_grp, V7X_MXU_DIM, 2 * V7X_MXU_DIM), lambda b, s: (0, 0, 0)),
                  vec, vec, vec],
        out_specs=row,
        out_shape=jax.ShapeDtypeStruct((bsz * seq, RNN_WIDTH), BF16),
        scratch_shapes=[pltpu.VMEM((ts, RNN_WIDTH), F32), pltpu.VMEM((ts, RNN_WIDTH), F32),
                        pltpu.VMEM((ts, RNN_WIDTH), F32), pltpu.VMEM((1, RNN_WIDTH), F32)],
        compiler_params=pltpu.CompilerParams(dimension_semantics=("parallel", "arbitrary"),
                                             vmem_limit_bytes=VMEM_LIMIT),
        name="rnn",
    )(rx, rx, rg, conv_w, conv_b.reshape(1, -1), _block_diag_gates(wx, wa),
      bx.reshape(1, -1), ba.reshape(1, -1), a_param.reshape(1, -1))


def _moba_body(q_ref, k_ref, v_ref, o_ref):
    seq = q_ref.shape[0]
    nb = seq // MOBA_BLOCK
    kmean = jnp.mean(k_ref[...].astype(F32).reshape(nb, MOBA_BLOCK, HEAD_DIM), axis=1)
    kmean = jnp.concatenate([kmean, jnp.zeros((LANES - nb, HEAD_DIM), F32)], axis=0)
    r_id = lax.broadcasted_iota(jnp.int32, (MOBA_BLOCK, MOBA_BLOCK), 0)
    c_id = lax.broadcasted_iota(jnp.int32, (MOBA_BLOCK, MOBA_BLOCK), 1)
    causal = c_id <= r_id

    def blk(ref, j):
        return ref[j * MOBA_BLOCK:(j + 1) * MOBA_BLOCK, :]

    for n in range(nb):
        qn = blk(q_ref, n)
        s = jnp.where(causal, _nt_dot(qn, blk(k_ref, n)), NEG)
        m = jnp.max(s, axis=1, keepdims=True)
        p = jnp.exp(s - m)
        l = jnp.sum(p, axis=1, keepdims=True)
        acc = jnp.dot(p.astype(BF16), blk(v_ref, n), preferred_element_type=F32)

        sel = [None] * n
        if n > MOBA_TOPK:
            gate = _nt_dot(qn.astype(F32), kmean, precision=lax.Precision.HIGHEST)
            g = [gate[:, j:j + 1] for j in range(n)]
            rank = [jnp.zeros((MOBA_BLOCK, 1), F32) for _ in range(n)]
            for lo in range(n):
                for hi in range(lo + 1, n):
                    lo_wins = (g[lo] >= g[hi]).astype(F32)
                    rank[hi] = rank[hi] + lo_wins
                    rank[lo] = rank[lo] + (1.0 - lo_wins)
            sel = [r < float(MOBA_TOPK) for r in rank]

        for j in range(n):
            s = _nt_dot(qn, blk(k_ref, j))
            if sel[j] is not None:
                s = jnp.where(sel[j], s, NEG)
            m_new = jnp.maximum(m, jnp.max(s, axis=1, keepdims=True))
            alpha = jnp.exp(m - m_new)
            p = jnp.exp(s - m_new)
            l = alpha * l + jnp.sum(p, axis=1, keepdims=True)
            acc = alpha * acc + jnp.dot(p.astype(BF16), blk(v_ref, j), preferred_element_type=F32)
            m = m_new
        o_ref[n * MOBA_BLOCK:(n + 1) * MOBA_BLOCK, :] = (acc / l).astype(BF16)


def _moba(q, k, v):
    bsz, n_heads, seq, hd = q.shape
    spec = pl.BlockSpec((None, None, seq, hd), lambda b, h: (b, h, 0, 0))
    return pl.pallas_call(
        _moba_body,
        grid=(bsz, n_heads),
        in_specs=[spec, spec, spec],
        out_specs=spec,
        out_shape=jax.ShapeDtypeStruct(q.shape, BF16),
        compiler_params=pltpu.CompilerParams(dimension_semantics=("parallel", "parallel"),
                                             vmem_limit_bytes=VMEM_LIMIT),
        name="moba",
    )(q, k, v)


def _merge_body(x_ref, hg_ref, at_ref, sr_ref, sa_ref, wpr_ref, wpa_ref, wo_ref, gffn_ref, rwt_ref, rb_ref,
                x1_ref, hn_ref, idx_ref, gate_ref, rank_ref, cnt_ref, carry_s):
    tm = x_ref.shape[0]

    @pl.when(pl.program_id(0) == 0)
    def _():
        carry_s[...] = jnp.zeros_like(carry_s)

    y_rnn = jnp.dot(hg_ref[...], wpr_ref[...], preferred_element_type=F32)
    attn = jnp.concatenate([at_ref[h] for h in range(N_HEADS)], axis=1)
    y_attn = jnp.dot(attn, wpa_ref[...], preferred_element_type=F32)
    mixed = sr_ref[...].astype(F32) * y_rnn + sa_ref[...].astype(F32) * y_attn
    x1 = x_ref[...] + jnp.dot(mixed.astype(BF16), wo_ref[...], preferred_element_type=F32)
    x1_ref[...] = x1
    hn = x1 * lax.rsqrt(jnp.mean(x1 * x1, axis=-1, keepdims=True) + EPS) * gffn_ref[...]
    hn_ref[...] = hn

    logits = _nt_dot(rwt_ref[...], hn, precision=lax.Precision.HIGHEST) + rb_ref[...]
    e_id = lax.broadcasted_iota(jnp.int32, (N_EXPERTS, tm), 0)
    vals = logits
    onehots, top_vals, top_ids = [], [], []
    for _ in range(TOP_K):
        best = jnp.max(vals, axis=0, keepdims=True)
        best_id = jnp.min(jnp.where(vals == best, e_id, N_EXPERTS), axis=0, keepdims=True)
        hit = e_id == best_id
        vals = jnp.where(hit, -jnp.inf, vals)
        onehots.append(hit)
        top_vals.append(best)
        top_ids.append(best_id)
    exps = [jnp.exp(v - top_vals[0]) for v in top_vals]
    denom = exps[0] + exps[1] + exps[2] + exps[3]
    idx_ref[...] = jnp.concatenate(top_ids, axis=0)
    gate_ref[...] = jnp.concatenate([e / denom for e in exps], axis=0)

    chosen = onehots[0] | onehots[1] | onehots[2] | onehots[3]
    t_row = lax.broadcasted_iota(jnp.int32, (tm, tm), 0)
    t_col = lax.broadcasted_iota(jnp.int32, (tm, tm), 1)
    before = (t_row < t_col).astype(BF16)
    chosen_f = chosen.astype(F32)
    prior = jnp.dot(chosen_f.astype(BF16), before, preferred_element_type=F32) + carry_s[:, 0:1]
    ranks = [jnp.sum(jnp.where(hit, prior, 0.0), axis=0, keepdims=True) for hit in onehots]
    rank_ref[...] = jnp.concatenate(ranks, axis=0).astype(jnp.int32)
    carry_s[...] = carry_s[...] + jnp.sum(chosen_f, axis=1, keepdims=True)
    cnt_ref[...] = carry_s[...].astype(jnp.int32)


def _merge(x2, hg, attn, sr, sa, w_proj_rnn, w_proj_attn, w_out, norm_ffn_g, router_w, router_b, seq):
    n_tok = x2.shape[0]
    tm = ROW_TILE
    tiles_per_seq = seq // tm
    row = pl.BlockSpec((tm, D_MODEL), lambda i: (i, 0))
    head = pl.BlockSpec((None, N_HEADS, tm, HEAD_DIM), lambda i: (i // tiles_per_seq, 0, i % tiles_per_seq, 0))
    mat = pl.BlockSpec((D_MODEL, D_MODEL), lambda i: (0, 0))
    topk = pl.BlockSpec((TOP_K, tm), lambda i: (0, i))
    return pl.pallas_call(
        _merge_body,
        grid=(n_tok // tm,),
        in_specs=[row, row, head, row, row, mat, mat, mat,
                  pl.BlockSpec((1, D_MODEL), lambda i: (0, 0)),
                  pl.BlockSpec((N_EXPERTS, D_MODEL), lambda i: (0, 0)),
                  pl.BlockSpec((N_EXPERTS, 1), lambda i: (0, 0))],
        out_specs=[row, row, topk, topk, topk, pl.BlockSpec((N_EXPERTS, LANES), lambda i: (0, 0))],
        out_shape=[jax.ShapeDtypeStruct((n_tok, D_MODEL), F32), jax.ShapeDtypeStruct((n_tok, D_MODEL), F32),
                   jax.ShapeDtypeStruct((TOP_K, n_tok), jnp.int32), jax.ShapeDtypeStruct((TOP_K, n_tok), F32),
                   jax.ShapeDtypeStruct((TOP_K, n_tok), jnp.int32),
                   jax.ShapeDtypeStruct((N_EXPERTS, LANES), jnp.int32)],
        scratch_shapes=[pltpu.VMEM((N_EXPERTS, LANES), F32)],
        compiler_params=pltpu.CompilerParams(dimension_semantics=("arbitrary",), vmem_limit_bytes=VMEM_LIMIT),
        name="merge_route",
    )(x2, hg, attn, sr, sa, w_proj_rnn.astype(BF16), w_proj_attn.astype(BF16), w_out.astype(BF16),
      norm_ffn_g.reshape(1, D_MODEL), router_w.T, router_b.reshape(N_EXPERTS, 1))


def _row_copy(src_ref, src_row, dst_ref, dst_row, sem):
    return pltpu.make_async_copy(src_ref.at[pl.ds(src_row, 1), :], dst_ref.at[pl.ds(dst_row, 1), :], sem)


def _dispatch_body(dest_ref, hn_ref, zero_ref, rows_ref, sem):
    del zero_ref
    tm = hn_ref.shape[0]

    def issue(t, c):
        for k in range(TOP_K):
            _row_copy(hn_ref, t, rows_ref, dest_ref[k, t], sem).start()
        return c

    def drain(t, c):
        for k in range(TOP_K):
            _row_copy(hn_ref, t, rows_ref, dest_ref[k, t], sem).wait()
        return c

    lax.fori_loop(0, tm, issue, 0)
    lax.fori_loop(0, tm, drain, 0)


def _dispatch(dest, hn, n_rows):
    n_tok = hn.shape[0]
    tm = GATHER_TILE
    return pl.pallas_call(
        _dispatch_body,
        grid=(n_tok // tm,),
        in_specs=[pl.BlockSpec((TOP_K, tm), lambda i: (0, i), memory_space=pltpu.SMEM),
                  pl.BlockSpec((tm, D_MODEL), lambda i: (i, 0)),
                  pl.BlockSpec(memory_space=pl.ANY)],
        out_specs=pl.BlockSpec(memory_space=pl.ANY),
        out_shape=jax.ShapeDtypeStruct((n_rows, D_MODEL), F32),
        scratch_shapes=[pltpu.SemaphoreType.DMA(())],
        input_output_aliases={2: 0},
        compiler_params=pltpu.CompilerParams(dimension_semantics=("arbitrary",)),
        name="dispatch",
    )(dest, hn, jnp.zeros((n_rows, D_MODEL), F32))


def _expert_body(blk_expert_ref, n_used_ref, x_ref, wgu_ref, bgu_ref, wd_ref, bd_ref, y_ref):
    del blk_expert_ref
    live = pl.program_id(0) < n_used_ref[0]

    @pl.when(live)
    def _():
        gu = jnp.dot(x_ref[...].astype(BF16), wgu_ref[...], preferred_element_type=F32) + bgu_ref[...]
        x_glu = jnp.minimum(gu[:, :D_EXPERT], SWIGLU_LIMIT)
        x_lin = jnp.clip(gu[:, D_EXPERT:], -SWIGLU_LIMIT, SWIGLU_LIMIT)
        act = x_glu * jax.nn.sigmoid(SWIGLU_ALPHA * x_glu) * (x_lin + 1.0)
        y_ref[...] = jnp.dot(act.astype(BF16), wd_ref[...], preferred_element_type=F32) + bd_ref[...]

    @pl.when(jnp.logical_not(live))
    def _():
        y_ref[...] = jnp.zeros_like(y_ref)


def _experts(blk_expert, n_used, x_rows, w_gu, b_gu, w_down, b_down):
    n_rows = x_rows.shape[0]
    rb = EXPERT_ROWS
    grid_spec = pltpu.PrefetchScalarGridSpec(
        num_scalar_prefetch=2,
        grid=(n_rows // rb,),
        in_specs=[
            pl.BlockSpec((rb, D_MODEL), lambda i, be, nu: (i, 0)),
            pl.BlockSpec((None, D_MODEL, 2 * D_EXPERT), lambda i, be, nu: (be[i], 0, 0)),
            pl.BlockSpec((None, 1, 2 * D_EXPERT), lambda i, be, nu: (be[i], 0, 0)),
            pl.BlockSpec((None, D_EXPERT, D_MODEL), lambda i, be, nu: (be[i], 0, 0)),
            pl.BlockSpec((None, 1, D_MODEL), lambda i, be, nu: (be[i], 0, 0)),
        ],
        out_specs=pl.BlockSpec((rb, D_MODEL), lambda i, be, nu: (i, 0)),
    )
    return pl.pallas_call(
        _expert_body,
        grid_spec=grid_spec,
        out_shape=jax.ShapeDtypeStruct((n_rows, D_MODEL), F32),
        compiler_params=pltpu.CompilerParams(dimension_semantics=("arbitrary",), vmem_limit_bytes=VMEM_LIMIT),
        name="experts",
    )(blk_expert, n_used, x_rows, w_gu.astype(BF16), b_gu.reshape(N_EXPERTS, 1, -1),
      w_down.astype(BF16), b_down.reshape(N_EXPERTS, 1, -1))


def _combine_body(dest_ref, gate_ref, x1_ref, y_ref, gfin_ref, out_ref, ybuf, sem):
    tm = x1_ref.shape[0]

    def issue(t, c):
        for k in range(TOP_K):
            _row_copy(y_ref, dest_ref[k, t], ybuf.at[k], t, sem).start()
        return c

    def drain(t, c):
        for k in range(TOP_K):
            _row_copy(y_ref, dest_ref[k, t], ybuf.at[k], t, sem).wait()
        return c

    lax.fori_loop(0, tm, issue, 0)
    lax.fori_loop(0, tm, drain, 0)
    gates = gate_ref[...]
    x2 = x1_ref[...]
    for k in range(TOP_K):
        x2 = x2 + gates[:, k:k + 1] * ybuf[k]
    out_ref[...] = x2 * lax.rsqrt(jnp.mean(x2 * x2, axis=-1, keepdims=True) + EPS) * gfin_ref[...]


def _combine(dest, gates_tok, x1, y_rows, norm_final_g):
    n_tok = x1.shape[0]
    tm = GATHER_TILE
    row = pl.BlockSpec((tm, D_MODEL), lambda i: (i, 0))
    return pl.pallas_call(
        _combine_body,
        grid=(n_tok // tm,),
        in_specs=[pl.BlockSpec((TOP_K, tm), lambda i: (0, i), memory_space=pltpu.SMEM),
                  pl.BlockSpec((tm, TOP_K), lambda i: (i, 0)),
                  row,
                  pl.BlockSpec(memory_space=pl.ANY),
                  pl.BlockSpec((1, D_MODEL), lambda i: (0, 0))],
        out_specs=row,
        out_shape=jax.ShapeDtypeStruct((n_tok, D_MODEL), F32),
        scratch_shapes=[pltpu.VMEM((TOP_K, tm, D_MODEL), F32), pltpu.SemaphoreType.DMA(())],
        compiler_params=pltpu.CompilerParams(dimension_semantics=("arbitrary",), vmem_limit_bytes=VMEM_LIMIT),
        name="combine",
    )(dest, gates_tok, x1, y_rows, norm_final_g.reshape(1, D_MODEL))


def _routing_plan(top_idx, rank, counts, n_tok):
    padded = ((counts + EXPERT_ROWS - 1) // EXPERT_ROWS) * EXPERT_ROWS
    pend = jnp.cumsum(padded)
    pstart = pend - padded
    e_id = jnp.arange(N_EXPERTS, dtype=jnp.int32)[:, None, None]
    dest = jnp.sum(jnp.where(top_idx[None] == e_id, pstart[:, None, None], 0), axis=0) + rank
    n_rows = n_tok * TOP_K + N_EXPERTS * EXPERT_ROWS
    blk_start = jnp.arange(n_rows // EXPERT_ROWS, dtype=jnp.int32) * EXPERT_ROWS
    blk_expert = jnp.minimum(jnp.sum(blk_start[:, None] >= pend[None, :], axis=1), N_EXPERTS - 1)
    n_used = (pend[-1] // EXPERT_ROWS).reshape(1)
    return dest.astype(jnp.int32), blk_expert.astype(jnp.int32), n_used.astype(jnp.int32), n_rows


def kernel(x, norm_mix_g, w_in, conv_w, conv_b, lru_wx, lru_bx, lru_wa, lru_ba, lru_a_param, w_proj_rnn, w_proj_attn, w_out, norm_ffn_g, router_w, router_b, expert_w_gu, expert_b_gu, expert_w_down, expert_b_down, norm_final_g):
    bsz, seq, _ = x.shape
    n_tok = bsz * seq
    assert w_in.shape[0] == 1, "single-layer problem: the final RMSNorm is fused into the combine stage"
    assert seq % ROW_TILE == 0 and seq % MOBA_BLOCK == 0
    layer = 0
    x2 = x.reshape(n_tok, D_MODEL)
    rx, rg, q, k, v, sr, sa = _in_proj(x2, norm_mix_g[layer], w_in[layer], seq)
    hg = _rnn(rx, rg, conv_w[layer], conv_b[layer], lru_wx[layer], lru_bx[layer], lru_wa[layer],
              lru_ba[layer], lru_a_param[layer], bsz, seq)
    attn = _moba(q, k, v)
    x1, hn, top_idx, gates, rank, cnt = _merge(x2, hg, attn, sr, sa, w_proj_rnn[layer], w_proj_attn[layer],
                                               w_out[layer], norm_ffn_g[layer], router_w[layer],
                                               router_b[layer], seq)
    dest, blk_expert, n_used, n_rows = _routing_plan(top_idx, rank, cnt[:, 0], n_tok)
    x_rows = _dispatch(dest, hn, n_rows)
    y_rows = _experts(blk_expert, n_used, x_rows, expert_w_gu[layer], expert_b_gu[layer],
                      expert_w_down[layer], expert_b_down[layer])
    out = _combine(dest, gates.T, x1, y_rows, norm_final_g)
    return out.reshape(bsz, seq, D_MODEL)
```

```python
import functools
import math

import jax
import jax.numpy as jnp
from jax import lax
from jax.experimental import pallas as pl
from jax.experimental.pallas import tpu as pltpu

F32 = jnp.float32
BF16 = jnp.bfloat16

D_MODEL = 1024
RNN_WIDTH = 1024
RNN_BLOCKS = 16
RNN_BLOCK_DIM = RNN_WIDTH // RNN_BLOCKS
CONV_WIDTH = 4
LRU_C = 8.0
N_HEADS = 8
HEAD_DIM = 128
ROPE_DIM = HEAD_DIM // 4
ROPE_THETA = 500000.0
MOBA_BLOCK = 256
MOBA_TOPK = 3
N_EXPERTS = 32
TOP_K = 4
D_EXPERT = 1024
SWIGLU_LIMIT = 7.0
SWIGLU_ALPHA = 1.702
EXPERT_ROWS = 256
EPS = 1e-6
NEG = -1e30

V7X_VMEM_BYTES = 64 * 1024 * 1024
V7X_MXU_DIM = 256
SUBLANES = 8
LANES = 128
BF16_SUBLANES = 16

ROW_TILE = 512
GATHER_TILE = 256
ISSUE_UNROLL = 8
DMA_QUEUES = 2
LRU_GROUP = V7X_MXU_DIM // RNN_BLOCK_DIM
VMEM_LIMIT = V7X_VMEM_BYTES - 8 * 1024 * 1024


def _nt_dot(a, b, **kw):
    return lax.dot_general(a, b, (((1,), (1,)), ((), ())), preferred_element_type=F32, **kw)


ROW_TILES = D_MODEL // LANES
assert ROW_TILES == SUBLANES


def _store_row_tiles(ref, val):
    rows = val.shape[0]
    for s in range(ROW_TILES):
        ref[pl.ds(s, rows, stride=ROW_TILES), :] = val[:, s * LANES:(s + 1) * LANES]


def _load_row_tiles(ref):
    rows = ref.shape[0] // ROW_TILES
    return jnp.concatenate([ref[pl.ds(s, rows, stride=ROW_TILES), :] for s in range(ROW_TILES)], axis=1)


def _row_slab(ref, row):
    return ref.at[pl.ds(pl.multiple_of(row * ROW_TILES, ROW_TILES), ROW_TILES), :]


def _gelu_tanh(x):
    return 0.5 * x * (1.0 + jnp.tanh(math.sqrt(2.0 / math.pi) * (x + 0.044715 * (x * x * x))))


def _inproj_body(x_ref, g_ref, w_ref, rot_ref, rx_ref, rg_ref, q_ref, k_ref, v_ref, sr_ref, sa_ref):
    x = x_ref[...]
    ms = jnp.mean(x * x, axis=-1, keepdims=True)
    xn = (x * lax.rsqrt(ms + EPS) * g_ref[...]).astype(BF16)

    def proj(j):
        return jnp.dot(xn, w_ref[:, j * D_MODEL:(j + 1) * D_MODEL], preferred_element_type=F32)

    rx_ref[...] = proj(0).astype(BF16)
    rg_ref[...] = _gelu_tanh(proj(1)).astype(BF16)

    cos, sin_hi, sin_lo = rot_ref[0], rot_ref[1], rot_ref[2]
    half = ROPE_DIM // 2

    def rotary_heads(z, out_ref, scale):
        for h in range(N_HEADS):
            zh = z[:, h * HEAD_DIM:(h + 1) * HEAD_DIM]
            r = zh * cos + pltpu.roll(zh, HEAD_DIM - half, axis=1) * sin_hi + pltpu.roll(zh, half, axis=1) * sin_lo
            out_ref[h] = (r * scale).astype(BF16)

    rotary_heads(proj(2), q_ref, 1.0 / math.sqrt(HEAD_DIM))
    rotary_heads(proj(3), k_ref, 1.0)
    zv = proj(4)
    for h in range(N_HEADS):
        v_ref[h] = zv[:, h * HEAD_DIM:(h + 1) * HEAD_DIM].astype(BF16)
    sr_ref[...] = jax.nn.sigmoid(proj(5)).astype(BF16)
    sa_ref[...] = jax.nn.sigmoid(proj(6)).astype(BF16)


def _rotary_tables(seq):
    half = ROPE_DIM // 2
    inv_freq = ROPE_THETA ** (-jnp.arange(half, dtype=F32) * (2.0 / ROPE_DIM))
    ang = jnp.arange(seq, dtype=F32)[:, None] * inv_freq[None, :]
    cos, sin = jnp.cos(ang), jnp.sin(ang)
    pad = HEAD_DIM - ROPE_DIM
    cos_t = jnp.concatenate([cos, cos, jnp.ones((seq, pad), F32)], axis=1)
    sin_hi = jnp.concatenate([-sin, jnp.zeros((seq, half + pad), F32)], axis=1)
    sin_lo = jnp.concatenate([jnp.zeros((seq, half), F32), sin, jnp.zeros((seq, pad), F32)], axis=1)
    return jnp.stack([cos_t, sin_hi, sin_lo])


def _in_proj(x2, norm_g, w_in, seq):
    n_tok = x2.shape[0]
    tm = ROW_TILE
    tiles_per_seq = seq // tm
    in_cols = w_in.shape[1]
    row = pl.BlockSpec((tm, D_MODEL), lambda i: (i, 0))
    head = pl.BlockSpec((None, N_HEADS, tm, HEAD_DIM), lambda i: (i // tiles_per_seq, 0, i % tiles_per_seq, 0))
    bsz = n_tok // seq
    tok_bf16 = jax.ShapeDtypeStruct((n_tok, D_MODEL), BF16)
    head_bf16 = jax.ShapeDtypeStruct((bsz, N_HEADS, seq, HEAD_DIM), BF16)
    return pl.pallas_call(
        _inproj_body,
        grid=(n_tok // tm,),
        in_specs=[
            row,
            pl.BlockSpec((1, D_MODEL), lambda i: (0, 0)),
            pl.BlockSpec((D_MODEL, in_cols), lambda i: (0, 0), pipeline_mode=pl.Buffered(1)),
            pl.BlockSpec((3, tm, HEAD_DIM), lambda i: (0, i % tiles_per_seq, 0)),
        ],
        out_specs=[row, row, head, head, head, row, row],
        out_shape=[tok_bf16, tok_bf16, head_bf16, head_bf16, head_bf16, tok_bf16, tok_bf16],
        compiler_params=pltpu.CompilerParams(dimension_semantics=("parallel",), vmem_limit_bytes=VMEM_LIMIT),
        name="in_proj",
    )(x2, norm_g.reshape(1, D_MODEL), w_in.astype(BF16), _rotary_tables(seq))


def _rnn_body(rx_ref, halo_ref, rg_ref, cw_ref, cb_ref, wbd_ref, bx_ref, ba_ref, ap_ref, hg_ref,
              a_s, b_s, h_s, hprev_s):
    s = pl.program_id(1)
    ts = rx_ref.shape[0]
    x = rx_ref[...].astype(F32)
    halo = jnp.where(s == 0, 0.0, halo_ref[...].astype(F32))
    n_halo = halo.shape[0]
    xfull = jnp.concatenate([halo, x], axis=0)
    xc = cb_ref[...] + cw_ref[CONV_WIDTH - 1:CONV_WIDTH, :] * x
    for shift in range(1, CONV_WIDTH):
        tap = CONV_WIDTH - 1 - shift
        xc = xc + cw_ref[tap:tap + 1, :] * pltpu.roll(xfull, shift, axis=0)[n_halo:, :]

    xcb = xc.astype(BF16)
    gx, ga = [], []
    for g in range(RNN_WIDTH // V7X_MXU_DIM):
        r = jnp.dot(xcb[:, g * V7X_MXU_DIM:(g + 1) * V7X_MXU_DIM], wbd_ref[g], preferred_element_type=F32)
        gx.append(r[:, :V7X_MXU_DIM])
        ga.append(r[:, V7X_MXU_DIM:])
    gate_x = jax.nn.sigmoid(jnp.concatenate(gx, axis=1) + bx_ref[...])
    gate_a = jax.nn.sigmoid(jnp.concatenate(ga, axis=1) + ba_ref[...])
    z = -ap_ref[...]
    softplus = jnp.maximum(z, 0.0) + jnp.log1p(jnp.exp(-jnp.abs(z)))
    log_a = -LRU_C * gate_a * softplus
    a = jnp.exp(log_a)
    u = jnp.sqrt(-jnp.tanh(log_a) * (a * a + 1.0)) * (gate_x * xc)

    r8 = lax.broadcasted_iota(jnp.int32, (ts, 1), 0) % SUBLANES
    d = 1
    while d < SUBLANES:
        inside = r8 >= d
        a_prev = jnp.where(inside, pltpu.roll(a, d, axis=0), 1.0)
        u_prev = jnp.where(inside, pltpu.roll(u, d, axis=0), 0.0)
        u = a * u_prev + u
        a = a * a_prev
        d *= 2
    a_s[...] = a
    b_s[...] = u

    def chain(g, h_last):
        rows = pl.ds(pl.multiple_of(g * SUBLANES, SUBLANES), SUBLANES)
        h = b_s[rows, :] + a_s[rows, :] * h_last
        h_s[rows, :] = h
        return h[SUBLANES - 1:SUBLANES, :]

    h_first = jnp.where(s == 0, 0.0, hprev_s[...])
    hprev_s[...] = lax.fori_loop(0, ts // SUBLANES, chain, h_first)
    hg_ref[...] = (h_s[...] * rg_ref[...].astype(F32)).astype(BF16)


def _block_diag_gates(wx, wa):
    n_grp = RNN_BLOCKS // LRU_GROUP

    def bd(w):
        w = w.reshape(n_grp, LRU_GROUP, RNN_BLOCK_DIM, RNN_BLOCK_DIM)
        eye = jnp.eye(LRU_GROUP, dtype=w.dtype)
        full = jnp.einsum('gaij,ab->gaibj', w, eye)
        return full.reshape(n_grp, V7X_MXU_DIM, V7X_MXU_DIM)

    return jnp.concatenate([bd(wx), bd(wa)], axis=2).astype(BF16)


def _rnn(rx, rg, conv_w, conv_b, wx, bx, wa, ba, a_param, bsz, seq):
    ts = ROW_TILE
    tiles_per_seq = seq // ts
    halo_rows = BF16_SUBLANES
    halo_per_tile = ts // halo_rows
    row = pl.BlockSpec((ts, RNN_WIDTH), lambda b, s: (b * tiles_per_seq + s, 0))
    halo = pl.BlockSpec((halo_rows, RNN_WIDTH),
                        lambda b, s: (jnp.maximum((b * tiles_per_seq + s) * halo_per_tile - 1, 0), 0))
    vec = pl.BlockSpec((1, RNN_WIDTH), lambda b, s: (0, 0))
    n_grp = RNN_WIDTH // V7X_MXU_DIM
    return pl.pallas_call(
        _rnn_body,
        grid=(bsz, tiles_per_seq),
        in_specs=[row, halo, row,
                  pl.BlockSpec((CONV_WIDTH, RNN_WIDTH), lambda b, s: (0, 0)), vec,
                  pl.BlockSpec((n_grp, V7X_MXU_DIM, 2 * V7X_MXU_DIM), lambda b, s: (0, 0, 0)),
                  vec, vec, vec],
        out_specs=row,
        out_shape=jax.ShapeDtypeStruct((bsz * seq, RNN_WIDTH), BF16),
        scratch_shapes=[pltpu.VMEM((ts, RNN_WIDTH), F32), pltpu.VMEM((ts, RNN_WIDTH), F32),
                        pltpu.VMEM((ts, RNN_WIDTH), F32), pltpu.VMEM((1, RNN_WIDTH), F32)],
        compiler_params=pltpu.CompilerParams(dimension_semantics=("parallel", "arbitrary"),
                                             vmem_limit_bytes=VMEM_LIMIT),
        name="rnn",
    )(rx, rx, rg, conv_w, conv_b.reshape(1, -1), _block_diag_gates(wx, wa),
      bx.reshape(1, -1), ba.reshape(1, -1), a_param.reshape(1, -1))


def _moba_body(q_ref, k_ref, v_ref, o_ref, kaug_s):
    seq = q_ref.shape[0]
    nb = seq // MOBA_BLOCK
    nb_pad = -(-nb // SUBLANES) * SUBLANES
    kmean = jnp.mean(k_ref[...].astype(F32).reshape(nb, MOBA_BLOCK, HEAD_DIM), axis=1)
    if nb_pad > nb:
        kmean = jnp.concatenate([kmean, jnp.zeros((nb_pad - nb, HEAD_DIM), F32)], axis=0)

    key_blk = lax.broadcasted_iota(jnp.int32, (seq, LANES), 0) // MOBA_BLOCK
    lane = lax.broadcasted_iota(jnp.int32, (seq, LANES), 1)
    kaug_s[:, :HEAD_DIM] = k_ref[...]
    kaug_s[:, HEAD_DIM:] = (key_blk == lane).astype(BF16)

    r_id = lax.broadcasted_iota(jnp.int32, (MOBA_BLOCK, MOBA_BLOCK), 0)
    c_id = lax.broadcasted_iota(jnp.int32, (MOBA_BLOCK, MOBA_BLOCK), 1)
    causal = c_id <= r_id
    eye = (c_id == r_id).astype(BF16)
    j_id = lax.broadcasted_iota(jnp.int32, (nb_pad, MOBA_BLOCK), 0)

    for n in range(nb):
        qn = q_ref[n * MOBA_BLOCK:(n + 1) * MOBA_BLOCK, :]
        if n > MOBA_TOPK:
            gate = _nt_dot(kmean, qn.astype(F32), precision=lax.Precision.HIGHEST)
            rank = jnp.zeros((nb_pad, MOBA_BLOCK), F32)
            for jp in range(n):
                row = gate[jp:jp + 1, :]
                rank = rank + ((row > gate) | ((row == gate) & (jp < j_id))).astype(F32)
            keep = (rank < float(MOBA_TOPK)) | (j_id >= n)
            bias_t = jnp.where(keep, 0.0, NEG)
            bias_t = jnp.concatenate([bias_t, jnp.zeros((LANES - nb_pad, MOBA_BLOCK), F32)], axis=0).astype(BF16)
            bias = _nt_dot(eye, bias_t).astype(BF16)
        else:
            bias = jnp.zeros((MOBA_BLOCK, LANES), BF16)
        q_aug = jnp.concatenate([qn, bias], axis=1)
        n_keys = (n + 1) * MOBA_BLOCK
        s = _nt_dot(q_aug, kaug_s[0:n_keys, :])
        parts = [s[:, j * MOBA_BLOCK:(j + 1) * MOBA_BLOCK] for j in range(n + 1)]
        parts[n] = jnp.where(causal, parts[n], NEG)
        m = functools.reduce(jnp.maximum, parts)
        m = jnp.max(m, axis=1, keepdims=True)
        probs = [jnp.exp(part - m) for part in parts]
        l = jnp.sum(functools.reduce(jnp.add, probs), axis=1, keepdims=True)
        p_all = jnp.concatenate([p.astype(BF16) for p in probs], axis=1)
        acc = jnp.dot(p_all, v_ref[0:n_keys, :], preferred_element_type=F32)
        o_ref[n * MOBA_BLOCK:(n + 1) * MOBA_BLOCK, :] = (acc * (1.0 / l)).astype(BF16)


def _moba(q, k, v):
    bsz, n_heads, seq, hd = q.shape
    spec = pl.BlockSpec((None, None, seq, hd), lambda b, h: (b, h, 0, 0))
    return pl.pallas_call(
        _moba_body,
        grid=(bsz, n_heads),
        in_specs=[spec, spec, spec],
        out_specs=spec,
        out_shape=jax.ShapeDtypeStruct(q.shape, BF16),
        scratch_shapes=[pltpu.VMEM((seq, 2 * hd), BF16)],
        compiler_params=pltpu.CompilerParams(dimension_semantics=("parallel", "parallel"),
                                             vmem_limit_bytes=VMEM_LIMIT),
        name="moba",
    )(q, k, v)


def _merge_body(x_ref, hg_ref, at_ref, sr_ref, sa_ref, wpr_ref, wpa_ref, wo_ref, gffn_ref, rwt_ref, rb_ref,
                x1_ref, hn_ref, idx_ref, gate_ref, rank_ref, cnt_ref, carry_s):
    tm = x_ref.shape[0]

    @pl.when(pl.program_id(0) == 0)
    def _():
        carry_s[...] = jnp.zeros_like(carry_s)

    y_rnn = jnp.dot(hg_ref[...], wpr_ref[...], preferred_element_type=F32)
    attn = jnp.concatenate([at_ref[h] for h in range(N_HEADS)], axis=1)
    y_attn = jnp.dot(attn, wpa_ref[...], preferred_element_type=F32)
    mixed = sr_ref[...].astype(F32) * y_rnn + sa_ref[...].astype(F32) * y_attn
    x1 = x_ref[...] + jnp.dot(mixed.astype(BF16), wo_ref[...], preferred_element_type=F32)
    x1_ref[...] = x1
    hn = x1 * lax.rsqrt(jnp.mean(x1 * x1, axis=-1, keepdims=True) + EPS) * gffn_ref[...]
    _store_row_tiles(hn_ref, hn)

    logits = _nt_dot(rwt_ref[...], hn, precision=lax.Precision.HIGHEST) + rb_ref[...]
    e_id = lax.broadcasted_iota(jnp.int32, (N_EXPERTS, tm), 0)
    vals = logits
    onehots, top_vals, top_ids = [], [], []
    for _ in range(TOP_K):
        best = jnp.max(vals, axis=0, keepdims=True)
        best_id = jnp.min(jnp.where(vals == best, e_id, N_EXPERTS), axis=0, keepdims=True)
        hit = e_id == best_id
        vals = jnp.where(hit, -jnp.inf, vals)
        onehots.append(hit)
        top_vals.append(best)
        top_ids.append(best_id)
    exps = [jnp.exp(v - top_vals[0]) for v in top_vals]
    denom = exps[0] + exps[1] + exps[2] + exps[3]
    idx_ref[...] = jnp.concatenate(top_ids, axis=0)
    gate_ref[...] = jnp.concatenate([e / denom for e in exps], axis=0)

    chosen = onehots[0] | onehots[1] | onehots[2] | onehots[3]
    t_row = lax.broadcasted_iota(jnp.int32, (tm, tm), 0)
    t_col = lax.broadcasted_iota(jnp.int32, (tm, tm), 1)
    before = (t_row < t_col).astype(BF16)
    chosen_f = chosen.astype(F32)
    prior = jnp.dot(chosen_f.astype(BF16), before, preferred_element_type=F32) + carry_s[:, 0:1]
    ranks = [jnp.sum(jnp.where(hit, prior, 0.0), axis=0, keepdims=True) for hit in onehots]
    rank_ref[...] = jnp.concatenate(ranks, axis=0).astype(jnp.int32)
    carry_s[...] = carry_s[...] + jnp.sum(chosen_f, axis=1, keepdims=True)
    cnt_ref[...] = carry_s[...].astype(jnp.int32)


def _merge(x2, hg, attn, sr, sa, w_proj_rnn, w_proj_attn, w_out, norm_ffn_g, router_w, router_b, seq):
    n_tok = x2.shape[0]
    tm = ROW_TILE
    tiles_per_seq = seq // tm
    row = pl.BlockSpec((tm, D_MODEL), lambda i: (i, 0))
    head = pl.BlockSpec((None, N_HEADS, tm, HEAD_DIM), lambda i: (i // tiles_per_seq, 0, i % tiles_per_seq, 0))
    mat = pl.BlockSpec((D_MODEL, D_MODEL), lambda i: (0, 0))
    topk = pl.BlockSpec((TOP_K, tm), lambda i: (0, i))
    return pl.pallas_call(
        _merge_body,
        grid=(n_tok // tm,),
        in_specs=[row, row, head, row, row, mat, mat, mat,
                  pl.BlockSpec((1, D_MODEL), lambda i: (0, 0)),
                  pl.BlockSpec((N_EXPERTS, D_MODEL), lambda i: (0, 0)),
                  pl.BlockSpec((N_EXPERTS, 1), lambda i: (0, 0))],
        out_specs=[row, pl.BlockSpec((tm * ROW_TILES, LANES), lambda i: (i, 0)), topk, topk, topk,
                   pl.BlockSpec((N_EXPERTS, LANES), lambda i: (0, 0))],
        out_shape=[jax.ShapeDtypeStruct((n_tok, D_MODEL), F32),
                   jax.ShapeDtypeStruct((n_tok * ROW_TILES, LANES), F32),
                   jax.ShapeDtypeStruct((TOP_K, n_tok), jnp.int32), jax.ShapeDtypeStruct((TOP_K, n_tok), F32),
                   jax.ShapeDtypeStruct((TOP_K, n_tok), jnp.int32),
                   jax.ShapeDtypeStruct((N_EXPERTS, LANES), jnp.int32)],
        scratch_shapes=[pltpu.VMEM((N_EXPERTS, LANES), F32)],
        compiler_params=pltpu.CompilerParams(dimension_semantics=("arbitrary",), vmem_limit_bytes=VMEM_LIMIT),
        name="merge_route",
    )(x2, hg, attn, sr, sa, w_proj_rnn.astype(BF16), w_proj_attn.astype(BF16), w_out.astype(BF16),
      norm_ffn_g.reshape(1, D_MODEL), router_w.T, router_b.reshape(N_EXPERTS, 1))


def _row_copy(src_ref, src_row, dst_ref, dst_row, sem):
    return pltpu.make_async_copy(_row_slab(src_ref, src_row), _row_slab(dst_ref, dst_row), sem)


def _start_row_copies(copy_of, tm):
    def group(g, c):
        for i in range(ISSUE_UNROLL):
            for k in range(TOP_K):
                copy_of(g * ISSUE_UNROLL + i, k).start(priority=(i * TOP_K + k) % DMA_QUEUES)
        return c

    lax.fori_loop(0, tm // ISSUE_UNROLL, group, 0)


def _dispatch_body(pad_off_ref, pad_len_ref, dest_ref, hn_ref, rows_ref, sem, pad_sem):
    tm = hn_ref.shape[0] // ROW_TILES
    first = pl.program_id(0) == 0

    def for_each_pad_row(act):
        def per_expert(e, c):
            off = pad_off_ref[e]

            def per_row(r, c2):
                act(_row_copy(hn_ref, 0, rows_ref, off + r, pad_sem))
                return c2

            return lax.fori_loop(0, pad_len_ref[e], per_row, c)

        lax.fori_loop(0, N_EXPERTS, per_expert, 0)

    @pl.when(first)
    def _():
        for_each_pad_row(lambda cp: cp.start())

    _start_row_copies(lambda t, k: _row_copy(hn_ref, t, rows_ref, dest_ref[k, t], sem), tm)
    for _ in range(TOP_K):
        pltpu.make_async_copy(hn_ref, rows_ref.at[pl.ds(0, tm * ROW_TILES), :], sem).wait()

    @pl.when(first)
    def _():
        for_each_pad_row(lambda cp: cp.wait())


def _dispatch(dest, pad_off, pad_len, hn, n_rows):
    n_tok = hn.shape[0] // ROW_TILES
    tm = GATHER_TILE
    grid_spec = pltpu.PrefetchScalarGridSpec(
        num_scalar_prefetch=2,
        grid=(n_tok // tm,),
        in_specs=[pl.BlockSpec((TOP_K, tm), lambda i, po, pn: (0, i), memory_space=pltpu.SMEM),
                  pl.BlockSpec((tm * ROW_TILES, LANES), lambda i, po, pn: (i, 0))],
        out_specs=pl.BlockSpec(memory_space=pl.ANY),
        scratch_shapes=[pltpu.SemaphoreType.DMA(()), pltpu.SemaphoreType.DMA(())],
    )
    return pl.pallas_call(
        _dispatch_body,
        grid_spec=grid_spec,
        out_shape=jax.ShapeDtypeStruct((n_rows * ROW_TILES, LANES), F32),
        compiler_params=pltpu.CompilerParams(dimension_semantics=("arbitrary",), disable_bounds_checks=True),
        name="dispatch",
    )(pad_off, pad_len, dest, hn)


def _expert_body(blk_expert_ref, n_used_ref, x_ref, wgu_ref, bgu_ref, wd_ref, bd_ref, y_ref):
    del blk_expert_ref
    live = pl.program_id(0) < n_used_ref[0]

    @pl.when(live)
    def _():
        gu = jnp.dot(_load_row_tiles(x_ref).astype(BF16), wgu_ref[...], preferred_element_type=F32) + bgu_ref[...]
        x_glu = jnp.minimum(gu[:, :D_EXPERT], SWIGLU_LIMIT)
        x_lin = jnp.clip(gu[:, D_EXPERT:], -SWIGLU_LIMIT, SWIGLU_LIMIT)
        act = x_glu * jax.nn.sigmoid(SWIGLU_ALPHA * x_glu) * (x_lin + 1.0)
        _store_row_tiles(y_ref, jnp.dot(act.astype(BF16), wd_ref[...], preferred_element_type=F32) + bd_ref[...])

    @pl.when(jnp.logical_not(live))
    def _():
        y_ref[...] = jnp.zeros_like(y_ref)


def _experts(blk_expert, n_used, x_rows, w_gu, b_gu, w_down, b_down):
    n_rows = x_rows.shape[0] // ROW_TILES
    rb = EXPERT_ROWS
    grid_spec = pltpu.PrefetchScalarGridSpec(
        num_scalar_prefetch=2,
        grid=(n_rows // rb,),
        in_specs=[
            pl.BlockSpec((rb * ROW_TILES, LANES), lambda i, be, nu: (jnp.minimum(i, nu[0] - 1), 0)),
            pl.BlockSpec((None, D_MODEL, 2 * D_EXPERT), lambda i, be, nu: (be[i], 0, 0)),
            pl.BlockSpec((None, 1, 2 * D_EXPERT), lambda i, be, nu: (be[i], 0, 0)),
            pl.BlockSpec((None, D_EXPERT, D_MODEL), lambda i, be, nu: (be[i], 0, 0)),
            pl.BlockSpec((None, 1, D_MODEL), lambda i, be, nu: (be[i], 0, 0)),
        ],
        out_specs=pl.BlockSpec((rb * ROW_TILES, LANES), lambda i, be, nu: (i, 0)),
    )
    return pl.pallas_call(
        _expert_body,
        grid_spec=grid_spec,
        out_shape=jax.ShapeDtypeStruct((n_rows * ROW_TILES, LANES), F32),
        compiler_params=pltpu.CompilerParams(dimension_semantics=("arbitrary",), vmem_limit_bytes=VMEM_LIMIT),
        name="experts",
    )(blk_expert, n_used, x_rows, w_gu.astype(BF16), b_gu.reshape(N_EXPERTS, 1, -1),
      w_down.astype(BF16), b_down.reshape(N_EXPERTS, 1, -1))


def _combine_body(dest_ref, dest_next_ref, gate_ref, x1_ref, y_ref, gfin_ref, out_ref, ybuf, sems):
    tm = x1_ref.shape[0]
    step = pl.program_id(0)
    slot = step % 2

    def gather(d_ref, into):
        _start_row_copies(lambda t, k: _row_copy(y_ref, d_ref[k, t], ybuf.at[into, k], t, sems.at[into]), tm)

    @pl.when(step == 0)
    def _():
        gather(dest_ref, 0)

    @pl.when(step + 1 < pl.num_programs(0))
    def _():
        gather(dest_next_ref, 1 - slot)

    for k in range(TOP_K):
        pltpu.make_async_copy(y_ref.at[pl.ds(0, tm * ROW_TILES), :], ybuf.at[slot, k], sems.at[slot]).wait()
    gates = gate_ref[...]
    x2 = x1_ref[...]
    for k in range(TOP_K):
        x2 = x2 + gates[:, k:k + 1] * _load_row_tiles(ybuf.at[slot, k])
    out_ref[...] = x2 * lax.rsqrt(jnp.mean(x2 * x2, axis=-1, keepdims=True) + EPS) * gfin_ref[...]


def _combine(dest, gates_tok, x1, y_rows, norm_final_g):
    n_tok = x1.shape[0]
    tm = GATHER_TILE
    row = pl.BlockSpec((tm, D_MODEL), lambda i: (i, 0))
    n_tiles = n_tok // tm
    return pl.pallas_call(
        _combine_body,
        grid=(n_tiles,),
        in_specs=[pl.BlockSpec((TOP_K, tm), lambda i: (0, i), memory_space=pltpu.SMEM),
                  pl.BlockSpec((TOP_K, tm), lambda i: (0, jnp.minimum(i + 1, n_tiles - 1)), memory_space=pltpu.SMEM),
                  pl.BlockSpec((tm, TOP_K), lambda i: (i, 0)),
                  row,
                  pl.BlockSpec(memory_space=pl.ANY),
                  pl.BlockSpec((1, D_MODEL), lambda i: (0, 0))],
        out_specs=row,
        out_shape=jax.ShapeDtypeStruct((n_tok, D_MODEL), F32),
        scratch_shapes=[pltpu.VMEM((2, TOP_K, tm * ROW_TILES, LANES), F32), pltpu.SemaphoreType.DMA((2,))],
        compiler_params=pltpu.CompilerParams(dimension_semantics=("arbitrary",), vmem_limit_bytes=VMEM_LIMIT,
                                             disable_bounds_checks=True),
        name="combine",
    )(dest, dest, gates_tok, x1, y_rows, norm_final_g.reshape(1, D_MODEL))


def _routing_plan(top_idx, rank, counts, n_tok):
    padded = ((counts + EXPERT_ROWS - 1) // EXPERT_ROWS) * EXPERT_ROWS
    pend = jnp.cumsum(padded)
    pstart = pend - padded
    e_id = jnp.arange(N_EXPERTS, dtype=jnp.int32)[:, None, None]
    dest = jnp.sum(jnp.where(top_idx[None] == e_id, pstart[:, None, None], 0), axis=0) + rank
    n_rows = n_tok * TOP_K + N_EXPERTS * EXPERT_ROWS
    blk_start = jnp.arange(n_rows // EXPERT_ROWS, dtype=jnp.int32) * EXPERT_ROWS
    blk_expert = jnp.minimum(jnp.sum(blk_start[:, None] >= pend[None, :], axis=1), N_EXPERTS - 1)
    n_used = (pend[-1] // EXPERT_ROWS).reshape(1)
    pad_off, pad_len = pstart + counts, padded - counts
    return (dest.astype(jnp.int32), blk_expert.astype(jnp.int32), n_used.astype(jnp.int32),
            pad_off.astype(jnp.int32), pad_len.astype(jnp.int32), n_rows)


def kernel(x, norm_mix_g, w_in, conv_w, conv_b, lru_wx, lru_bx, lru_wa, lru_ba, lru_a_param, w_proj_rnn, w_proj_attn, w_out, norm_ffn_g, router_w, router_b, expert_w_gu, expert_b_gu, expert_w_down, expert_b_down, norm_final_g):
    bsz, seq, _ = x.shape
    n_tok = bsz * seq
    assert w_in.shape[0] == 1, "single-layer problem: the final RMSNorm is fused into the combine stage"
    assert seq % ROW_TILE == 0 and seq % MOBA_BLOCK == 0
    layer = 0
    x2 = x.reshape(n_tok, D_MODEL)
    rx, rg, q, k, v, sr, sa = _in_proj(x2, norm_mix_g[layer], w_in[layer], seq)
    hg = _rnn(rx, rg, conv_w[layer], conv_b[layer], lru_wx[layer], lru_bx[layer], lru_wa[layer],
              lru_ba[layer], lru_a_param[layer], bsz, seq)
    attn = _moba(q, k, v)
    x1, hn, top_idx, gates, rank, cnt = _merge(x2, hg, attn, sr, sa, w_proj_rnn[layer], w_proj_attn[layer],
                                               w_out[layer], norm_ffn_g[layer], router_w[layer],
                                               router_b[layer], seq)
    dest, blk_expert, n_used, pad_off, pad_len, n_rows = _routing_plan(top_idx, rank, cnt[:, 0], n_tok)
    x_rows = _dispatch(dest, pad_off, pad_len, hn, n_rows)
    y_rows = _experts(blk_expert, n_used, x_rows, expert_w_gu[layer], expert_b_gu[layer],
                      expert_w_down[layer], expert_b_down[layer])
    out = _combine(dest, gates.T, x1, y_rows, norm_final_g)
    return out.reshape(bsz, seq, D_MODEL)
```

```python
import functools
import math

import jax
import jax.numpy as jnp
from jax import lax
from jax.experimental import pallas as pl
from jax.experimental.pallas import tpu as pltpu

F32 = jnp.float32
BF16 = jnp.bfloat16

D_MODEL = 1024
RNN_WIDTH = 1024
RNN_BLOCKS = 16
RNN_BLOCK_DIM = RNN_WIDTH // RNN_BLOCKS
CONV_WIDTH = 4
LRU_C = 8.0
N_HEADS = 8
HEAD_DIM = 128
ROPE_DIM = HEAD_DIM // 4
ROPE_THETA = 500000.0
MOBA_BLOCK = 256
MOBA_TOPK = 3
N_EXPERTS = 32
TOP_K = 4
D_EXPERT = 1024
SWIGLU_LIMIT = 7.0
SWIGLU_ALPHA = 1.702
EPS = 1e-6
NEG = -1e30

V7X_VMEM_BYTES = 64 * 1024 * 1024
V7X_MXU_DIM = 256
SUBLANES = 8
LANES = 128
BF16_SUBLANES = 16

ROW_TILE = 512
EXPERT_BLOCK_ROWS = 512
WEIGHT_CAST_ROWS = 128
DISPATCH_TILE = 1024
GATHER_TILE = 256
ISSUE_UNROLL = 8
DMA_QUEUES = 2
LRU_GROUP = V7X_MXU_DIM // RNN_BLOCK_DIM
VMEM_LIMIT = V7X_VMEM_BYTES - 8 * 1024 * 1024


def _nt_dot(a, b, **kw):
    return lax.dot_general(a, b, (((1,), (1,)), ((), ())), preferred_element_type=F32, **kw)


ROW_TILES = D_MODEL // LANES
assert ROW_TILES == SUBLANES


def _store_row_tiles(ref, val):
    rows = val.shape[0]
    for s in range(ROW_TILES):
        ref[pl.ds(s, rows, stride=ROW_TILES), :] = val[:, s * LANES:(s + 1) * LANES]


def _load_row_tiles(ref, first_row=0, rows=None):
    rows = ref.shape[0] // ROW_TILES if rows is None else rows
    base = first_row * ROW_TILES
    return jnp.concatenate([ref[pl.ds(base + s, rows, stride=ROW_TILES), :] for s in range(ROW_TILES)], axis=1)


def _row_slab(ref, row):
    return ref.at[pl.ds(pl.multiple_of(row * ROW_TILES, ROW_TILES), ROW_TILES), :]


def _gelu_tanh(x):
    return 0.5 * x * (1.0 + jnp.tanh(math.sqrt(2.0 / math.pi) * (x + 0.044715 * (x * x * x))))


def _inproj_body(x_ref, g_ref, w_ref, rot_ref, rx_ref, rg_ref, q_ref, k_ref, v_ref, sr_ref, sa_ref):
    x = x_ref[...]
    ms = jnp.mean(x * x, axis=-1, keepdims=True)
    xn = (x * lax.rsqrt(ms + EPS) * g_ref[...]).astype(BF16)

    def proj(j):
        return jnp.dot(xn, w_ref[:, j * D_MODEL:(j + 1) * D_MODEL], preferred_element_type=F32)

    rx_ref[...] = proj(0).astype(BF16)
    rg_ref[...] = _gelu_tanh(proj(1)).astype(BF16)

    cos, sin_hi, sin_lo = rot_ref[0], rot_ref[1], rot_ref[2]
    half = ROPE_DIM // 2

    def rotary_heads(z, out_ref, scale):
        for h in range(N_HEADS):
            zh = z[:, h * HEAD_DIM:(h + 1) * HEAD_DIM]
            r = zh * cos + pltpu.roll(zh, HEAD_DIM - half, axis=1) * sin_hi + pltpu.roll(zh, half, axis=1) * sin_lo
            out_ref[h] = (r * scale).astype(BF16)

    rotary_heads(proj(2), q_ref, 1.0 / math.sqrt(HEAD_DIM))
    rotary_heads(proj(3), k_ref, 1.0)
    zv = proj(4)
    for h in range(N_HEADS):
        v_ref[h] = zv[:, h * HEAD_DIM:(h + 1) * HEAD_DIM].astype(BF16)
    sr_ref[...] = jax.nn.sigmoid(proj(5)).astype(BF16)
    sa_ref[...] = jax.nn.sigmoid(proj(6)).astype(BF16)


def _rotary_tables(seq):
    half = ROPE_DIM // 2
    inv_freq = ROPE_THETA ** (-jnp.arange(half, dtype=F32) * (2.0 / ROPE_DIM))
    ang = jnp.arange(seq, dtype=F32)[:, None] * inv_freq[None, :]
    cos, sin = jnp.cos(ang), jnp.sin(ang)
    pad = HEAD_DIM - ROPE_DIM
    cos_t = jnp.concatenate([cos, cos, jnp.ones((seq, pad), F32)], axis=1)
    sin_hi = jnp.concatenate([-sin, jnp.zeros((seq, half + pad), F32)], axis=1)
    sin_lo = jnp.concatenate([jnp.zeros((seq, half), F32), sin, jnp.zeros((seq, pad), F32)], axis=1)
    return jnp.stack([cos_t, sin_hi, sin_lo])


def _in_proj(x2, norm_g, w_in, seq):
    n_tok = x2.shape[0]
    tm = ROW_TILE
    tiles_per_seq = seq // tm
    in_cols = w_in.shape[1]
    row = pl.BlockSpec((tm, D_MODEL), lambda i: (i, 0))
    head = pl.BlockSpec((None, N_HEADS, tm, HEAD_DIM), lambda i: (i // tiles_per_seq, 0, i % tiles_per_seq, 0))
    bsz = n_tok // seq
    tok_bf16 = jax.ShapeDtypeStruct((n_tok, D_MODEL), BF16)
    head_bf16 = jax.ShapeDtypeStruct((bsz, N_HEADS, seq, HEAD_DIM), BF16)
    return pl.pallas_call(
        _inproj_body,
        grid=(n_tok // tm,),
        in_specs=[
            row,
            pl.BlockSpec((1, D_MODEL), lambda i: (0, 0)),
            pl.BlockSpec((D_MODEL, in_cols), lambda i: (0, 0), pipeline_mode=pl.Buffered(1)),
            pl.BlockSpec((3, tm, HEAD_DIM), lambda i: (0, i % tiles_per_seq, 0)),
        ],
        out_specs=[row, row, head, head, head, row, row],
        out_shape=[tok_bf16, tok_bf16, head_bf16, head_bf16, head_bf16, tok_bf16, tok_bf16],
        compiler_params=pltpu.CompilerParams(dimension_semantics=("parallel",), vmem_limit_bytes=VMEM_LIMIT),
        name="in_proj",
    )(x2, norm_g.reshape(1, D_MODEL), w_in.astype(BF16), _rotary_tables(seq))


def _rnn_body(rx_ref, halo_ref, rg_ref, cw_ref, cb_ref, wbd_ref, bx_ref, ba_ref, ap_ref, hg_ref,
              xe_s, a_s, u_s, acum_s, ucum_s, h_s, hprev_s):
    s = pl.program_id(1)
    ts = rx_ref.shape[0]
    n_halo = halo_ref.shape[0]
    xe_s[0:n_halo, :] = jnp.where(s == 0, 0.0, halo_ref[...].astype(F32))
    xe_s[n_halo:, :] = rx_ref[...].astype(F32)
    xc = cb_ref[...]
    for tap in range(CONV_WIDTH):
        xc = xc + cw_ref[tap:tap + 1, :] * xe_s[pl.ds(n_halo - (CONV_WIDTH - 1) + tap, ts), :]

    xcb = xc.astype(BF16)
    gx, ga = [], []
    for g in range(RNN_WIDTH // V7X_MXU_DIM):
        r = jnp.dot(xcb[:, g * V7X_MXU_DIM:(g + 1) * V7X_MXU_DIM], wbd_ref[g], preferred_element_type=F32)
        gx.append(r[:, :V7X_MXU_DIM])
        ga.append(r[:, V7X_MXU_DIM:])
    gate_x = jax.nn.sigmoid(jnp.concatenate(gx, axis=1) + bx_ref[...])
    gate_a = jax.nn.sigmoid(jnp.concatenate(ga, axis=1) + ba_ref[...])
    z = -ap_ref[...]
    softplus = jnp.maximum(z, 0.0) + jnp.log1p(jnp.exp(-jnp.abs(z)))
    log_a = -LRU_C * gate_a * softplus
    a = jnp.exp(log_a)
    u = jnp.sqrt(-jnp.tanh(log_a) * (a * a + 1.0)) * (gate_x * xc)

    n_slab = a_s.shape[0]
    n_grp = ts // SUBLANES
    for c in range(n_slab):
        a_s[c] = a[:, c * LANES:(c + 1) * LANES]
        u_s[c] = u[:, c * LANES:(c + 1) * LANES]
    for c in range(n_slab):
        a_cum = u_cum = None
        for j in range(SUBLANES):
            rows_j = pl.ds(j, n_grp, stride=SUBLANES)
            a_j, u_j = a_s[c, rows_j, :], u_s[c, rows_j, :]
            a_cum, u_cum = (a_j, u_j) if j == 0 else (a_j * a_cum, a_j * u_cum + u_j)
            acum_s[c, rows_j, :] = a_cum
            ucum_s[c, rows_j, :] = u_cum

    def chain(g, h_last):
        rows = pl.ds(pl.multiple_of(g * SUBLANES, SUBLANES), SUBLANES)
        last = []
        for c in range(n_slab):
            h = ucum_s[c, rows, :] + acum_s[c, rows, :] * h_last[c]
            h_s[c, rows, :] = h
            last.append(h[SUBLANES - 1:SUBLANES, :])
        return tuple(last)

    h_first = jnp.where(s == 0, 0.0, hprev_s[...])
    h_last = lax.fori_loop(0, n_grp, chain,
                           tuple(h_first[:, c * LANES:(c + 1) * LANES] for c in range(n_slab)))
    hprev_s[...] = jnp.concatenate(h_last, axis=1)
    h_all = jnp.concatenate([h_s[c] for c in range(n_slab)], axis=1)
    hg_ref[...] = (h_all * rg_ref[...].astype(F32)).astype(BF16)


def _block_diag_gates(wx, wa):
    n_grp = RNN_BLOCKS // LRU_GROUP

    def bd(w):
        w = w.reshape(n_grp, LRU_GROUP, RNN_BLOCK_DIM, RNN_BLOCK_DIM)
        eye = jnp.eye(LRU_GROUP, dtype=w.dtype)
        full = jnp.einsum('gaij,ab->gaibj', w, eye)
        return full.reshape(n_grp, V7X_MXU_DIM, V7X_MXU_DIM)

    return jnp.concatenate([bd(wx), bd(wa)], axis=2).astype(BF16)


def _rnn(rx, rg, conv_w, conv_b, wx, bx, wa, ba, a_param, bsz, seq):
    ts = ROW_TILE
    tiles_per_seq = seq // ts
    halo_rows = BF16_SUBLANES
    halo_per_tile = ts // halo_rows
    row = pl.BlockSpec((ts, RNN_WIDTH), lambda b, s: (b * tiles_per_seq + s, 0))
    halo = pl.BlockSpec((halo_rows, RNN_WIDTH),
                        lambda b, s: (jnp.maximum((b * tiles_per_seq + s) * halo_per_tile - 1, 0), 0))
    vec = pl.BlockSpec((1, RNN_WIDTH), lambda b, s: (0, 0))
    n_grp = RNN_WIDTH // V7X_MXU_DIM
    return pl.pallas_call(
        _rnn_body,
        grid=(bsz, tiles_per_seq),
        in_specs=[row, halo, row,
                  pl.BlockSpec((CONV_WIDTH, RNN_WIDTH), lambda b, s: (0, 0)), vec,
                  pl.BlockSpec((n_grp, V7X_MXU_DIM, 2 * V7X_MXU_DIM), lambda b, s: (0, 0, 0)),
                  vec, vec, vec],
        out_specs=row,
        out_shape=jax.ShapeDtypeStruct((bsz * seq, RNN_WIDTH), BF16),
        scratch_shapes=[pltpu.VMEM((halo_rows + ts, RNN_WIDTH), F32)]
        + [pltpu.VMEM((RNN_WIDTH // LANES, ts, LANES), F32)] * 5
        + [pltpu.VMEM((1, RNN_WIDTH), F32)],
        compiler_params=pltpu.CompilerParams(dimension_semantics=("parallel", "arbitrary"),
                                             vmem_limit_bytes=VMEM_LIMIT),
        name="rnn",
    )(rx, rx, rg, conv_w, conv_b.reshape(1, -1), _block_diag_gates(wx, wa),
      bx.reshape(1, -1), ba.reshape(1, -1), a_param.reshape(1, -1))


def _moba_body(q_ref, k_ref, v_ref, o_ref, kaug_s):
    seq = q_ref.shape[0]
    nb = seq // MOBA_BLOCK
    nb_pad = -(-nb // SUBLANES) * SUBLANES
    kmean = jnp.mean(k_ref[...].astype(F32).reshape(nb, MOBA_BLOCK, HEAD_DIM), axis=1)
    if nb_pad > nb:
        kmean = jnp.concatenate([kmean, jnp.zeros((nb_pad - nb, HEAD_DIM), F32)], axis=0)

    key_blk = lax.broadcasted_iota(jnp.int32, (seq, LANES), 0) // MOBA_BLOCK
    lane = lax.broadcasted_iota(jnp.int32, (seq, LANES), 1)
    kaug_s[:, :HEAD_DIM] = k_ref[...]
    kaug_s[:, HEAD_DIM:] = (key_blk == lane).astype(BF16)

    r_id = lax.broadcasted_iota(jnp.int32, (MOBA_BLOCK, MOBA_BLOCK), 0)
    c_id = lax.broadcasted_iota(jnp.int32, (MOBA_BLOCK, MOBA_BLOCK), 1)
    causal = c_id <= r_id
    eye = (c_id == r_id).astype(BF16)
    j_id = lax.broadcasted_iota(jnp.int32, (nb_pad, MOBA_BLOCK), 0)

    for n in range(nb):
        qn = q_ref[n * MOBA_BLOCK:(n + 1) * MOBA_BLOCK, :]
        if n > MOBA_TOPK:
            gate = _nt_dot(kmean, qn.astype(F32), precision=lax.Precision.HIGHEST)
            rank = jnp.zeros((nb_pad, MOBA_BLOCK), F32)
            for jp in range(n):
                row = gate[jp:jp + 1, :]
                rank = rank + ((row > gate) | ((row == gate) & (jp < j_id))).astype(F32)
            keep = (rank < float(MOBA_TOPK)) | (j_id >= n)
            bias_t = jnp.where(keep, 0.0, NEG)
            bias_t = jnp.concatenate([bias_t, jnp.zeros((LANES - nb_pad, MOBA_BLOCK), F32)], axis=0).astype(BF16)
            bias = _nt_dot(eye, bias_t).astype(BF16)
        else:
            bias = jnp.zeros((MOBA_BLOCK, LANES), BF16)
        q_aug = jnp.concatenate([qn, bias], axis=1)
        n_keys = (n + 1) * MOBA_BLOCK
        s = _nt_dot(q_aug, kaug_s[0:n_keys, :])
        parts = [s[:, j * MOBA_BLOCK:(j + 1) * MOBA_BLOCK] for j in range(n + 1)]
        parts[n] = jnp.where(causal, parts[n], NEG)
        m = functools.reduce(jnp.maximum, parts)
        m = jnp.max(m, axis=1, keepdims=True)
        probs = [jnp.exp(part - m) for part in parts]
        l = jnp.sum(functools.reduce(jnp.add, probs), axis=1, keepdims=True)
        p_all = jnp.concatenate([p.astype(BF16) for p in probs], axis=1)
        acc = jnp.dot(p_all, v_ref[0:n_keys, :], preferred_element_type=F32)
        o_ref[n * MOBA_BLOCK:(n + 1) * MOBA_BLOCK, :] = (acc * (1.0 / l)).astype(BF16)


def _moba(q, k, v):
    bsz, n_heads, seq, hd = q.shape
    spec = pl.BlockSpec((None, None, seq, hd), lambda b, h: (b, h, 0, 0))
    return pl.pallas_call(
        _moba_body,
        grid=(bsz, n_heads),
        in_specs=[spec, spec, spec],
        out_specs=spec,
        out_shape=jax.ShapeDtypeStruct(q.shape, BF16),
        scratch_shapes=[pltpu.VMEM((seq, 2 * hd), BF16)],
        compiler_params=pltpu.CompilerParams(dimension_semantics=("parallel", "parallel"),
                                             vmem_limit_bytes=VMEM_LIMIT),
        name="moba",
    )(q, k, v)


def _merge_body(x_ref, hg_ref, at_ref, sr_ref, sa_ref, wpr_ref, wpa_ref, wo_ref, gffn_ref, rwt_ref, rb_ref,
                x1_ref, hn_ref, idx_ref, gate_ref, rank_ref, cnt_ref, carry_s):
    tm = x_ref.shape[0]

    @pl.when(pl.program_id(0) == 0)
    def _():
        carry_s[...] = jnp.zeros_like(carry_s)

    y_rnn = jnp.dot(hg_ref[...], wpr_ref[...], preferred_element_type=F32)
    attn = jnp.concatenate([at_ref[h] for h in range(N_HEADS)], axis=1)
    y_attn = jnp.dot(attn, wpa_ref[...], preferred_element_type=F32)
    mixed = sr_ref[...].astype(F32) * y_rnn + sa_ref[...].astype(F32) * y_attn
    x1 = x_ref[...] + jnp.dot(mixed.astype(BF16), wo_ref[...], preferred_element_type=F32)
    x1_ref[...] = x1
    hn = x1 * lax.rsqrt(jnp.mean(x1 * x1, axis=-1, keepdims=True) + EPS) * gffn_ref[...]
    _store_row_tiles(hn_ref, hn)

    logits = _nt_dot(rwt_ref[...], hn.astype(BF16)) + rb_ref[...]
    e_id = lax.broadcasted_iota(jnp.int32, (N_EXPERTS, tm), 0)
    vals = logits
    onehots, top_vals, top_ids = [], [], []
    for _ in range(TOP_K):
        best = jnp.max(vals, axis=0, keepdims=True)
        best_id = jnp.min(jnp.where(vals == best, e_id, N_EXPERTS), axis=0, keepdims=True)
        hit = e_id == best_id
        vals = jnp.where(hit, -jnp.inf, vals)
        onehots.append(hit)
        top_vals.append(best)
        top_ids.append(best_id)
    exps = [jnp.exp(v - top_vals[0]) for v in top_vals]
    denom = exps[0] + exps[1] + exps[2] + exps[3]
    idx_ref[...] = jnp.concatenate(top_ids, axis=0)
    gate_ref[...] = jnp.concatenate([e / denom for e in exps], axis=0)

    chosen = onehots[0] | onehots[1] | onehots[2] | onehots[3]
    t_row = lax.broadcasted_iota(jnp.int32, (tm, tm), 0)
    t_col = lax.broadcasted_iota(jnp.int32, (tm, tm), 1)
    before = (t_row < t_col).astype(BF16)
    chosen_f = chosen.astype(F32)
    prior = jnp.dot(chosen_f.astype(BF16), before, preferred_element_type=F32) + carry_s[:, 0:1]
    ranks = [jnp.sum(jnp.where(hit, prior, 0.0), axis=0, keepdims=True) for hit in onehots]
    rank_ref[...] = jnp.concatenate(ranks, axis=0).astype(jnp.int32)
    carry_s[...] = carry_s[...] + jnp.sum(chosen_f, axis=1, keepdims=True)
    cnt_ref[...] = carry_s[...].astype(jnp.int32)


def _merge(x2, hg, attn, sr, sa, w_proj_rnn, w_proj_attn, w_out, norm_ffn_g, router_w, router_b, seq):
    n_tok = x2.shape[0]
    tm = ROW_TILE
    tiles_per_seq = seq // tm
    row = pl.BlockSpec((tm, D_MODEL), lambda i: (i, 0))
    head = pl.BlockSpec((None, N_HEADS, tm, HEAD_DIM), lambda i: (i // tiles_per_seq, 0, i % tiles_per_seq, 0))
    mat = pl.BlockSpec((D_MODEL, D_MODEL), lambda i: (0, 0))
    topk = pl.BlockSpec((TOP_K, tm), lambda i: (0, i))
    return pl.pallas_call(
        _merge_body,
        grid=(n_tok // tm,),
        in_specs=[row, row, head, row, row, mat, mat, mat,
                  pl.BlockSpec((1, D_MODEL), lambda i: (0, 0)),
                  pl.BlockSpec((N_EXPERTS, D_MODEL), lambda i: (0, 0)),
                  pl.BlockSpec((N_EXPERTS, 1), lambda i: (0, 0))],
        out_specs=[row, pl.BlockSpec((tm * ROW_TILES, LANES), lambda i: (i, 0)), topk, topk, topk,
                   pl.BlockSpec((N_EXPERTS, LANES), lambda i: (0, 0))],
        out_shape=[jax.ShapeDtypeStruct((n_tok, D_MODEL), F32),
                   jax.ShapeDtypeStruct((n_tok * ROW_TILES, LANES), F32),
                   jax.ShapeDtypeStruct((TOP_K, n_tok), jnp.int32), jax.ShapeDtypeStruct((TOP_K, n_tok), F32),
                   jax.ShapeDtypeStruct((TOP_K, n_tok), jnp.int32),
                   jax.ShapeDtypeStruct((N_EXPERTS, LANES), jnp.int32)],
        scratch_shapes=[pltpu.VMEM((N_EXPERTS, LANES), F32)],
        compiler_params=pltpu.CompilerParams(dimension_semantics=("arbitrary",), vmem_limit_bytes=VMEM_LIMIT),
        name="merge_route",
    )(x2, hg, attn, sr, sa, w_proj_rnn.astype(BF16), w_proj_attn.astype(BF16), w_out.astype(BF16),
      norm_ffn_g.reshape(1, D_MODEL), router_w.T.astype(BF16), router_b.reshape(N_EXPERTS, 1))


def _row_copy(src_ref, src_row, dst_ref, dst_row, sem):
    return pltpu.make_async_copy(_row_slab(src_ref, src_row), _row_slab(dst_ref, dst_row), sem)


def _for_each_token_group(tm, body):
    def group(g, c):
        body(pl.multiple_of(g * ISSUE_UNROLL, ISSUE_UNROLL))
        return c

    lax.fori_loop(0, tm // ISSUE_UNROLL, group, 0)


def _start_group_copies(copy_of, t0):
    for i in range(ISSUE_UNROLL):
        for k in range(TOP_K):
            copy_of(t0 + i, k).start(priority=(i * TOP_K + k) % DMA_QUEUES)


def _dispatch_body(pad_off_ref, pad_len_ref, dest_ref, hn_ref, rows_ref, sem, pad_sem):
    tm = hn_ref.shape[0] // ROW_TILES
    first = pl.program_id(0) == 0

    def for_each_pad_row(act):
        def per_expert(e, c):
            off = pad_off_ref[e]

            def per_row(r, c2):
                act(_row_copy(hn_ref, 0, rows_ref, off + r, pad_sem))
                return c2

            return lax.fori_loop(0, pad_len_ref[e], per_row, c)

        lax.fori_loop(0, N_EXPERTS, per_expert, 0)

    @pl.when(first)
    def _():
        for_each_pad_row(lambda cp: cp.start())

    scatter = lambda t, k: _row_copy(hn_ref, t, rows_ref, dest_ref[k, t], sem)
    _for_each_token_group(tm, lambda t0: _start_group_copies(scatter, t0))
    for _ in range(TOP_K):
        pltpu.make_async_copy(hn_ref, rows_ref.at[pl.ds(0, tm * ROW_TILES), :], sem).wait()

    @pl.when(first)
    def _():
        for_each_pad_row(lambda cp: cp.wait())


def _dispatch(dest, pad_off, pad_len, hn, n_rows):
    n_tok = hn.shape[0] // ROW_TILES
    tm = math.gcd(DISPATCH_TILE, n_tok)
    grid_spec = pltpu.PrefetchScalarGridSpec(
        num_scalar_prefetch=2,
        grid=(n_tok // tm,),
        in_specs=[pl.BlockSpec((TOP_K, tm), lambda i, po, pn: (0, i), memory_space=pltpu.SMEM),
                  pl.BlockSpec((tm * ROW_TILES, LANES), lambda i, po, pn: (i, 0))],
        out_specs=pl.BlockSpec(memory_space=pl.ANY),
        scratch_shapes=[pltpu.SemaphoreType.DMA(()), pltpu.SemaphoreType.DMA(())],
    )
    return pl.pallas_call(
        _dispatch_body,
        grid_spec=grid_spec,
        out_shape=jax.ShapeDtypeStruct((n_rows * ROW_TILES, LANES), F32),
        compiler_params=pltpu.CompilerParams(dimension_semantics=("arbitrary",), disable_bounds_checks=True),
        name="dispatch",
    )(pad_off, pad_len, dest, hn)


def _expert_body(blk_expert_ref, n_used_ref, x_ref, wgu_ref, bgu_ref, wd_ref, bd_ref, y_ref, wgu_s, wd_s):
    i = pl.program_id(0)
    live = i < n_used_ref[0]
    new_expert = (i == 0) | (blk_expert_ref[i] != blk_expert_ref[jnp.maximum(i - 1, 0)])

    @pl.when(live & new_expert)
    def _():
        for r in range(0, D_MODEL, WEIGHT_CAST_ROWS):
            wgu_s[r:r + WEIGHT_CAST_ROWS, :] = wgu_ref[r:r + WEIGHT_CAST_ROWS, :].astype(BF16)
        for r in range(0, D_EXPERT, WEIGHT_CAST_ROWS):
            wd_s[r:r + WEIGHT_CAST_ROWS, :] = wd_ref[r:r + WEIGHT_CAST_ROWS, :].astype(BF16)

    @pl.when(live)
    def _():
        gu = jnp.dot(_load_row_tiles(x_ref).astype(BF16), wgu_s[...], preferred_element_type=F32) + bgu_ref[...]
        x_glu = jnp.minimum(gu[:, :D_EXPERT], SWIGLU_LIMIT)
        x_lin = jnp.clip(gu[:, D_EXPERT:], -SWIGLU_LIMIT, SWIGLU_LIMIT)
        act = x_glu * jax.nn.sigmoid(SWIGLU_ALPHA * x_glu) * (x_lin + 1.0)
        _store_row_tiles(y_ref, jnp.dot(act.astype(BF16), wd_s[...], preferred_element_type=F32) + bd_ref[...])

    @pl.when(jnp.logical_not(live))
    def _():
        y_ref[...] = jnp.zeros_like(y_ref)


def _experts(blk_expert, n_used, x_rows, w_gu, b_gu, w_down, b_down):
    n_rows = x_rows.shape[0] // ROW_TILES
    rb = EXPERT_BLOCK_ROWS
    grid_spec = pltpu.PrefetchScalarGridSpec(
        num_scalar_prefetch=2,
        grid=(n_rows // rb,),
        in_specs=[
            pl.BlockSpec((rb * ROW_TILES, LANES), lambda i, be, nu: (jnp.minimum(i, nu[0] - 1), 0)),
            pl.BlockSpec((None, D_MODEL, 2 * D_EXPERT), lambda i, be, nu: (be[i], 0, 0)),
            pl.BlockSpec((None, 1, 2 * D_EXPERT), lambda i, be, nu: (be[i], 0, 0)),
            pl.BlockSpec((None, D_EXPERT, D_MODEL), lambda i, be, nu: (be[i], 0, 0)),
            pl.BlockSpec((None, 1, D_MODEL), lambda i, be, nu: (be[i], 0, 0)),
        ],
        out_specs=pl.BlockSpec((rb * ROW_TILES, LANES), lambda i, be, nu: (i, 0)),
        scratch_shapes=[pltpu.VMEM((D_MODEL, 2 * D_EXPERT), BF16), pltpu.VMEM((D_EXPERT, D_MODEL), BF16)],
    )
    return pl.pallas_call(
        _expert_body,
        grid_spec=grid_spec,
        out_shape=jax.ShapeDtypeStruct((n_rows * ROW_TILES, LANES), F32),
        compiler_params=pltpu.CompilerParams(dimension_semantics=("arbitrary",), vmem_limit_bytes=VMEM_LIMIT),
        name="experts",
    )(blk_expert, n_used, x_rows, w_gu, b_gu.reshape(N_EXPERTS, 1, -1), w_down, b_down.reshape(N_EXPERTS, 1, -1))


def _combine_body(dest_ref, dest_next_ref, gate_ref, x1_ref, y_ref, gfin_ref, out_ref, ybuf, sems):
    tm = x1_ref.shape[0]
    step = pl.program_id(0)
    slot = step % 2

    def gather(d_ref, into):
        fetch = lambda t, k: _row_copy(y_ref, d_ref[k, t], ybuf.at[into, k], t, sems.at[into])
        _for_each_token_group(tm, lambda t0: _start_group_copies(fetch, t0))

    @pl.when(step == 0)
    def _():
        gather(dest_ref, 0)

    @pl.when(step + 1 < pl.num_programs(0))
    def _():
        gather(dest_next_ref, 1 - slot)

    for k in range(TOP_K):
        pltpu.make_async_copy(y_ref.at[pl.ds(0, tm * ROW_TILES), :], ybuf.at[slot, k], sems.at[slot]).wait()
    gates = gate_ref[...]
    x2 = x1_ref[...]
    for k in range(TOP_K):
        x2 = x2 + gates[:, k:k + 1] * _load_row_tiles(ybuf.at[slot, k])
    out_ref[...] = x2 * lax.rsqrt(jnp.mean(x2 * x2, axis=-1, keepdims=True) + EPS) * gfin_ref[...]


def _combine(dest, gates_tok, x1, y_rows, norm_final_g):
    n_tok = x1.shape[0]
    tm = GATHER_TILE
    row = pl.BlockSpec((tm, D_MODEL), lambda i: (i, 0))
    n_tiles = n_tok // tm
    return pl.pallas_call(
        _combine_body,
        grid=(n_tiles,),
        in_specs=[pl.BlockSpec((TOP_K, tm), lambda i: (0, i), memory_space=pltpu.SMEM),
                  pl.BlockSpec((TOP_K, tm), lambda i: (0, jnp.minimum(i + 1, n_tiles - 1)), memory_space=pltpu.SMEM),
                  pl.BlockSpec((tm, TOP_K), lambda i: (i, 0)),
                  row,
                  pl.BlockSpec(memory_space=pl.ANY),
                  pl.BlockSpec((1, D_MODEL), lambda i: (0, 0))],
        out_specs=row,
        out_shape=jax.ShapeDtypeStruct((n_tok, D_MODEL), F32),
        scratch_shapes=[pltpu.VMEM((2, TOP_K, tm * ROW_TILES, LANES), F32), pltpu.SemaphoreType.DMA((2,))],
        compiler_params=pltpu.CompilerParams(dimension_semantics=("arbitrary",), vmem_limit_bytes=VMEM_LIMIT,
                                             disable_bounds_checks=True),
        name="combine",
    )(dest, dest, gates_tok, x1, y_rows, norm_final_g.reshape(1, D_MODEL))


def _routing_plan(top_idx, rank, counts, n_tok):
    padded = ((counts + EXPERT_BLOCK_ROWS - 1) // EXPERT_BLOCK_ROWS) * EXPERT_BLOCK_ROWS
    pend = jnp.cumsum(padded)
    pstart = pend - padded
    e_id = jnp.arange(N_EXPERTS, dtype=jnp.int32)[:, None, None]
    dest = jnp.sum(jnp.where(top_idx[None] == e_id, pstart[:, None, None], 0), axis=0) + rank
    n_rows = n_tok * TOP_K + N_EXPERTS * EXPERT_BLOCK_ROWS
    blk_start = jnp.arange(n_rows // EXPERT_BLOCK_ROWS, dtype=jnp.int32) * EXPERT_BLOCK_ROWS
    blk_expert = jnp.minimum(jnp.sum(blk_start[:, None] >= pend[None, :], axis=1), N_EXPERTS - 1)
    n_used = (pend[-1] // EXPERT_BLOCK_ROWS).reshape(1)
    pad_off, pad_len = pstart + counts, padded - counts
    return (dest.astype(jnp.int32), blk_expert.astype(jnp.int32), n_used.astype(jnp.int32),
            pad_off.astype(jnp.int32), pad_len.astype(jnp.int32), n_rows)


def kernel(x, norm_mix_g, w_in, conv_w, conv_b, lru_wx, lru_bx, lru_wa, lru_ba, lru_a_param, w_proj_rnn, w_proj_attn, w_out, norm_ffn_g, router_w, router_b, expert_w_gu, expert_b_gu, expert_w_down, expert_b_down, norm_final_g):
    bsz, seq, _ = x.shape
    n_tok = bsz * seq
    assert w_in.shape[0] == 1, "single-layer problem: the final RMSNorm is fused into the combine stage"
    assert seq % ROW_TILE == 0 and seq % MOBA_BLOCK == 0
    layer = 0
    x2 = x.reshape(n_tok, D_MODEL)
    rx, rg, q, k, v, sr, sa = _in_proj(x2, norm_mix_g[layer], w_in[layer], seq)
    hg = _rnn(rx, rg, conv_w[layer], conv_b[layer], lru_wx[layer], lru_bx[layer], lru_wa[layer],
              lru_ba[layer], lru_a_param[layer], bsz, seq)
    attn = _moba(q, k, v)
    x1, hn, top_idx, gates, rank, cnt = _merge(x2, hg, attn, sr, sa, w_proj_rnn[layer], w_proj_attn[layer],
                                               w_out[layer], norm_ffn_g[layer], router_w[layer],
                                               router_b[layer], seq)
    dest, blk_expert, n_used, pad_off, pad_len, n_rows = _routing_plan(top_idx, rank, cnt[:, 0], n_tok)
    x_rows = _dispatch(dest, pad_off, pad_len, hn, n_rows)
    y_rows = _experts(blk_expert, n_used, x_rows, expert_w_gu[layer], expert_b_gu[layer],
                      expert_w_down[layer], expert_b_down[layer])
    out = _combine(dest, gates.T, x1, y_rows, norm_final_g)
    return out.reshape(bsz, seq, D_MODEL)
```

```python
import functools
import math

import jax
import jax.numpy as jnp
from jax import lax
from jax.experimental import pallas as pl
from jax.experimental.pallas import tpu as pltpu

F32 = jnp.float32
BF16 = jnp.bfloat16

D_MODEL = 1024
RNN_WIDTH = 1024
RNN_BLOCKS = 16
RNN_BLOCK_DIM = RNN_WIDTH // RNN_BLOCKS
CONV_WIDTH = 4
LRU_C = 8.0
N_HEADS = 8
HEAD_DIM = 128
ROPE_DIM = HEAD_DIM // 4
ROPE_THETA = 500000.0
MOBA_BLOCK = 256
MOBA_TOPK = 3
N_EXPERTS = 32
TOP_K = 4
D_EXPERT = 1024
SWIGLU_LIMIT = 7.0
SWIGLU_ALPHA = 1.702
EPS = 1e-6
NEG = -1e30

V7X_VMEM_BYTES = 64 * 1024 * 1024
V7X_MXU_DIM = 256
SUBLANES = 8
LANES = 128
BF16_SUBLANES = 16

ROW_TILE = 512
EXPERT_BLOCK_ROWS = 512
WEIGHT_CAST_ROWS = 128
DISPATCH_TILE = 1024
GATHER_TILE = 256
ISSUE_UNROLL = 8
DMA_QUEUES = 2
LRU_GROUP = V7X_MXU_DIM // RNN_BLOCK_DIM
VMEM_LIMIT = V7X_VMEM_BYTES - 8 * 1024 * 1024


def _nt_dot(a, b, **kw):
    return lax.dot_general(a, b, (((1,), (1,)), ((), ())), preferred_element_type=F32, **kw)


ROW_TILES = D_MODEL // LANES
assert ROW_TILES == SUBLANES


def _store_row_tiles(ref, val):
    rows = val.shape[0]
    for s in range(ROW_TILES):
        ref[pl.ds(s, rows, stride=ROW_TILES), :] = val[:, s * LANES:(s + 1) * LANES]


def _load_row_tiles(ref, first_row=0, rows=None):
    rows = ref.shape[0] // ROW_TILES if rows is None else rows
    base = first_row * ROW_TILES
    return jnp.concatenate([ref[pl.ds(base + s, rows, stride=ROW_TILES), :] for s in range(ROW_TILES)], axis=1)


def _row_slab(ref, row):
    return ref.at[pl.ds(pl.multiple_of(row * ROW_TILES, ROW_TILES), ROW_TILES), :]


def _gelu_tanh(x):
    return 0.5 * x * (1.0 + jnp.tanh(math.sqrt(2.0 / math.pi) * (x + 0.044715 * (x * x * x))))


def _inproj_body(x_ref, g_ref, w_ref, rot_ref, rx_ref, rg_ref, q_ref, k_ref, v_ref, sr_ref, sa_ref):
    x = x_ref[...]
    ms = jnp.mean(x * x, axis=-1, keepdims=True)
    xn = (x * lax.rsqrt(ms + EPS) * g_ref[...]).astype(BF16)

    def proj(j):
        return jnp.dot(xn, w_ref[:, j * D_MODEL:(j + 1) * D_MODEL], preferred_element_type=F32)

    rx_ref[...] = proj(0).astype(BF16)
    rg_ref[...] = _gelu_tanh(proj(1)).astype(BF16)

    cos, sin_hi, sin_lo = rot_ref[0], rot_ref[1], rot_ref[2]
    half = ROPE_DIM // 2

    def rotary_heads(z, out_ref, scale):
        for h in range(N_HEADS):
            zh = z[:, h * HEAD_DIM:(h + 1) * HEAD_DIM]
            r = zh * cos + pltpu.roll(zh, HEAD_DIM - half, axis=1) * sin_hi + pltpu.roll(zh, half, axis=1) * sin_lo
            out_ref[h] = (r * scale).astype(BF16)

    rotary_heads(proj(2), q_ref, 1.0 / math.sqrt(HEAD_DIM))
    rotary_heads(proj(3), k_ref, 1.0)
    zv = proj(4)
    for h in range(N_HEADS):
        v_ref[h] = zv[:, h * HEAD_DIM:(h + 1) * HEAD_DIM].astype(BF16)
    sr_ref[...] = jax.nn.sigmoid(proj(5)).astype(BF16)
    sa_ref[...] = jax.nn.sigmoid(proj(6)).astype(BF16)


def _rotary_tables(seq):
    half = ROPE_DIM // 2
    inv_freq = ROPE_THETA ** (-jnp.arange(half, dtype=F32) * (2.0 / ROPE_DIM))
    ang = jnp.arange(seq, dtype=F32)[:, None] * inv_freq[None, :]
    cos, sin = jnp.cos(ang), jnp.sin(ang)
    pad = HEAD_DIM - ROPE_DIM
    cos_t = jnp.concatenate([cos, cos, jnp.ones((seq, pad), F32)], axis=1)
    sin_hi = jnp.concatenate([-sin, jnp.zeros((seq, half + pad), F32)], axis=1)
    sin_lo = jnp.concatenate([jnp.zeros((seq, half), F32), sin, jnp.zeros((seq, pad), F32)], axis=1)
    return jnp.stack([cos_t, sin_hi, sin_lo])


def _in_proj(x2, norm_g, w_in, seq):
    n_tok = x2.shape[0]
    tm = ROW_TILE
    tiles_per_seq = seq // tm
    in_cols = w_in.shape[1]
    row = pl.BlockSpec((tm, D_MODEL), lambda i: (i, 0))
    head = pl.BlockSpec((None, N_HEADS, tm, HEAD_DIM), lambda i: (i // tiles_per_seq, 0, i % tiles_per_seq, 0))
    bsz = n_tok // seq
    tok_bf16 = jax.ShapeDtypeStruct((n_tok, D_MODEL), BF16)
    head_bf16 = jax.ShapeDtypeStruct((bsz, N_HEADS, seq, HEAD_DIM), BF16)
    return pl.pallas_call(
        _inproj_body,
        grid=(n_tok // tm,),
        in_specs=[
            row,
            pl.BlockSpec((1, D_MODEL), lambda i: (0, 0)),
            pl.BlockSpec((D_MODEL, in_cols), lambda i: (0, 0), pipeline_mode=pl.Buffered(1)),
            pl.BlockSpec((3, tm, HEAD_DIM), lambda i: (0, i % tiles_per_seq, 0)),
        ],
        out_specs=[row, row, head, head, head, row, row],
        out_shape=[tok_bf16, tok_bf16, head_bf16, head_bf16, head_bf16, tok_bf16, tok_bf16],
        compiler_params=pltpu.CompilerParams(dimension_semantics=("parallel",), vmem_limit_bytes=VMEM_LIMIT),
        name="in_proj",
    )(x2, norm_g.reshape(1, D_MODEL), w_in.astype(BF16), _rotary_tables(seq))


def _rnn_body(rx_ref, halo_ref, rg_ref, cw_ref, cb_ref, wbd_ref, bx_ref, ba_ref, ap_ref, hg_ref,
              xe_s, a_s, u_s, acum_s, ucum_s, h_s, hprev_s):
    s = pl.program_id(1)
    ts = rx_ref.shape[0]
    n_halo = halo_ref.shape[0]
    xe_s[0:n_halo, :] = jnp.where(s == 0, 0.0, halo_ref[...].astype(F32))
    xe_s[n_halo:, :] = rx_ref[...].astype(F32)
    xc = cb_ref[...]
    for tap in range(CONV_WIDTH):
        xc = xc + cw_ref[tap:tap + 1, :] * xe_s[pl.ds(n_halo - (CONV_WIDTH - 1) + tap, ts), :]

    xcb = xc.astype(BF16)
    gx, ga = [], []
    for g in range(RNN_WIDTH // V7X_MXU_DIM):
        r = jnp.dot(xcb[:, g * V7X_MXU_DIM:(g + 1) * V7X_MXU_DIM], wbd_ref[g], preferred_element_type=F32)
        gx.append(r[:, :V7X_MXU_DIM])
        ga.append(r[:, V7X_MXU_DIM:])
    gate_x = jax.nn.sigmoid(jnp.concatenate(gx, axis=1) + bx_ref[...])
    gate_a = jax.nn.sigmoid(jnp.concatenate(ga, axis=1) + ba_ref[...])
    z = -ap_ref[...]
    softplus = jnp.maximum(z, 0.0) + jnp.log1p(jnp.exp(-jnp.abs(z)))
    log_a = -LRU_C * gate_a * softplus
    a = jnp.exp(log_a)
    u = jnp.sqrt(-jnp.tanh(log_a) * (a * a + 1.0)) * (gate_x * xc)

    n_slab = a_s.shape[0]
    n_grp = ts // SUBLANES
    for c in range(n_slab):
        a_s[c] = a[:, c * LANES:(c + 1) * LANES]
        u_s[c] = u[:, c * LANES:(c + 1) * LANES]
    for c in range(n_slab):
        a_cum = u_cum = None
        for j in range(SUBLANES):
            rows_j = pl.ds(j, n_grp, stride=SUBLANES)
            a_j, u_j = a_s[c, rows_j, :], u_s[c, rows_j, :]
            a_cum, u_cum = (a_j, u_j) if j == 0 else (a_j * a_cum, a_j * u_cum + u_j)
            acum_s[c, rows_j, :] = a_cum
            ucum_s[c, rows_j, :] = u_cum

    def chain(g, h_last):
        rows = pl.ds(pl.multiple_of(g * SUBLANES, SUBLANES), SUBLANES)
        last = []
        for c in range(n_slab):
            h = ucum_s[c, rows, :] + acum_s[c, rows, :] * h_last[c]
            h_s[c, rows, :] = h
            last.append(h[SUBLANES - 1:SUBLANES, :])
        return tuple(last)

    h_first = jnp.where(s == 0, 0.0, hprev_s[...])
    h_last = lax.fori_loop(0, n_grp, chain,
                           tuple(h_first[:, c * LANES:(c + 1) * LANES] for c in range(n_slab)))
    hprev_s[...] = jnp.concatenate(h_last, axis=1)
    h_all = jnp.concatenate([h_s[c] for c in range(n_slab)], axis=1)
    hg_ref[...] = (h_all * rg_ref[...].astype(F32)).astype(BF16)


def _block_diag_gates(wx, wa):
    n_grp = RNN_BLOCKS // LRU_GROUP

    def bd(w):
        w = w.reshape(n_grp, LRU_GROUP, RNN_BLOCK_DIM, RNN_BLOCK_DIM)
        eye = jnp.eye(LRU_GROUP, dtype=w.dtype)
        full = jnp.einsum('gaij,ab->gaibj', w, eye)
        return full.reshape(n_grp, V7X_MXU_DIM, V7X_MXU_DIM)

    return jnp.concatenate([bd(wx), bd(wa)], axis=2).astype(BF16)


def _rnn(rx, rg, conv_w, conv_b, wx, bx, wa, ba, a_param, bsz, seq):
    ts = ROW_TILE
    tiles_per_seq = seq // ts
    halo_rows = BF16_SUBLANES
    halo_per_tile = ts // halo_rows
    row = pl.BlockSpec((ts, RNN_WIDTH), lambda b, s: (b * tiles_per_seq + s, 0))
    halo = pl.BlockSpec((halo_rows, RNN_WIDTH),
                        lambda b, s: (jnp.maximum((b * tiles_per_seq + s) * halo_per_tile - 1, 0), 0))
    vec = pl.BlockSpec((1, RNN_WIDTH), lambda b, s: (0, 0))
    n_grp = RNN_WIDTH // V7X_MXU_DIM
    return pl.pallas_call(
        _rnn_body,
        grid=(bsz, tiles_per_seq),
        in_specs=[row, halo, row,
                  pl.BlockSpec((CONV_WIDTH, RNN_WIDTH), lambda b, s: (0, 0)), vec,
                  pl.BlockSpec((n_grp, V7X_MXU_DIM, 2 * V7X_MXU_DIM), lambda b, s: (0, 0, 0)),
                  vec, vec, vec],
        out_specs=row,
        out_shape=jax.ShapeDtypeStruct((bsz * seq, RNN_WIDTH), BF16),
        scratch_shapes=[pltpu.VMEM((halo_rows + ts, RNN_WIDTH), F32)]
        + [pltpu.VMEM((RNN_WIDTH // LANES, ts, LANES), F32)] * 5
        + [pltpu.VMEM((1, RNN_WIDTH), F32)],
        compiler_params=pltpu.CompilerParams(dimension_semantics=("parallel", "arbitrary"),
                                             vmem_limit_bytes=VMEM_LIMIT),
        name="rnn",
    )(rx, rx, rg, conv_w, conv_b.reshape(1, -1), _block_diag_gates(wx, wa),
      bx.reshape(1, -1), ba.reshape(1, -1), a_param.reshape(1, -1))


def _moba_body(q_ref, k_ref, v_ref, o_ref, kaug_s):
    seq = q_ref.shape[0]
    nb = seq // MOBA_BLOCK
    nb_pad = -(-nb // SUBLANES) * SUBLANES
    kmean = jnp.mean(k_ref[...].astype(F32).reshape(nb, MOBA_BLOCK, HEAD_DIM), axis=1)
    if nb_pad > nb:
        kmean = jnp.concatenate([kmean, jnp.zeros((nb_pad - nb, HEAD_DIM), F32)], axis=0)

    key_blk = lax.broadcasted_iota(jnp.int32, (seq, LANES), 0) // MOBA_BLOCK
    lane = lax.broadcasted_iota(jnp.int32, (seq, LANES), 1)
    kaug_s[:, :HEAD_DIM] = k_ref[...]
    kaug_s[:, HEAD_DIM:] = (key_blk == lane).astype(BF16)

    r_id = lax.broadcasted_iota(jnp.int32, (MOBA_BLOCK, MOBA_BLOCK), 0)
    c_id = lax.broadcasted_iota(jnp.int32, (MOBA_BLOCK, MOBA_BLOCK), 1)
    causal = c_id <= r_id
    eye = (c_id == r_id).astype(BF16)
    j_id = lax.broadcasted_iota(jnp.int32, (nb_pad, MOBA_BLOCK), 0)

    def masked_scores(n):
        qn = q_ref[n * MOBA_BLOCK:(n + 1) * MOBA_BLOCK, :]
        if n > MOBA_TOPK:
            gate = _nt_dot(kmean, qn.astype(F32), precision=lax.Precision.HIGHEST)
            rank = jnp.zeros((nb_pad, MOBA_BLOCK), F32)
            for jp in range(n):
                row = gate[jp:jp + 1, :]
                rank = rank + ((row > gate) | ((row == gate) & (jp < j_id))).astype(F32)
            keep = (rank < float(MOBA_TOPK)) | (j_id >= n)
            bias_t = jnp.where(keep, 0.0, NEG)
            bias_t = jnp.concatenate([bias_t, jnp.zeros((LANES - nb_pad, MOBA_BLOCK), F32)], axis=0).astype(BF16)
            bias = _nt_dot(eye, bias_t).astype(BF16)
        else:
            bias = jnp.zeros((MOBA_BLOCK, LANES), BF16)
        q_aug = jnp.concatenate([qn, bias], axis=1)
        s = _nt_dot(q_aug, kaug_s[0:(n + 1) * MOBA_BLOCK, :])
        parts = [s[:, j * MOBA_BLOCK:(j + 1) * MOBA_BLOCK] for j in range(n + 1)]
        parts[n] = jnp.where(causal, parts[n], NEG)
        return parts

    def attend(n, parts):
        m = functools.reduce(jnp.maximum, parts)
        m = jnp.max(m, axis=1, keepdims=True)
        probs = [jnp.exp(part - m) for part in parts]
        l = jnp.sum(functools.reduce(jnp.add, probs), axis=1, keepdims=True)
        p_all = jnp.concatenate([p.astype(BF16) for p in probs], axis=1)
        acc = jnp.dot(p_all, v_ref[0:(n + 1) * MOBA_BLOCK, :], preferred_element_type=F32)
        o_ref[n * MOBA_BLOCK:(n + 1) * MOBA_BLOCK, :] = (acc * (1.0 / l)).astype(BF16)

    ahead = masked_scores(0)
    for n in range(nb):
        parts = ahead
        if n + 1 < nb:
            ahead = masked_scores(n + 1)
        attend(n, parts)


def _moba(q, k, v):
    bsz, n_heads, seq, hd = q.shape
    spec = pl.BlockSpec((None, None, seq, hd), lambda b, h: (b, h, 0, 0))
    return pl.pallas_call(
        _moba_body,
        grid=(bsz, n_heads),
        in_specs=[spec, spec, spec],
        out_specs=spec,
        out_shape=jax.ShapeDtypeStruct(q.shape, BF16),
        scratch_shapes=[pltpu.VMEM((seq, 2 * hd), BF16)],
        compiler_params=pltpu.CompilerParams(dimension_semantics=("parallel", "parallel"),
                                             vmem_limit_bytes=VMEM_LIMIT),
        name="moba",
    )(q, k, v)


def _merge_body(x_ref, hg_ref, at_ref, sr_ref, sa_ref, wpr_ref, wpa_ref, wo_ref, gffn_ref, rwt_ref, rb_ref,
                x1_ref, hn_ref, idx_ref, gate_ref, rank_ref, cnt_ref, carry_s):
    tm = x_ref.shape[0]

    @pl.when(pl.program_id(0) == 0)
    def _():
        carry_s[...] = jnp.zeros_like(carry_s)

    y_rnn = jnp.dot(hg_ref[...], wpr_ref[...], preferred_element_type=F32)
    attn = jnp.concatenate([at_ref[h] for h in range(N_HEADS)], axis=1)
    y_attn = jnp.dot(attn, wpa_ref[...], preferred_element_type=F32)
    mixed = sr_ref[...].astype(F32) * y_rnn + sa_ref[...].astype(F32) * y_attn
    x1 = x_ref[...] + jnp.dot(mixed.astype(BF16), wo_ref[...], preferred_element_type=F32)
    x1_ref[...] = x1
    hn = x1 * lax.rsqrt(jnp.mean(x1 * x1, axis=-1, keepdims=True) + EPS) * gffn_ref[...]
    _store_row_tiles(hn_ref, hn)

    logits = _nt_dot(rwt_ref[...], hn.astype(BF16)) + rb_ref[...]
    e_id = lax.broadcasted_iota(jnp.int32, (N_EXPERTS, tm), 0)
    vals = logits
    onehots, top_vals, top_ids = [], [], []
    for _ in range(TOP_K):
        best = jnp.max(vals, axis=0, keepdims=True)
        best_id = jnp.min(jnp.where(vals == best, e_id, N_EXPERTS), axis=0, keepdims=True)
        hit = e_id == best_id
        vals = jnp.where(hit, -jnp.inf, vals)
        onehots.append(hit)
        top_vals.append(best)
        top_ids.append(best_id)
    exps = [jnp.exp(v - top_vals[0]) for v in top_vals]
    denom = exps[0] + exps[1] + exps[2] + exps[3]
    idx_ref[...] = jnp.concatenate(top_ids, axis=0)
    gate_ref[...] = jnp.concatenate([e / denom for e in exps], axis=0)

    chosen = onehots[0] | onehots[1] | onehots[2] | onehots[3]
    t_row = lax.broadcasted_iota(jnp.int32, (tm, tm), 0)
    t_col = lax.broadcasted_iota(jnp.int32, (tm, tm), 1)
    before = (t_row < t_col).astype(BF16)
    chosen_f = chosen.astype(F32)
    prior = jnp.dot(chosen_f.astype(BF16), before, preferred_element_type=F32) + carry_s[:, 0:1]
    ranks = [jnp.sum(jnp.where(hit, prior, 0.0), axis=0, keepdims=True) for hit in onehots]
    rank_ref[...] = jnp.concatenate(ranks, axis=0).astype(jnp.int32)
    carry_s[...] = carry_s[...] + jnp.sum(chosen_f, axis=1, keepdims=True)
    cnt_ref[...] = carry_s[...].astype(jnp.int32)


def _merge(x2, hg, attn, sr, sa, w_proj_rnn, w_proj_attn, w_out, norm_ffn_g, router_w, router_b, seq):
    n_tok = x2.shape[0]
    tm = ROW_TILE
    tiles_per_seq = seq // tm
    row = pl.BlockSpec((tm, D_MODEL), lambda i: (i, 0))
    head = pl.BlockSpec((None, N_HEADS, tm, HEAD_DIM), lambda i: (i // tiles_per_seq, 0, i % tiles_per_seq, 0))
    mat = pl.BlockSpec((D_MODEL, D_MODEL), lambda i: (0, 0))
    topk = pl.BlockSpec((TOP_K, tm), lambda i: (0, i))
    return pl.pallas_call(
        _merge_body,
        grid=(n_tok // tm,),
        in_specs=[row, row, head, row, row, mat, mat, mat,
                  pl.BlockSpec((1, D_MODEL), lambda i: (0, 0)),
                  pl.BlockSpec((N_EXPERTS, D_MODEL), lambda i: (0, 0)),
                  pl.BlockSpec((N_EXPERTS, 1), lambda i: (0, 0))],
        out_specs=[row, pl.BlockSpec((tm * ROW_TILES, LANES), lambda i: (i, 0)), topk, topk, topk,
                   pl.BlockSpec((N_EXPERTS, LANES), lambda i: (0, 0))],
        out_shape=[jax.ShapeDtypeStruct((n_tok, D_MODEL), F32),
                   jax.ShapeDtypeStruct((n_tok * ROW_TILES, LANES), F32),
                   jax.ShapeDtypeStruct((TOP_K, n_tok), jnp.int32), jax.ShapeDtypeStruct((TOP_K, n_tok), F32),
                   jax.ShapeDtypeStruct((TOP_K, n_tok), jnp.int32),
                   jax.ShapeDtypeStruct((N_EXPERTS, LANES), jnp.int32)],
        scratch_shapes=[pltpu.VMEM((N_EXPERTS, LANES), F32)],
        compiler_params=pltpu.CompilerParams(dimension_semantics=("arbitrary",), vmem_limit_bytes=VMEM_LIMIT),
        name="merge_route",
    )(x2, hg, attn, sr, sa, w_proj_rnn.astype(BF16), w_proj_attn.astype(BF16), w_out.astype(BF16),
      norm_ffn_g.reshape(1, D_MODEL), router_w.T.astype(BF16), router_b.reshape(N_EXPERTS, 1))


def _row_copy(src_ref, src_row, dst_ref, dst_row, sem):
    return pltpu.make_async_copy(_row_slab(src_ref, src_row), _row_slab(dst_ref, dst_row), sem)


def _for_each_token_group(tm, body):
    def group(g, c):
        body(pl.multiple_of(g * ISSUE_UNROLL, ISSUE_UNROLL))
        return c

    lax.fori_loop(0, tm // ISSUE_UNROLL, group, 0)


def _start_group_copies(copy_of, t0):
    for i in range(ISSUE_UNROLL):
        for k in range(TOP_K):
            copy_of(t0 + i, k).start(priority=(i * TOP_K + k) % DMA_QUEUES)


def _dispatch_body(pad_off_ref, pad_len_ref, dest_ref, hn_ref, rows_ref, sem, pad_sem):
    tm = hn_ref.shape[0] // ROW_TILES
    first = pl.program_id(0) == 0

    def for_each_pad_row(act):
        def per_expert(e, c):
            off = pad_off_ref[e]

            def per_row(r, c2):
                act(_row_copy(hn_ref, 0, rows_ref, off + r, pad_sem))
                return c2

            return lax.fori_loop(0, pad_len_ref[e], per_row, c)

        lax.fori_loop(0, N_EXPERTS, per_expert, 0)

    @pl.when(first)
    def _():
        for_each_pad_row(lambda cp: cp.start())

    scatter = lambda t, k: _row_copy(hn_ref, t, rows_ref, dest_ref[k, t], sem)
    _for_each_token_group(tm, lambda t0: _start_group_copies(scatter, t0))
    for _ in range(TOP_K):
        pltpu.make_async_copy(hn_ref, rows_ref.at[pl.ds(0, tm * ROW_TILES), :], sem).wait()

    @pl.when(first)
    def _():
        for_each_pad_row(lambda cp: cp.wait())


def _dispatch(dest, pad_off, pad_len, hn, n_rows):
    n_tok = hn.shape[0] // ROW_TILES
    tm = math.gcd(DISPATCH_TILE, n_tok)
    grid_spec = pltpu.PrefetchScalarGridSpec(
        num_scalar_prefetch=2,
        grid=(n_tok // tm,),
        in_specs=[pl.BlockSpec((TOP_K, tm), lambda i, po, pn: (0, i), memory_space=pltpu.SMEM),
                  pl.BlockSpec((tm * ROW_TILES, LANES), lambda i, po, pn: (i, 0))],
        out_specs=pl.BlockSpec(memory_space=pl.ANY),
        scratch_shapes=[pltpu.SemaphoreType.DMA(()), pltpu.SemaphoreType.DMA(())],
    )
    return pl.pallas_call(
        _dispatch_body,
        grid_spec=grid_spec,
        out_shape=jax.ShapeDtypeStruct((n_rows * ROW_TILES, LANES), F32),
        compiler_params=pltpu.CompilerParams(dimension_semantics=("arbitrary",), disable_bounds_checks=True),
        name="dispatch",
    )(pad_off, pad_len, dest, hn)


def _expert_body(blk_expert_ref, n_used_ref, next_expert_ref, slot_ref, x_ref, wgu_hbm, bgu_ref, wd_hbm, bd_ref, y_ref,
                 wgu_buf, wd_buf, wgu_s, wd_s, sems):
    i = pl.program_id(0)
    live = i < n_used_ref[0]
    expert = blk_expert_ref[i]
    new_expert = (i == 0) | (expert != blk_expert_ref[jnp.maximum(i - 1, 0)])

    def weight_copies(e, slot):
        return (pltpu.make_async_copy(wgu_hbm.at[e], wgu_buf.at[slot], sems.at[0, slot]),
                pltpu.make_async_copy(wd_hbm.at[e], wd_buf.at[slot], sems.at[1, slot]))

    @pl.when(live & new_expert)
    def _():
        slot = slot_ref[expert]
        upcoming = next_expert_ref[expert]

        @pl.when(i == 0)
        def _():
            for cp in weight_copies(expert, slot):
                cp.start()

        @pl.when(upcoming >= 0)
        def _():
            for cp in weight_copies(upcoming, 1 - slot):
                cp.start()

        for cp in weight_copies(expert, slot):
            cp.wait()
        for r in range(0, D_MODEL, WEIGHT_CAST_ROWS):
            wgu_s[r:r + WEIGHT_CAST_ROWS, :] = wgu_buf[slot, r:r + WEIGHT_CAST_ROWS, :].astype(BF16)
        for r in range(0, D_EXPERT, WEIGHT_CAST_ROWS):
            wd_s[r:r + WEIGHT_CAST_ROWS, :] = wd_buf[slot, r:r + WEIGHT_CAST_ROWS, :].astype(BF16)

    @pl.when(live)
    def _():
        gu = jnp.dot(_load_row_tiles(x_ref).astype(BF16), wgu_s[...], preferred_element_type=F32) + bgu_ref[...]
        x_glu = jnp.minimum(gu[:, :D_EXPERT], SWIGLU_LIMIT)
        x_lin = jnp.clip(gu[:, D_EXPERT:], -SWIGLU_LIMIT, SWIGLU_LIMIT)
        act = x_glu * jax.nn.sigmoid(SWIGLU_ALPHA * x_glu) * (x_lin + 1.0)
        _store_row_tiles(y_ref, jnp.dot(act.astype(BF16), wd_s[...], preferred_element_type=F32) + bd_ref[...])

    @pl.when(jnp.logical_not(live))
    def _():
        y_ref[...] = jnp.zeros_like(y_ref)


def _experts(blk_expert, n_used, next_expert, slot_of, x_rows, w_gu, b_gu, w_down, b_down):
    n_rows = x_rows.shape[0] // ROW_TILES
    rb = EXPERT_BLOCK_ROWS
    grid_spec = pltpu.PrefetchScalarGridSpec(
        num_scalar_prefetch=4,
        grid=(n_rows // rb,),
        in_specs=[
            pl.BlockSpec((rb * ROW_TILES, LANES), lambda i, be, nu, nx, sl: (jnp.minimum(i, nu[0] - 1), 0)),
            pl.BlockSpec(memory_space=pl.ANY),
            pl.BlockSpec((None, 1, 2 * D_EXPERT), lambda i, be, nu, nx, sl: (be[i], 0, 0)),
            pl.BlockSpec(memory_space=pl.ANY),
            pl.BlockSpec((None, 1, D_MODEL), lambda i, be, nu, nx, sl: (be[i], 0, 0)),
        ],
        out_specs=pl.BlockSpec((rb * ROW_TILES, LANES), lambda i, be, nu, nx, sl: (i, 0)),
        scratch_shapes=[pltpu.VMEM((2, D_MODEL, 2 * D_EXPERT), F32), pltpu.VMEM((2, D_EXPERT, D_MODEL), F32),
                        pltpu.VMEM((D_MODEL, 2 * D_EXPERT), BF16), pltpu.VMEM((D_EXPERT, D_MODEL), BF16),
                        pltpu.SemaphoreType.DMA((2, 2))],
    )
    return pl.pallas_call(
        _expert_body,
        grid_spec=grid_spec,
        out_shape=jax.ShapeDtypeStruct((n_rows * ROW_TILES, LANES), F32),
        compiler_params=pltpu.CompilerParams(dimension_semantics=("arbitrary",), vmem_limit_bytes=VMEM_LIMIT),
        name="experts",
    )(blk_expert, n_used, next_expert, slot_of, x_rows, w_gu, b_gu.reshape(N_EXPERTS, 1, -1), w_down,
      b_down.reshape(N_EXPERTS, 1, -1))


def _combine_body(dest_ref, dest_next_ref, gate_ref, x1_ref, y_ref, gfin_ref, out_ref, ybuf, sems):
    tm = x1_ref.shape[0]
    step = pl.program_id(0)
    slot = step % 2

    def gather(d_ref, into):
        fetch = lambda t, k: _row_copy(y_ref, d_ref[k, t], ybuf.at[into, k], t, sems.at[into])
        _for_each_token_group(tm, lambda t0: _start_group_copies(fetch, t0))

    @pl.when(step == 0)
    def _():
        gather(dest_ref, 0)

    @pl.when(step + 1 < pl.num_programs(0))
    def _():
        gather(dest_next_ref, 1 - slot)

    for k in range(TOP_K):
        pltpu.make_async_copy(y_ref.at[pl.ds(0, tm * ROW_TILES), :], ybuf.at[slot, k], sems.at[slot]).wait()
    gates = gate_ref[...]
    x2 = x1_ref[...]
    for k in range(TOP_K):
        x2 = x2 + gates[:, k:k + 1] * _load_row_tiles(ybuf.at[slot, k])
    out_ref[...] = x2 * lax.rsqrt(jnp.mean(x2 * x2, axis=-1, keepdims=True) + EPS) * gfin_ref[...]


def _combine(dest, gates_tok, x1, y_rows, norm_final_g):
    n_tok = x1.shape[0]
    tm = GATHER_TILE
    row = pl.BlockSpec((tm, D_MODEL), lambda i: (i, 0))
    n_tiles = n_tok // tm
    return pl.pallas_call(
        _combine_body,
        grid=(n_tiles,),
        in_specs=[pl.BlockSpec((TOP_K, tm), lambda i: (0, i), memory_space=pltpu.SMEM),
                  pl.BlockSpec((TOP_K, tm), lambda i: (0, jnp.minimum(i + 1, n_tiles - 1)), memory_space=pltpu.SMEM),
                  pl.BlockSpec((tm, TOP_K), lambda i: (i, 0)),
                  row,
                  pl.BlockSpec(memory_space=pl.ANY),
                  pl.BlockSpec((1, D_MODEL), lambda i: (0, 0))],
        out_specs=row,
        out_shape=jax.ShapeDtypeStruct((n_tok, D_MODEL), F32),
        scratch_shapes=[pltpu.VMEM((2, TOP_K, tm * ROW_TILES, LANES), F32), pltpu.SemaphoreType.DMA((2,))],
        compiler_params=pltpu.CompilerParams(dimension_semantics=("arbitrary",), vmem_limit_bytes=VMEM_LIMIT,
                                             disable_bounds_checks=True),
        name="combine",
    )(dest, dest, gates_tok, x1, y_rows, norm_final_g.reshape(1, D_MODEL))


def _routing_plan(top_idx, rank, counts, n_tok):
    padded = ((counts + EXPERT_BLOCK_ROWS - 1) // EXPERT_BLOCK_ROWS) * EXPERT_BLOCK_ROWS
    pend = jnp.cumsum(padded)
    pstart = pend - padded
    e_id = jnp.arange(N_EXPERTS, dtype=jnp.int32)[:, None, None]
    dest = jnp.sum(jnp.where(top_idx[None] == e_id, pstart[:, None, None], 0), axis=0) + rank
    n_rows = n_tok * TOP_K + N_EXPERTS * EXPERT_BLOCK_ROWS
    blk_start = jnp.arange(n_rows // EXPERT_BLOCK_ROWS, dtype=jnp.int32) * EXPERT_BLOCK_ROWS
    blk_expert = jnp.minimum(jnp.sum(blk_start[:, None] >= pend[None, :], axis=1), N_EXPERTS - 1)
    n_used = (pend[-1] // EXPERT_BLOCK_ROWS).reshape(1)
    pad_off, pad_len = pstart + counts, padded - counts
    used = counts > 0
    ids = jnp.arange(N_EXPERTS, dtype=jnp.int32)
    later = jnp.where(used[None, :] & (ids[None, :] > ids[:, None]), ids[None, :], N_EXPERTS)
    next_expert = jnp.min(later, axis=1)
    next_expert = jnp.where(next_expert == N_EXPERTS, -1, next_expert)
    slot_of = (jnp.cumsum(used) - used) % 2
    return (dest.astype(jnp.int32), blk_expert.astype(jnp.int32), n_used.astype(jnp.int32),
            next_expert.astype(jnp.int32), slot_of.astype(jnp.int32),
            pad_off.astype(jnp.int32), pad_len.astype(jnp.int32), n_rows)


def kernel(x, norm_mix_g, w_in, conv_w, conv_b, lru_wx, lru_bx, lru_wa, lru_ba, lru_a_param, w_proj_rnn, w_proj_attn, w_out, norm_ffn_g, router_w, router_b, expert_w_gu, expert_b_gu, expert_w_down, expert_b_down, norm_final_g):
    bsz, seq, _ = x.shape
    n_tok = bsz * seq
    assert w_in.shape[0] == 1, "single-layer problem: the final RMSNorm is fused into the combine stage"
    assert seq % ROW_TILE == 0 and seq % MOBA_BLOCK == 0
    layer = 0
    x2 = x.reshape(n_tok, D_MODEL)
    rx, rg, q, k, v, sr, sa = _in_proj(x2, norm_mix_g[layer], w_in[layer], seq)
    hg = _rnn(rx, rg, conv_w[layer], conv_b[layer], lru_wx[layer], lru_bx[layer], lru_wa[layer],
              lru_ba[layer], lru_a_param[layer], bsz, seq)
    attn = _moba(q, k, v)
    x1, hn, top_idx, gates, rank, cnt = _merge(x2, hg, attn, sr, sa, w_proj_rnn[layer], w_proj_attn[layer],
                                               w_out[layer], norm_ffn_g[layer], router_w[layer],
                                               router_b[layer], seq)
    dest, blk_expert, n_used, next_expert, slot_of, pad_off, pad_len, n_rows = _routing_plan(
        top_idx, rank, cnt[:, 0], n_tok)
    x_rows = _dispatch(dest, pad_off, pad_len, hn, n_rows)
    y_rows = _experts(blk_expert, n_used, next_expert, slot_of, x_rows, expert_w_gu[layer], expert_b_gu[layer],
                      expert_w_down[layer], expert_b_down[layer])
    out = _combine(dest, gates.T, x1, y_rows, norm_final_g)
    return out.reshape(bsz, seq, D_MODEL)
```

```python
import functools
import math

import jax
import jax.numpy as jnp
from jax import lax
from jax.experimental import pallas as pl
from jax.experimental.pallas import tpu as pltpu

F32 = jnp.float32
BF16 = jnp.bfloat16

D_MODEL = 1024
RNN_WIDTH = 1024
RNN_BLOCKS = 16
RNN_BLOCK_DIM = RNN_WIDTH // RNN_BLOCKS
CONV_WIDTH = 4
LRU_C = 8.0
N_HEADS = 8
HEAD_DIM = 128
ROPE_DIM = HEAD_DIM // 4
ROPE_THETA = 500000.0
MOBA_BLOCK = 256
MOBA_TOPK = 3
N_EXPERTS = 32
TOP_K = 4
D_EXPERT = 1024
SWIGLU_LIMIT = 7.0
SWIGLU_ALPHA = 1.702
EPS = 1e-6
NEG = -1e30

V7X_VMEM_BYTES = 64 * 1024 * 1024
V7X_MXU_DIM = 256
SUBLANES = 8
LANES = 128
BF16_SUBLANES = 16

ROW_TILE = 512
EXPERT_BLOCK_ROWS = 512
WEIGHT_CAST_ROWS = 128
DISPATCH_TILE = 1024
GATHER_TILE = 256
ISSUE_UNROLL = 8
DMA_QUEUES = 2
LRU_GROUP = V7X_MXU_DIM // RNN_BLOCK_DIM
VMEM_LIMIT = V7X_VMEM_BYTES - 8 * 1024 * 1024


def _nt_dot(a, b, **kw):
    return lax.dot_general(a, b, (((1,), (1,)), ((), ())), preferred_element_type=F32, **kw)


ROW_TILES = D_MODEL // LANES
assert ROW_TILES == SUBLANES


def _store_row_tiles(ref, val, first_row=0):
    rows = val.shape[0]
    base = first_row * ROW_TILES
    for s in range(ROW_TILES):
        ref[pl.ds(base + s, rows, stride=ROW_TILES), :] = val[:, s * LANES:(s + 1) * LANES]


def _load_row_tiles(ref, first_row=0, rows=None):
    rows = ref.shape[0] // ROW_TILES if rows is None else rows
    base = first_row * ROW_TILES
    return jnp.concatenate([ref[pl.ds(base + s, rows, stride=ROW_TILES), :] for s in range(ROW_TILES)], axis=1)


def _row_slab(ref, row):
    return ref.at[pl.ds(pl.multiple_of(row * ROW_TILES, ROW_TILES), ROW_TILES), :]


def _gelu_tanh(x):
    return 0.5 * x * (1.0 + jnp.tanh(math.sqrt(2.0 / math.pi) * (x + 0.044715 * (x * x * x))))


def _inproj_body(x_ref, g_ref, w_ref, rot_ref, rx_ref, rg_ref, q_ref, k_ref, v_ref, sr_ref, sa_ref):
    x = x_ref[...]
    ms = jnp.mean(x * x, axis=-1, keepdims=True)
    xn = (x * lax.rsqrt(ms + EPS) * g_ref[...]).astype(BF16)

    def proj(j):
        return jnp.dot(xn, w_ref[:, j * D_MODEL:(j + 1) * D_MODEL], preferred_element_type=F32)

    rg_ref[...] = _gelu_tanh(proj(1)).astype(BF16)

    cos, sin_hi, sin_lo = rot_ref[0], rot_ref[1], rot_ref[2]
    half = ROPE_DIM // 2

    def rotary_heads(z, out_ref, scale):
        for h in range(N_HEADS):
            zh = z[:, h * HEAD_DIM:(h + 1) * HEAD_DIM]
            r = zh * cos + pltpu.roll(zh, HEAD_DIM - half, axis=1) * sin_hi + pltpu.roll(zh, half, axis=1) * sin_lo
            out_ref[h] = (r * scale).astype(BF16)

    rotary_heads(proj(2), q_ref, 1.0 / math.sqrt(HEAD_DIM))
    rotary_heads(proj(3), k_ref, 1.0)
    sr_ref[...] = jax.nn.sigmoid(proj(5)).astype(BF16)
    sa_ref[...] = jax.nn.sigmoid(proj(6)).astype(BF16)
    zv = proj(4)
    for h in range(N_HEADS):
        v_ref[h] = zv[:, h * HEAD_DIM:(h + 1) * HEAD_DIM].astype(BF16)
    rx_ref[...] = proj(0).astype(BF16)


def _rotary_tables(seq):
    half = ROPE_DIM // 2
    inv_freq = ROPE_THETA ** (-jnp.arange(half, dtype=F32) * (2.0 / ROPE_DIM))
    ang = jnp.arange(seq, dtype=F32)[:, None] * inv_freq[None, :]
    cos, sin = jnp.cos(ang), jnp.sin(ang)
    pad = HEAD_DIM - ROPE_DIM
    cos_t = jnp.concatenate([cos, cos, jnp.ones((seq, pad), F32)], axis=1)
    sin_hi = jnp.concatenate([-sin, jnp.zeros((seq, half + pad), F32)], axis=1)
    sin_lo = jnp.concatenate([jnp.zeros((seq, half), F32), sin, jnp.zeros((seq, pad), F32)], axis=1)
    return jnp.stack([cos_t, sin_hi, sin_lo])


def _in_proj(x2, norm_g, w_in, seq):
    n_tok = x2.shape[0]
    tm = ROW_TILE
    tiles_per_seq = seq // tm
    in_cols = w_in.shape[1]
    row = pl.BlockSpec((tm, D_MODEL), lambda i: (i, 0))
    head = pl.BlockSpec((None, N_HEADS, tm, HEAD_DIM), lambda i: (i // tiles_per_seq, 0, i % tiles_per_seq, 0))
    bsz = n_tok // seq
    tok_bf16 = jax.ShapeDtypeStruct((n_tok, D_MODEL), BF16)
    head_bf16 = jax.ShapeDtypeStruct((bsz, N_HEADS, seq, HEAD_DIM), BF16)
    return pl.pallas_call(
        _inproj_body,
        grid=(n_tok // tm,),
        in_specs=[
            row,
            pl.BlockSpec((1, D_MODEL), lambda i: (0, 0)),
            pl.BlockSpec((D_MODEL, in_cols), lambda i: (0, 0), pipeline_mode=pl.Buffered(1)),
            pl.BlockSpec((3, tm, HEAD_DIM), lambda i: (0, i % tiles_per_seq, 0)),
        ],
        out_specs=[row, row, head, head, head, row, row],
        out_shape=[tok_bf16, tok_bf16, head_bf16, head_bf16, head_bf16, tok_bf16, tok_bf16],
        compiler_params=pltpu.CompilerParams(dimension_semantics=("parallel",), vmem_limit_bytes=VMEM_LIMIT),
        name="in_proj",
    )(x2, norm_g.reshape(1, D_MODEL), w_in.astype(BF16), _rotary_tables(seq))


def _rnn_body(rx_ref, halo_ref, rg_ref, cw_ref, cb_ref, wbd_ref, bx_ref, ba_ref, ap_ref, hg_ref,
              xe_s, alast_s, ulast_s, hin_s, h_s, hprev_s):
    s = pl.program_id(1)
    ts = rx_ref.shape[0]
    n_halo = halo_ref.shape[0]
    n_slab = xe_s.shape[0]
    n_grp = ts // SUBLANES

    halo = jnp.where(s == 0, 0.0, halo_ref[...].astype(F32))
    x = rx_ref[...].astype(F32)
    for c in range(n_slab):
        xe_s[c, 0:n_halo, :] = halo[:, c * LANES:(c + 1) * LANES]
        xe_s[c, n_halo:, :] = x[:, c * LANES:(c + 1) * LANES]

    def phase(d):
        return jnp.concatenate([xe_s[c, pl.ds(n_halo + d, n_grp, stride=SUBLANES), :] for c in range(n_slab)], axis=1)

    shifted = {d: phase(d) for d in range(-(CONV_WIDTH - 1), SUBLANES)}
    blocks = []
    for j in range(SUBLANES):
        blk = cb_ref[...]
        for tap in range(CONV_WIDTH):
            blk = blk + cw_ref[tap:tap + 1, :] * shifted[j - (CONV_WIDTH - 1) + tap]
        blocks.append(blk)
    xc = jnp.concatenate(blocks, axis=0)

    xcb = xc.astype(BF16)
    gx, ga = [], []
    for g in range(RNN_WIDTH // V7X_MXU_DIM):
        r = jnp.dot(xcb[:, g * V7X_MXU_DIM:(g + 1) * V7X_MXU_DIM], wbd_ref[g], preferred_element_type=F32)
        gx.append(r[:, :V7X_MXU_DIM])
        ga.append(r[:, V7X_MXU_DIM:])
    gate_x = jax.nn.sigmoid(jnp.concatenate(gx, axis=1) + bx_ref[...])
    gate_a = jax.nn.sigmoid(jnp.concatenate(ga, axis=1) + ba_ref[...])
    z = -ap_ref[...]
    softplus = jnp.maximum(z, 0.0) + jnp.log1p(jnp.exp(-jnp.abs(z)))
    log_a = -LRU_C * gate_a * softplus
    a = jnp.exp(log_a)
    u = jnp.sqrt(-jnp.tanh(log_a) * (a * a + 1.0)) * (gate_x * xc)

    cums = []
    a_cum = u_cum = None
    for j in range(SUBLANES):
        rows = slice(j * n_grp, (j + 1) * n_grp)
        a_j, u_j = a[rows, :], u[rows, :]
        a_cum, u_cum = (a_j, u_j) if j == 0 else (a_j * a_cum, a_j * u_cum + u_j)
        cums.append((a_cum, u_cum))
    alast_s[...] = a_cum
    ulast_s[...] = u_cum

    hin_s[0:1, :] = jnp.where(s == 0, 0.0, hprev_s[...])

    def chain(g, h):
        h = ulast_s[pl.ds(g, 1), :] + alast_s[pl.ds(g, 1), :] * h
        hin_s[pl.ds(g + 1, 1), :] = h
        return h

    hprev_s[...] = lax.fori_loop(0, n_grp, chain, hin_s[0:1, :])
    h_in = hin_s[0:n_grp, :]
    for j, (a_c, u_c) in enumerate(cums):
        h_j = u_c + a_c * h_in
        for c in range(n_slab):
            h_s[c, pl.ds(j, n_grp, stride=SUBLANES), :] = h_j[:, c * LANES:(c + 1) * LANES]
    h_all = jnp.concatenate([h_s[c] for c in range(n_slab)], axis=1)
    hg_ref[...] = (h_all * rg_ref[...].astype(F32)).astype(BF16)


def _block_diag_gates(wx, wa):
    n_grp = RNN_BLOCKS // LRU_GROUP

    def bd(w):
        w = w.reshape(n_grp, LRU_GROUP, RNN_BLOCK_DIM, RNN_BLOCK_DIM)
        eye = jnp.eye(LRU_GROUP, dtype=w.dtype)
        full = jnp.einsum('gaij,ab->gaibj', w, eye)
        return full.reshape(n_grp, V7X_MXU_DIM, V7X_MXU_DIM)

    return jnp.concatenate([bd(wx), bd(wa)], axis=2).astype(BF16)


def _rnn(rx, rg, conv_w, conv_b, wx, bx, wa, ba, a_param, bsz, seq):
    ts = ROW_TILE
    tiles_per_seq = seq // ts
    halo_rows = BF16_SUBLANES
    halo_per_tile = ts // halo_rows
    row = pl.BlockSpec((ts, RNN_WIDTH), lambda b, s: (b * tiles_per_seq + s, 0))
    halo = pl.BlockSpec((halo_rows, RNN_WIDTH),
                        lambda b, s: (jnp.maximum((b * tiles_per_seq + s) * halo_per_tile - 1, 0), 0))
    vec = pl.BlockSpec((1, RNN_WIDTH), lambda b, s: (0, 0))
    n_grp = RNN_WIDTH // V7X_MXU_DIM
    return pl.pallas_call(
        _rnn_body,
        grid=(bsz, tiles_per_seq),
        in_specs=[row, halo, row,
                  pl.BlockSpec((CONV_WIDTH, RNN_WIDTH), lambda b, s: (0, 0)), vec,
                  pl.BlockSpec((n_grp, V7X_MXU_DIM, 2 * V7X_MXU_DIM), lambda b, s: (0, 0, 0)),
                  vec, vec, vec],
        out_specs=row,
        out_shape=jax.ShapeDtypeStruct((bsz * seq, RNN_WIDTH), BF16),
        scratch_shapes=[pltpu.VMEM((RNN_WIDTH // LANES, halo_rows + ts, LANES), F32),
                        pltpu.VMEM((ts // SUBLANES, RNN_WIDTH), F32), pltpu.VMEM((ts // SUBLANES, RNN_WIDTH), F32),
                        pltpu.VMEM((ts // SUBLANES + SUBLANES, RNN_WIDTH), F32),
                        pltpu.VMEM((RNN_WIDTH // LANES, ts, LANES), F32), pltpu.VMEM((1, RNN_WIDTH), F32)],
        compiler_params=pltpu.CompilerParams(dimension_semantics=("parallel", "arbitrary"),
                                             vmem_limit_bytes=VMEM_LIMIT),
        name="rnn",
    )(rx, rx, rg, conv_w, conv_b.reshape(1, -1), _block_diag_gates(wx, wa),
      bx.reshape(1, -1), ba.reshape(1, -1), a_param.reshape(1, -1))


def _moba_body(q_ref, k_ref, v_ref, o_ref, kaug_s, qaug_s):
    seq = q_ref.shape[0]
    nb = seq // MOBA_BLOCK
    nb_pad = -(-nb // SUBLANES) * SUBLANES
    kmean = jnp.mean(k_ref[...].astype(F32).reshape(nb, MOBA_BLOCK, HEAD_DIM), axis=1)
    if nb_pad > nb:
        kmean = jnp.concatenate([kmean, jnp.zeros((nb_pad - nb, HEAD_DIM), F32)], axis=0)

    key_blk = lax.broadcasted_iota(jnp.int32, (seq, LANES), 0) // MOBA_BLOCK
    lane = lax.broadcasted_iota(jnp.int32, (seq, LANES), 1)
    kaug_s[:, :HEAD_DIM] = k_ref[...]
    kaug_s[:, HEAD_DIM:] = (key_blk == lane).astype(BF16)

    r_id = lax.broadcasted_iota(jnp.int32, (MOBA_BLOCK, MOBA_BLOCK), 0)
    c_id = lax.broadcasted_iota(jnp.int32, (MOBA_BLOCK, MOBA_BLOCK), 1)
    causal = c_id <= r_id
    eye = (c_id == r_id).astype(BF16)

    j_id = lax.broadcasted_iota(jnp.int32, (nb_pad, seq), 0)
    own = lax.broadcasted_iota(jnp.int32, (1, seq), 1) // MOBA_BLOCK
    gate = _nt_dot(kmean, q_ref[...].astype(F32), precision=lax.Precision.HIGHEST)
    rank = jnp.zeros((nb_pad, seq), F32)
    for jp in range(nb - 1):
        row = gate[jp:jp + 1, :]
        beats = ((row > gate) | ((row == gate) & (jp < j_id))) & (jp < own)
        rank = rank + beats.astype(F32)
    allowed = (j_id >= own) | (rank < float(MOBA_TOPK))
    bias_t = jnp.where(allowed, 0.0, NEG)
    bias_t = jnp.concatenate([bias_t, jnp.zeros((LANES - nb_pad, seq), F32)], axis=0).astype(BF16)
    qaug_s[:, :HEAD_DIM] = q_ref[...]
    for n in range(nb):
        cols = slice(n * MOBA_BLOCK, (n + 1) * MOBA_BLOCK)
        qaug_s[cols, HEAD_DIM:] = _nt_dot(eye, bias_t[:, cols]).astype(BF16)

    def masked_scores(n):
        s = _nt_dot(qaug_s[n * MOBA_BLOCK:(n + 1) * MOBA_BLOCK, :], kaug_s[0:(n + 1) * MOBA_BLOCK, :])
        parts = [s[:, j * MOBA_BLOCK:(j + 1) * MOBA_BLOCK] for j in range(n + 1)]
        parts[n] = jnp.where(causal, parts[n], NEG)
        return parts

    def attend(n, parts):
        m = functools.reduce(jnp.maximum, parts)
        m = jnp.max(m, axis=1, keepdims=True)
        probs = [jnp.exp(part - m) for part in parts]
        l = jnp.sum(functools.reduce(jnp.add, probs), axis=1, keepdims=True)
        p_all = jnp.concatenate([p.astype(BF16) for p in probs], axis=1)
        acc = jnp.dot(p_all, v_ref[0:(n + 1) * MOBA_BLOCK, :], preferred_element_type=F32)
        o_ref[n * MOBA_BLOCK:(n + 1) * MOBA_BLOCK, :] = (acc * (1.0 / l)).astype(BF16)

    ahead = masked_scores(0)
    for n in range(nb):
        parts = ahead
        if n + 1 < nb:
            ahead = masked_scores(n + 1)
        attend(n, parts)


def _moba(q, k, v):
    bsz, n_heads, seq, hd = q.shape
    spec = pl.BlockSpec((None, None, seq, hd), lambda b, h: (b, h, 0, 0))
    return pl.pallas_call(
        _moba_body,
        grid=(bsz, n_heads),
        in_specs=[spec, spec, spec],
        out_specs=spec,
        out_shape=jax.ShapeDtypeStruct(q.shape, BF16),
        scratch_shapes=[pltpu.VMEM((seq, 2 * hd), BF16), pltpu.VMEM((seq, 2 * hd), BF16)],
        compiler_params=pltpu.CompilerParams(dimension_semantics=("parallel", "parallel"),
                                             vmem_limit_bytes=VMEM_LIMIT),
        name="moba",
    )(q, k, v)


def _merge_body(x_ref, hg_ref, at_ref, sr_ref, sa_ref, wpr_ref, wpa_ref, wo_ref, gffn_ref, rwt_ref, rb_ref,
                x1_ref, hn_ref, idx_ref, gate_ref, rank_ref, cnt_ref, carry_s):
    tm = x_ref.shape[0]

    @pl.when(pl.program_id(0) == 0)
    def _():
        carry_s[...] = jnp.zeros_like(carry_s)

    y_rnn = jnp.dot(hg_ref[...], wpr_ref[...], preferred_element_type=F32)
    attn = jnp.concatenate([at_ref[h] for h in range(N_HEADS)], axis=1)
    y_attn = jnp.dot(attn, wpa_ref[...], preferred_element_type=F32)
    mixed = sr_ref[...].astype(F32) * y_rnn + sa_ref[...].astype(F32) * y_attn
    x1 = x_ref[...] + jnp.dot(mixed.astype(BF16), wo_ref[...], preferred_element_type=F32)
    x1_ref[...] = x1
    hn = x1 * lax.rsqrt(jnp.mean(x1 * x1, axis=-1, keepdims=True) + EPS) * gffn_ref[...]
    _store_row_tiles(hn_ref, hn)

    logits = _nt_dot(rwt_ref[...], hn.astype(BF16)) + rb_ref[...]
    e_id = lax.broadcasted_iota(jnp.int32, (N_EXPERTS, tm), 0)
    vals = logits
    onehots, top_vals, top_ids = [], [], []
    for _ in range(TOP_K):
        best = jnp.max(vals, axis=0, keepdims=True)
        best_id = jnp.min(jnp.where(vals == best, e_id, N_EXPERTS), axis=0, keepdims=True)
        hit = e_id == best_id
        vals = jnp.where(hit, -jnp.inf, vals)
        onehots.append(hit)
        top_vals.append(best)
        top_ids.append(best_id)
    exps = [jnp.exp(v - top_vals[0]) for v in top_vals]
    denom = exps[0] + exps[1] + exps[2] + exps[3]
    idx_ref[...] = jnp.concatenate(top_ids, axis=0)
    gate_ref[...] = jnp.concatenate([e / denom for e in exps], axis=0)

    chosen = onehots[0] | onehots[1] | onehots[2] | onehots[3]
    t_row = lax.broadcasted_iota(jnp.int32, (tm, tm), 0)
    t_col = lax.broadcasted_iota(jnp.int32, (tm, tm), 1)
    before = (t_row < t_col).astype(BF16)
    chosen_f = chosen.astype(F32)
    prior = jnp.dot(chosen_f.astype(BF16), before, preferred_element_type=F32) + carry_s[:, 0:1]
    ranks = [jnp.sum(jnp.where(hit, prior, 0.0), axis=0, keepdims=True) for hit in onehots]
    rank_ref[...] = jnp.concatenate(ranks, axis=0).astype(jnp.int32)
    carry_s[...] = carry_s[...] + jnp.sum(chosen_f, axis=1, keepdims=True)
    cnt_ref[...] = carry_s[...].astype(jnp.int32)


def _merge(x2, hg, attn, sr, sa, w_proj_rnn, w_proj_attn, w_out, norm_ffn_g, router_w, router_b, seq):
    n_tok = x2.shape[0]
    tm = ROW_TILE
    tiles_per_seq = seq // tm
    row = pl.BlockSpec((tm, D_MODEL), lambda i: (i, 0))
    head = pl.BlockSpec((None, N_HEADS, tm, HEAD_DIM), lambda i: (i // tiles_per_seq, 0, i % tiles_per_seq, 0))
    mat = pl.BlockSpec((D_MODEL, D_MODEL), lambda i: (0, 0))
    topk = pl.BlockSpec((TOP_K, tm), lambda i: (0, i))
    return pl.pallas_call(
        _merge_body,
        grid=(n_tok // tm,),
        in_specs=[row, row, head, row, row, mat, mat, mat,
                  pl.BlockSpec((1, D_MODEL), lambda i: (0, 0)),
                  pl.BlockSpec((N_EXPERTS, D_MODEL), lambda i: (0, 0)),
                  pl.BlockSpec((N_EXPERTS, 1), lambda i: (0, 0))],
        out_specs=[row, pl.BlockSpec((tm * ROW_TILES, LANES), lambda i: (i, 0)), topk, topk, topk,
                   pl.BlockSpec((N_EXPERTS, LANES), lambda i: (0, 0))],
        out_shape=[jax.ShapeDtypeStruct((n_tok, D_MODEL), F32),
                   jax.ShapeDtypeStruct((n_tok * ROW_TILES, LANES), F32),
                   jax.ShapeDtypeStruct((TOP_K, n_tok), jnp.int32), jax.ShapeDtypeStruct((TOP_K, n_tok), F32),
                   jax.ShapeDtypeStruct((TOP_K, n_tok), jnp.int32),
                   jax.ShapeDtypeStruct((N_EXPERTS, LANES), jnp.int32)],
        scratch_shapes=[pltpu.VMEM((N_EXPERTS, LANES), F32)],
        compiler_params=pltpu.CompilerParams(dimension_semantics=("arbitrary",), vmem_limit_bytes=VMEM_LIMIT),
        name="merge_route",
    )(x2, hg, attn, sr, sa, w_proj_rnn.astype(BF16), w_proj_attn.astype(BF16), w_out.astype(BF16),
      norm_ffn_g.reshape(1, D_MODEL), router_w.T.astype(BF16), router_b.reshape(N_EXPERTS, 1))


def _row_copy(src_ref, src_row, dst_ref, dst_row, sem):
    return pltpu.make_async_copy(_row_slab(src_ref, src_row), _row_slab(dst_ref, dst_row), sem)


def _for_each_token_group(tm, body):
    def group(g, c):
        body(pl.multiple_of(g * ISSUE_UNROLL, ISSUE_UNROLL))
        return c

    lax.fori_loop(0, tm // ISSUE_UNROLL, group, 0)


def _start_group_copies(copy_of, t0):
    for i in range(ISSUE_UNROLL):
        for k in range(TOP_K):
            copy_of(t0 + i, k).start(priority=(i * TOP_K + k) % DMA_QUEUES)


def _dispatch_body(pad_off_ref, pad_len_ref, dest_ref, hn_ref, rows_ref, sem, pad_sem):
    tm = hn_ref.shape[0] // ROW_TILES
    first = pl.program_id(0) == 0

    def for_each_pad_row(act):
        def per_expert(e, c):
            off = pad_off_ref[e]

            def per_row(r, c2):
                act(_row_copy(hn_ref, 0, rows_ref, off + r, pad_sem))
                return c2

            return lax.fori_loop(0, pad_len_ref[e], per_row, c)

        lax.fori_loop(0, N_EXPERTS, per_expert, 0)

    @pl.when(first)
    def _():
        for_each_pad_row(lambda cp: cp.start())

    scatter = lambda t, k: _row_copy(hn_ref, t, rows_ref, dest_ref[k, t], sem)
    _for_each_token_group(tm, lambda t0: _start_group_copies(scatter, t0))
    for _ in range(TOP_K):
        pltpu.make_async_copy(hn_ref, rows_ref.at[pl.ds(0, tm * ROW_TILES), :], sem).wait()

    @pl.when(first)
    def _():
        for_each_pad_row(lambda cp: cp.wait())


def _dispatch(dest, pad_off, pad_len, hn, n_rows):
    n_tok = hn.shape[0] // ROW_TILES
    tm = math.gcd(DISPATCH_TILE, n_tok)
    grid_spec = pltpu.PrefetchScalarGridSpec(
        num_scalar_prefetch=2,
        grid=(n_tok // tm,),
        in_specs=[pl.BlockSpec((TOP_K, tm), lambda i, po, pn: (0, i), memory_space=pltpu.SMEM),
                  pl.BlockSpec((tm * ROW_TILES, LANES), lambda i, po, pn: (i, 0))],
        out_specs=pl.BlockSpec(memory_space=pl.ANY),
        scratch_shapes=[pltpu.SemaphoreType.DMA(()), pltpu.SemaphoreType.DMA(())],
    )
    return pl.pallas_call(
        _dispatch_body,
        grid_spec=grid_spec,
        out_shape=jax.ShapeDtypeStruct((n_rows * ROW_TILES, LANES), F32),
        compiler_params=pltpu.CompilerParams(dimension_semantics=("arbitrary",), disable_bounds_checks=True),
        name="dispatch",
    )(pad_off, pad_len, dest, hn)


def _expert_body(blk_expert_ref, n_used_ref, next_expert_ref, slot_ref, x_ref, wgu_hbm, bgu_ref, wd_hbm, bd_ref, y_ref,
                 wgu_buf, wd_buf, wgu_s, wd_s, sems):
    i = pl.program_id(0)
    live = i < n_used_ref[0]
    expert = blk_expert_ref[i]
    new_expert = (i == 0) | (expert != blk_expert_ref[jnp.maximum(i - 1, 0)])

    def weight_copies(e, slot):
        return (pltpu.make_async_copy(wgu_hbm.at[e], wgu_buf.at[slot], sems.at[0, slot]),
                pltpu.make_async_copy(wd_hbm.at[e], wd_buf.at[slot], sems.at[1, slot]))

    @pl.when(live & new_expert)
    def _():
        slot = slot_ref[expert]
        upcoming = next_expert_ref[expert]

        @pl.when(i == 0)
        def _():
            for cp in weight_copies(expert, slot):
                cp.start()

        @pl.when(upcoming >= 0)
        def _():
            for cp in weight_copies(upcoming, 1 - slot):
                cp.start()

        for cp in weight_copies(expert, slot):
            cp.wait()
        for r in range(0, D_MODEL, WEIGHT_CAST_ROWS):
            wgu_s[r:r + WEIGHT_CAST_ROWS, :] = wgu_buf[slot, r:r + WEIGHT_CAST_ROWS, :].astype(BF16)
        for r in range(0, D_EXPERT, WEIGHT_CAST_ROWS):
            wd_s[r:r + WEIGHT_CAST_ROWS, :] = wd_buf[slot, r:r + WEIGHT_CAST_ROWS, :].astype(BF16)

    @pl.when(live)
    def _():
        gu = jnp.dot(_load_row_tiles(x_ref).astype(BF16), wgu_s[...], preferred_element_type=F32) + bgu_ref[...]
        x_glu = jnp.minimum(gu[:, :D_EXPERT], SWIGLU_LIMIT)
        x_lin = jnp.clip(gu[:, D_EXPERT:], -SWIGLU_LIMIT, SWIGLU_LIMIT)
        act = x_glu * jax.nn.sigmoid(SWIGLU_ALPHA * x_glu) * (x_lin + 1.0)
        _store_row_tiles(y_ref, jnp.dot(act.astype(BF16), wd_s[...], preferred_element_type=F32) + bd_ref[...])

    @pl.when(jnp.logical_not(live))
    def _():
        y_ref[...] = jnp.zeros_like(y_ref)


def _experts(blk_expert, n_used, next_expert, slot_of, x_rows, w_gu, b_gu, w_down, b_down):
    n_rows = x_rows.shape[0] // ROW_TILES
    rb = EXPERT_BLOCK_ROWS
    grid_spec = pltpu.PrefetchScalarGridSpec(
        num_scalar_prefetch=4,
        grid=(n_rows // rb,),
        in_specs=[
            pl.BlockSpec((rb * ROW_TILES, LANES), lambda i, be, nu, nx, sl: (jnp.minimum(i, nu[0] - 1), 0)),
            pl.BlockSpec(memory_space=pl.ANY),
            pl.BlockSpec((None, 1, 2 * D_EXPERT), lambda i, be, nu, nx, sl: (be[i], 0, 0)),
            pl.BlockSpec(memory_space=pl.ANY),
            pl.BlockSpec((None, 1, D_MODEL), lambda i, be, nu, nx, sl: (be[i], 0, 0)),
        ],
        out_specs=pl.BlockSpec((rb * ROW_TILES, LANES), lambda i, be, nu, nx, sl: (i, 0)),
        scratch_shapes=[pltpu.VMEM((2, D_MODEL, 2 * D_EXPERT), F32), pltpu.VMEM((2, D_EXPERT, D_MODEL), F32),
                        pltpu.VMEM((D_MODEL, 2 * D_EXPERT), BF16), pltpu.VMEM((D_EXPERT, D_MODEL), BF16),
                        pltpu.SemaphoreType.DMA((2, 2))],
    )
    return pl.pallas_call(
        _expert_body,
        grid_spec=grid_spec,
        out_shape=jax.ShapeDtypeStruct((n_rows * ROW_TILES, LANES), F32),
        compiler_params=pltpu.CompilerParams(dimension_semantics=("arbitrary",), vmem_limit_bytes=VMEM_LIMIT),
        name="experts",
    )(blk_expert, n_used, next_expert, slot_of, x_rows, w_gu, b_gu.reshape(N_EXPERTS, 1, -1), w_down,
      b_down.reshape(N_EXPERTS, 1, -1))


def _combine_body(dest_ref, dest_next_ref, gate_ref, x1_ref, y_ref, gfin_ref, out_ref, ybuf, sems):
    tm = x1_ref.shape[0]
    step = pl.program_id(0)
    slot = step % 2

    def gather(d_ref, into):
        fetch = lambda t, k: _row_copy(y_ref, d_ref[k, t], ybuf.at[into, k], t, sems.at[into])
        _for_each_token_group(tm, lambda t0: _start_group_copies(fetch, t0))

    @pl.when(step == 0)
    def _():
        gather(dest_ref, 0)

    @pl.when(step + 1 < pl.num_programs(0))
    def _():
        gather(dest_next_ref, 1 - slot)

    for k in range(TOP_K):
        pltpu.make_async_copy(y_ref.at[pl.ds(0, tm * ROW_TILES), :], ybuf.at[slot, k], sems.at[slot]).wait()
    gates = gate_ref[...]
    x2 = x1_ref[...]
    for k in range(TOP_K):
        x2 = x2 + gates[:, k:k + 1] * _load_row_tiles(ybuf.at[slot, k])
    out_ref[...] = x2 * lax.rsqrt(jnp.mean(x2 * x2, axis=-1, keepdims=True) + EPS) * gfin_ref[...]


def _combine(dest, gates_tok, x1, y_rows, norm_final_g):
    n_tok = x1.shape[0]
    tm = GATHER_TILE
    row = pl.BlockSpec((tm, D_MODEL), lambda i: (i, 0))
    n_tiles = n_tok // tm
    return pl.pallas_call(
        _combine_body,
        grid=(n_tiles,),
        in_specs=[pl.BlockSpec((TOP_K, tm), lambda i: (0, i), memory_space=pltpu.SMEM),
                  pl.BlockSpec((TOP_K, tm), lambda i: (0, jnp.minimum(i + 1, n_tiles - 1)), memory_space=pltpu.SMEM),
                  pl.BlockSpec((tm, TOP_K), lambda i: (i, 0)),
                  row,
                  pl.BlockSpec(memory_space=pl.ANY),
                  pl.BlockSpec((1, D_MODEL), lambda i: (0, 0))],
        out_specs=row,
        out_shape=jax.ShapeDtypeStruct((n_tok, D_MODEL), F32),
        scratch_shapes=[pltpu.VMEM((2, TOP_K, tm * ROW_TILES, LANES), F32), pltpu.SemaphoreType.DMA((2,))],
        compiler_params=pltpu.CompilerParams(dimension_semantics=("arbitrary",), vmem_limit_bytes=VMEM_LIMIT,
                                             disable_bounds_checks=True),
        name="combine",
    )(dest, dest, gates_tok, x1, y_rows, norm_final_g.reshape(1, D_MODEL))


def _routing_plan(top_idx, rank, counts, n_tok):
    padded = ((counts + EXPERT_BLOCK_ROWS - 1) // EXPERT_BLOCK_ROWS) * EXPERT_BLOCK_ROWS
    pend = jnp.cumsum(padded)
    pstart = pend - padded
    e_id = jnp.arange(N_EXPERTS, dtype=jnp.int32)[:, None, None]
    dest = jnp.sum(jnp.where(top_idx[None] == e_id, pstart[:, None, None], 0), axis=0) + rank
    n_rows = n_tok * TOP_K + N_EXPERTS * EXPERT_BLOCK_ROWS
    blk_start = jnp.arange(n_rows // EXPERT_BLOCK_ROWS, dtype=jnp.int32) * EXPERT_BLOCK_ROWS
    blk_expert = jnp.minimum(jnp.sum(blk_start[:, None] >= pend[None, :], axis=1), N_EXPERTS - 1)
    n_used = (pend[-1] // EXPERT_BLOCK_ROWS).reshape(1)
    pad_off, pad_len = pstart + counts, padded - counts
    used = counts > 0
    ids = jnp.arange(N_EXPERTS, dtype=jnp.int32)
    later = jnp.where(used[None, :] & (ids[None, :] > ids[:, None]), ids[None, :], N_EXPERTS)
    next_expert = jnp.min(later, axis=1)
    next_expert = jnp.where(next_expert == N_EXPERTS, -1, next_expert)
    slot_of = (jnp.cumsum(used) - used) % 2
    return (dest.astype(jnp.int32), blk_expert.astype(jnp.int32), n_used.astype(jnp.int32),
            next_expert.astype(jnp.int32), slot_of.astype(jnp.int32),
            pad_off.astype(jnp.int32), pad_len.astype(jnp.int32), n_rows)


def kernel(x, norm_mix_g, w_in, conv_w, conv_b, lru_wx, lru_bx, lru_wa, lru_ba, lru_a_param, w_proj_rnn, w_proj_attn, w_out, norm_ffn_g, router_w, router_b, expert_w_gu, expert_b_gu, expert_w_down, expert_b_down, norm_final_g):
    bsz, seq, _ = x.shape
    n_tok = bsz * seq
    assert w_in.shape[0] == 1, "single-layer problem: the final RMSNorm is fused into the combine stage"
    assert seq % ROW_TILE == 0 and seq % MOBA_BLOCK == 0
    layer = 0
    x2 = x.reshape(n_tok, D_MODEL)
    rx, rg, q, k, v, sr, sa = _in_proj(x2, norm_mix_g[layer], w_in[layer], seq)
    hg = _rnn(rx, rg, conv_w[layer], conv_b[layer], lru_wx[layer], lru_bx[layer], lru_wa[layer],
              lru_ba[layer], lru_a_param[layer], bsz, seq)
    attn = _moba(q, k, v)
    x1, hn, top_idx, gates, rank, cnt = _merge(x2, hg, attn, sr, sa, w_proj_rnn[layer], w_proj_attn[layer],
                                               w_out[layer], norm_ffn_g[layer], router_w[layer],
                                               router_b[layer], seq)
    dest, blk_expert, n_used, next_expert, slot_of, pad_off, pad_len, n_rows = _routing_plan(
        top_idx, rank, cnt[:, 0], n_tok)
    x_rows = _dispatch(dest, pad_off, pad_len, hn, n_rows)
    y_rows = _experts(blk_expert, n_used, next_expert, slot_of, x_rows, expert_w_gu[layer], expert_b_gu[layer],
                      expert_w_down[layer], expert_b_down[layer])
    out = _combine(dest, gates.T, x1, y_rows, norm_final_g)
    return out.reshape(bsz, seq, D_MODEL)
```

```python
import functools
import math

import jax
import jax.numpy as jnp
from jax import lax
from jax.experimental import pallas as pl
from jax.experimental.pallas import tpu as pltpu

F32 = jnp.float32
BF16 = jnp.bfloat16

D_MODEL = 1024
RNN_WIDTH = 1024
RNN_BLOCKS = 16
RNN_BLOCK_DIM = RNN_WIDTH // RNN_BLOCKS
CONV_WIDTH = 4
LRU_C = 8.0
N_HEADS = 8
HEAD_DIM = 128
ROPE_DIM = HEAD_DIM // 4
ROPE_THETA = 500000.0
MOBA_BLOCK = 256
MOBA_TOPK = 3
N_EXPERTS = 32
TOP_K = 4
D_EXPERT = 1024
SWIGLU_LIMIT = 7.0
SWIGLU_ALPHA = 1.702
EPS = 1e-6
NEG = -1e30

V7X_VMEM_BYTES = 64 * 1024 * 1024
V7X_MXU_DIM = 256
SUBLANES = 8
LANES = 128
BF16_SUBLANES = 16

ROW_TILE = 512
EXPERT_BLOCK_ROWS = 512
WEIGHT_CAST_ROWS = 128
MOBA_HEADS_PER_STEP = 2
DISPATCH_TILE = 1024
GATHER_TILE = 256
ISSUE_UNROLL = 8
DMA_QUEUES = 2
LRU_GROUP = V7X_MXU_DIM // RNN_BLOCK_DIM
VMEM_LIMIT = V7X_VMEM_BYTES - 8 * 1024 * 1024


def _nt_dot(a, b, **kw):
    return lax.dot_general(a, b, (((1,), (1,)), ((), ())), preferred_element_type=F32, **kw)


ROW_TILES = D_MODEL // LANES
assert ROW_TILES == SUBLANES


def _store_row_tiles(ref, val, first_row=0):
    rows = val.shape[0]
    base = first_row * ROW_TILES
    for s in range(ROW_TILES):
        ref[pl.ds(base + s, rows, stride=ROW_TILES), :] = val[:, s * LANES:(s + 1) * LANES]


def _load_row_tiles(ref, first_row=0, rows=None):
    rows = ref.shape[0] // ROW_TILES if rows is None else rows
    base = first_row * ROW_TILES
    return jnp.concatenate([ref[pl.ds(base + s, rows, stride=ROW_TILES), :] for s in range(ROW_TILES)], axis=1)


def _row_slab(ref, row):
    return ref.at[pl.ds(pl.multiple_of(row * ROW_TILES, ROW_TILES), ROW_TILES), :]


def _gelu_tanh(x):
    return 0.5 * x * (1.0 + jnp.tanh(math.sqrt(2.0 / math.pi) * (x + 0.044715 * (x * x * x))))


def _inproj_body(x_ref, g_ref, w_ref, rot_ref, rx_ref, rg_ref, q_ref, k_ref, v_ref, sr_ref, sa_ref):
    x = x_ref[...]
    ms = jnp.mean(x * x, axis=-1, keepdims=True)
    xn = (x * lax.rsqrt(ms + EPS) * g_ref[...]).astype(BF16)

    def proj(j):
        return jnp.dot(xn, w_ref[:, j * D_MODEL:(j + 1) * D_MODEL], preferred_element_type=F32)

    rg_ref[...] = _gelu_tanh(proj(1)).astype(BF16)

    cos, sin_hi, sin_lo = rot_ref[0], rot_ref[1], rot_ref[2]
    half = ROPE_DIM // 2

    def rotary_heads(z, out_ref, scale):
        for h in range(N_HEADS):
            zh = z[:, h * HEAD_DIM:(h + 1) * HEAD_DIM]
            r = zh * cos + pltpu.roll(zh, HEAD_DIM - half, axis=1) * sin_hi + pltpu.roll(zh, half, axis=1) * sin_lo
            out_ref[h] = (r * scale).astype(BF16)

    rotary_heads(proj(2), q_ref, 1.0 / math.sqrt(HEAD_DIM))
    rotary_heads(proj(3), k_ref, 1.0)
    sr_ref[...] = jax.nn.sigmoid(proj(5)).astype(BF16)
    sa_ref[...] = jax.nn.sigmoid(proj(6)).astype(BF16)
    zv = proj(4)
    for h in range(N_HEADS):
        v_ref[h] = zv[:, h * HEAD_DIM:(h + 1) * HEAD_DIM].astype(BF16)
    rx_ref[...] = proj(0).astype(BF16)


def _rotary_tables(seq):
    half = ROPE_DIM // 2
    inv_freq = ROPE_THETA ** (-jnp.arange(half, dtype=F32) * (2.0 / ROPE_DIM))
    ang = jnp.arange(seq, dtype=F32)[:, None] * inv_freq[None, :]
    cos, sin = jnp.cos(ang), jnp.sin(ang)
    pad = HEAD_DIM - ROPE_DIM
    cos_t = jnp.concatenate([cos, cos, jnp.ones((seq, pad), F32)], axis=1)
    sin_hi = jnp.concatenate([-sin, jnp.zeros((seq, half + pad), F32)], axis=1)
    sin_lo = jnp.concatenate([jnp.zeros((seq, half), F32), sin, jnp.zeros((seq, pad), F32)], axis=1)
    return jnp.stack([cos_t, sin_hi, sin_lo])


def _in_proj(x2, norm_g, w_in, seq):
    n_tok = x2.shape[0]
    tm = ROW_TILE
    tiles_per_seq = seq // tm
    in_cols = w_in.shape[1]
    row = pl.BlockSpec((tm, D_MODEL), lambda i: (i, 0))
    head = pl.BlockSpec((None, N_HEADS, tm, HEAD_DIM), lambda i: (i // tiles_per_seq, 0, i % tiles_per_seq, 0))
    bsz = n_tok // seq
    tok_bf16 = jax.ShapeDtypeStruct((n_tok, D_MODEL), BF16)
    head_bf16 = jax.ShapeDtypeStruct((bsz, N_HEADS, seq, HEAD_DIM), BF16)
    return pl.pallas_call(
        _inproj_body,
        grid=(n_tok // tm,),
        in_specs=[
            row,
            pl.BlockSpec((1, D_MODEL), lambda i: (0, 0)),
            pl.BlockSpec((D_MODEL, in_cols), lambda i: (0, 0), pipeline_mode=pl.Buffered(1)),
            pl.BlockSpec((3, tm, HEAD_DIM), lambda i: (0, i % tiles_per_seq, 0)),
        ],
        out_specs=[row, row, head, head, head, row, row],
        out_shape=[tok_bf16, tok_bf16, head_bf16, head_bf16, head_bf16, tok_bf16, tok_bf16],
        compiler_params=pltpu.CompilerParams(dimension_semantics=("parallel",), vmem_limit_bytes=VMEM_LIMIT),
        name="in_proj",
    )(x2, norm_g.reshape(1, D_MODEL), w_in.astype(BF16), _rotary_tables(seq))


def _rnn_body(rx_ref, halo_ref, rg_ref, cw_ref, cb_ref, wbd_ref, bx_ref, ba_ref, ap_ref, hg_ref,
              xe_s, alast_s, ulast_s, hin_s, h_s, hprev_s):
    s = pl.program_id(1)
    ts = rx_ref.shape[0]
    n_halo = halo_ref.shape[0]
    n_slab = xe_s.shape[0]
    n_grp = ts // SUBLANES

    halo = jnp.where(s == 0, 0.0, halo_ref[...].astype(F32))
    x = rx_ref[...].astype(F32)
    for c in range(n_slab):
        xe_s[c, 0:n_halo, :] = halo[:, c * LANES:(c + 1) * LANES]
        xe_s[c, n_halo:, :] = x[:, c * LANES:(c + 1) * LANES]

    def phase(d):
        return jnp.concatenate([xe_s[c, pl.ds(n_halo + d, n_grp, stride=SUBLANES), :] for c in range(n_slab)], axis=1)

    shifted = {d: phase(d) for d in range(-(CONV_WIDTH - 1), SUBLANES)}
    blocks = []
    for j in range(SUBLANES):
        blk = cb_ref[...]
        for tap in range(CONV_WIDTH):
            blk = blk + cw_ref[tap:tap + 1, :] * shifted[j - (CONV_WIDTH - 1) + tap]
        blocks.append(blk)
    xc = jnp.concatenate(blocks, axis=0)

    xcb = xc.astype(BF16)
    gx, ga = [], []
    for g in range(RNN_WIDTH // V7X_MXU_DIM):
        r = jnp.dot(xcb[:, g * V7X_MXU_DIM:(g + 1) * V7X_MXU_DIM], wbd_ref[g], preferred_element_type=F32)
        gx.append(r[:, :V7X_MXU_DIM])
        ga.append(r[:, V7X_MXU_DIM:])
    gate_x = jax.nn.sigmoid(jnp.concatenate(gx, axis=1) + bx_ref[...])
    gate_a = jax.nn.sigmoid(jnp.concatenate(ga, axis=1) + ba_ref[...])
    z = -ap_ref[...]
    softplus = jnp.maximum(z, 0.0) + jnp.log1p(jnp.exp(-jnp.abs(z)))
    log_a = -LRU_C * gate_a * softplus
    a = jnp.exp(log_a)
    u = jnp.sqrt(-jnp.tanh(log_a) * (a * a + 1.0)) * (gate_x * xc)

    cums = []
    a_cum = u_cum = None
    for j in range(SUBLANES):
        rows = slice(j * n_grp, (j + 1) * n_grp)
        a_j, u_j = a[rows, :], u[rows, :]
        a_cum, u_cum = (a_j, u_j) if j == 0 else (a_j * a_cum, a_j * u_cum + u_j)
        cums.append((a_cum, u_cum))
    alast_s[...] = a_cum
    ulast_s[...] = u_cum

    hin_s[0:1, :] = jnp.where(s == 0, 0.0, hprev_s[...])

    def chain(g, h):
        h = ulast_s[pl.ds(g, 1), :] + alast_s[pl.ds(g, 1), :] * h
        hin_s[pl.ds(g + 1, 1), :] = h
        return h

    hprev_s[...] = lax.fori_loop(0, n_grp, chain, hin_s[0:1, :])
    h_in = hin_s[0:n_grp, :]
    for j, (a_c, u_c) in enumerate(cums):
        h_j = u_c + a_c * h_in
        for c in range(n_slab):
            h_s[c, pl.ds(j, n_grp, stride=SUBLANES), :] = h_j[:, c * LANES:(c + 1) * LANES]
    h_all = jnp.concatenate([h_s[c] for c in range(n_slab)], axis=1)
    hg_ref[...] = (h_all * rg_ref[...].astype(F32)).astype(BF16)


def _block_diag_gates(wx, wa):
    n_grp = RNN_BLOCKS // LRU_GROUP

    def bd(w):
        w = w.reshape(n_grp, LRU_GROUP, RNN_BLOCK_DIM, RNN_BLOCK_DIM)
        eye = jnp.eye(LRU_GROUP, dtype=w.dtype)
        full = jnp.einsum('gaij,ab->gaibj', w, eye)
        return full.reshape(n_grp, V7X_MXU_DIM, V7X_MXU_DIM)

    return jnp.concatenate([bd(wx), bd(wa)], axis=2).astype(BF16)


def _rnn(rx, rg, conv_w, conv_b, wx, bx, wa, ba, a_param, bsz, seq):
    ts = ROW_TILE
    tiles_per_seq = seq // ts
    halo_rows = BF16_SUBLANES
    halo_per_tile = ts // halo_rows
    row = pl.BlockSpec((ts, RNN_WIDTH), lambda b, s: (b * tiles_per_seq + s, 0))
    halo = pl.BlockSpec((halo_rows, RNN_WIDTH),
                        lambda b, s: (jnp.maximum((b * tiles_per_seq + s) * halo_per_tile - 1, 0), 0))
    vec = pl.BlockSpec((1, RNN_WIDTH), lambda b, s: (0, 0))
    n_grp = RNN_WIDTH // V7X_MXU_DIM
    return pl.pallas_call(
        _rnn_body,
        grid=(bsz, tiles_per_seq),
        in_specs=[row, halo, row,
                  pl.BlockSpec((CONV_WIDTH, RNN_WIDTH), lambda b, s: (0, 0)), vec,
                  pl.BlockSpec((n_grp, V7X_MXU_DIM, 2 * V7X_MXU_DIM), lambda b, s: (0, 0, 0)),
                  vec, vec, vec],
        out_specs=row,
        out_shape=jax.ShapeDtypeStruct((bsz * seq, RNN_WIDTH), BF16),
        scratch_shapes=[pltpu.VMEM((RNN_WIDTH // LANES, halo_rows + ts, LANES), F32),
                        pltpu.VMEM((ts // SUBLANES, RNN_WIDTH), F32), pltpu.VMEM((ts // SUBLANES, RNN_WIDTH), F32),
                        pltpu.VMEM((ts // SUBLANES + SUBLANES, RNN_WIDTH), F32),
                        pltpu.VMEM((RNN_WIDTH // LANES, ts, LANES), F32), pltpu.VMEM((1, RNN_WIDTH), F32)],
        compiler_params=pltpu.CompilerParams(dimension_semantics=("parallel", "arbitrary"),
                                             vmem_limit_bytes=VMEM_LIMIT),
        name="rnn",
    )(rx, rx, rg, conv_w, conv_b.reshape(1, -1), _block_diag_gates(wx, wa),
      bx.reshape(1, -1), ba.reshape(1, -1), a_param.reshape(1, -1))


def _moba_body(q_all, k_all, v_all, o_all, kaug_all, qaug_all):
    n_heads_here, seq = q_all.shape[0], q_all.shape[1]
    nb = seq // MOBA_BLOCK
    nb_pad = -(-nb // SUBLANES) * SUBLANES
    key_blk = lax.broadcasted_iota(jnp.int32, (seq, LANES), 0) // MOBA_BLOCK
    lane = lax.broadcasted_iota(jnp.int32, (seq, LANES), 1)
    r_id = lax.broadcasted_iota(jnp.int32, (MOBA_BLOCK, MOBA_BLOCK), 0)
    c_id = lax.broadcasted_iota(jnp.int32, (MOBA_BLOCK, MOBA_BLOCK), 1)
    causal = c_id <= r_id
    eye = (c_id == r_id).astype(BF16)
    j_id = lax.broadcasted_iota(jnp.int32, (nb_pad, seq), 0)
    own = lax.broadcasted_iota(jnp.int32, (1, seq), 1) // MOBA_BLOCK

    def prepare(q_ref, k_ref, kaug_s, qaug_s):
        kmean = jnp.mean(k_ref[...].astype(F32).reshape(nb, MOBA_BLOCK, HEAD_DIM), axis=1)
        if nb_pad > nb:
            kmean = jnp.concatenate([kmean, jnp.zeros((nb_pad - nb, HEAD_DIM), F32)], axis=0)
        kaug_s[:, :HEAD_DIM] = k_ref[...]
        kaug_s[:, HEAD_DIM:] = (key_blk == lane).astype(BF16)
        gate = _nt_dot(kmean, q_ref[...].astype(F32), precision=lax.Precision.HIGHEST)
        rank = jnp.zeros((nb_pad, seq), F32)
        for jp in range(nb - 1):
            row = gate[jp:jp + 1, :]
            beats = ((row > gate) | ((row == gate) & (jp < j_id))) & (jp < own)
            rank = rank + beats.astype(F32)
        allowed = (j_id >= own) | (rank < float(MOBA_TOPK))
        bias_t = jnp.where(allowed, 0.0, NEG)
        bias_t = jnp.concatenate([bias_t, jnp.zeros((LANES - nb_pad, seq), F32)], axis=0).astype(BF16)
        qaug_s[:, :HEAD_DIM] = q_ref[...]
        for n in range(nb):
            cols = slice(n * MOBA_BLOCK, (n + 1) * MOBA_BLOCK)
            qaug_s[cols, HEAD_DIM:] = _nt_dot(eye, bias_t[:, cols]).astype(BF16)

    def masked_scores(h, n):
        s = _nt_dot(qaug_all[h, n * MOBA_BLOCK:(n + 1) * MOBA_BLOCK, :], kaug_all[h, 0:(n + 1) * MOBA_BLOCK, :])
        parts = [s[:, j * MOBA_BLOCK:(j + 1) * MOBA_BLOCK] for j in range(n + 1)]
        parts[n] = jnp.where(causal, parts[n], NEG)
        return parts

    def attend(h, n, parts):
        m = functools.reduce(jnp.maximum, parts)
        m = jnp.max(m, axis=1, keepdims=True)
        probs = [jnp.exp(part - m) for part in parts]
        l = jnp.sum(functools.reduce(jnp.add, probs), axis=1, keepdims=True)
        p_all = jnp.concatenate([p.astype(BF16) for p in probs], axis=1)
        acc = jnp.dot(p_all, v_all[h, 0:(n + 1) * MOBA_BLOCK, :], preferred_element_type=F32)
        o_all[h, n * MOBA_BLOCK:(n + 1) * MOBA_BLOCK, :] = (acc * (1.0 / l)).astype(BF16)

    for h in range(n_heads_here):
        prepare(q_all.at[h], k_all.at[h], kaug_all.at[h], qaug_all.at[h])
    ahead = [masked_scores(h, 0) for h in range(n_heads_here)]
    for n in range(nb):
        for h in range(n_heads_here):
            parts = ahead[h]
            if n + 1 < nb:
                ahead[h] = masked_scores(h, n + 1)
            attend(h, n, parts)


def _moba(q, k, v):
    bsz, n_heads, seq, hd = q.shape
    g = MOBA_HEADS_PER_STEP
    spec = pl.BlockSpec((None, g, seq, hd), lambda b, h: (b, h, 0, 0))
    return pl.pallas_call(
        _moba_body,
        grid=(bsz, n_heads // g),
        in_specs=[spec, spec, spec],
        out_specs=spec,
        out_shape=jax.ShapeDtypeStruct(q.shape, BF16),
        scratch_shapes=[pltpu.VMEM((g, seq, 2 * hd), BF16), pltpu.VMEM((g, seq, 2 * hd), BF16)],
        compiler_params=pltpu.CompilerParams(dimension_semantics=("parallel", "parallel"),
                                             vmem_limit_bytes=VMEM_LIMIT),
        name="moba",
    )(q, k, v)


def _merge_body(x_ref, hg_ref, at_ref, sr_ref, sa_ref, wpr_ref, wpa_ref, wo_ref, gffn_ref, rwt_ref, rb_ref,
                x1_ref, hn_ref, idx_ref, gate_ref, rank_ref, cnt_ref, carry_s):
    tm = x_ref.shape[0]

    @pl.when(pl.program_id(0) == 0)
    def _():
        carry_s[...] = jnp.zeros_like(carry_s)

    y_rnn = jnp.dot(hg_ref[...], wpr_ref[...], preferred_element_type=F32)
    attn = jnp.concatenate([at_ref[h] for h in range(N_HEADS)], axis=1)
    y_attn = jnp.dot(attn, wpa_ref[...], preferred_element_type=F32)
    mixed = sr_ref[...].astype(F32) * y_rnn + sa_ref[...].astype(F32) * y_attn
    x1 = x_ref[...] + jnp.dot(mixed.astype(BF16), wo_ref[...], preferred_element_type=F32)
    x1_ref[...] = x1
    hn = x1 * lax.rsqrt(jnp.mean(x1 * x1, axis=-1, keepdims=True) + EPS) * gffn_ref[...]
    _store_row_tiles(hn_ref, hn)

    logits = _nt_dot(rwt_ref[...], hn.astype(BF16)) + rb_ref[...]
    e_id = lax.broadcasted_iota(jnp.int32, (N_EXPERTS, tm), 0)
    vals = logits
    onehots, top_vals, top_ids = [], [], []
    for _ in range(TOP_K):
        best = jnp.max(vals, axis=0, keepdims=True)
        best_id = jnp.min(jnp.where(vals == best, e_id, N_EXPERTS), axis=0, keepdims=True)
        hit = e_id == best_id
        vals = jnp.where(hit, -jnp.inf, vals)
        onehots.append(hit)
        top_vals.append(best)
        top_ids.append(best_id)
    exps = [jnp.exp(v - top_vals[0]) for v in top_vals]
    denom = exps[0] + exps[1] + exps[2] + exps[3]
    idx_ref[...] = jnp.concatenate(top_ids, axis=0)
    gate_ref[...] = jnp.concatenate([e / denom for e in exps], axis=0)

    chosen = onehots[0] | onehots[1] | onehots[2] | onehots[3]
    t_row = lax.broadcasted_iota(jnp.int32, (tm, tm), 0)
    t_col = lax.broadcasted_iota(jnp.int32, (tm, tm), 1)
    before = (t_row < t_col).astype(BF16)
    chosen_f = chosen.astype(F32)
    prior = jnp.dot(chosen_f.astype(BF16), before, preferred_element_type=F32) + carry_s[:, 0:1]
    ranks = [jnp.sum(jnp.where(hit, prior, 0.0), axis=0, keepdims=True) for hit in onehots]
    rank_ref[...] = jnp.concatenate(ranks, axis=0).astype(jnp.int32)
    carry_s[...] = carry_s[...] + jnp.sum(chosen_f, axis=1, keepdims=True)
    cnt_ref[...] = carry_s[...].astype(jnp.int32)


def _merge(x2, hg, attn, sr, sa, w_proj_rnn, w_proj_attn, w_out, norm_ffn_g, router_w, router_b, seq):
    n_tok = x2.shape[0]
    tm = ROW_TILE
    tiles_per_seq = seq // tm
    row = pl.BlockSpec((tm, D_MODEL), lambda i: (i, 0))
    head = pl.BlockSpec((None, N_HEADS, tm, HEAD_DIM), lambda i: (i // tiles_per_seq, 0, i % tiles_per_seq, 0))
    mat = pl.BlockSpec((D_MODEL, D_MODEL), lambda i: (0, 0))
    topk = pl.BlockSpec((TOP_K, tm), lambda i: (0, i))
    return pl.pallas_call(
        _merge_body,
        grid=(n_tok // tm,),
        in_specs=[row, row, head, row, row, mat, mat, mat,
                  pl.BlockSpec((1, D_MODEL), lambda i: (0, 0)),
                  pl.BlockSpec((N_EXPERTS, D_MODEL), lambda i: (0, 0)),
                  pl.BlockSpec((N_EXPERTS, 1), lambda i: (0, 0))],
        out_specs=[row, pl.BlockSpec((tm * ROW_TILES, LANES), lambda i: (i, 0)), topk, topk, topk,
                   pl.BlockSpec((N_EXPERTS, LANES), lambda i: (0, 0))],
        out_shape=[jax.ShapeDtypeStruct((n_tok, D_MODEL), F32),
                   jax.ShapeDtypeStruct((n_tok * ROW_TILES, LANES), F32),
                   jax.ShapeDtypeStruct((TOP_K, n_tok), jnp.int32), jax.ShapeDtypeStruct((TOP_K, n_tok), F32),
                   jax.ShapeDtypeStruct((TOP_K, n_tok), jnp.int32),
                   jax.ShapeDtypeStruct((N_EXPERTS, LANES), jnp.int32)],
        scratch_shapes=[pltpu.VMEM((N_EXPERTS, LANES), F32)],
        compiler_params=pltpu.CompilerParams(dimension_semantics=("arbitrary",), vmem_limit_bytes=VMEM_LIMIT),
        name="merge_route",
    )(x2, hg, attn, sr, sa, w_proj_rnn.astype(BF16), w_proj_attn.astype(BF16), w_out.astype(BF16),
      norm_ffn_g.reshape(1, D_MODEL), router_w.T.astype(BF16), router_b.reshape(N_EXPERTS, 1))


def _row_copy(src_ref, src_row, dst_ref, dst_row, sem):
    return pltpu.make_async_copy(_row_slab(src_ref, src_row), _row_slab(dst_ref, dst_row), sem)


def _for_each_token_group(tm, body):
    def group(g, c):
        body(pl.multiple_of(g * ISSUE_UNROLL, ISSUE_UNROLL))
        return c

    lax.fori_loop(0, tm // ISSUE_UNROLL, group, 0)


def _start_group_copies(copy_of, t0):
    for i in range(ISSUE_UNROLL):
        for k in range(TOP_K):
            copy_of(t0 + i, k).start(priority=(i * TOP_K + k) % DMA_QUEUES)


def _dispatch_body(pad_off_ref, pad_len_ref, dest_ref, hn_ref, rows_ref, sem, pad_sem):
    tm = hn_ref.shape[0] // ROW_TILES
    first = pl.program_id(0) == 0

    def for_each_pad_run(act):
        def per_expert(e, c):
            off, length = pad_off_ref[e], pad_len_ref[e]
            for bit in range(EXPERT_BLOCK_ROWS.bit_length() - 1):
                rows = 1 << bit

                @pl.when(((length >> bit) & 1) == 1)
                def _():
                    start = pl.multiple_of((off + (length & (rows - 1))) * ROW_TILES, ROW_TILES)
                    act(pltpu.make_async_copy(hn_ref.at[pl.ds(0, rows * ROW_TILES), :],
                                              rows_ref.at[pl.ds(start, rows * ROW_TILES), :], pad_sem))
            return c

        lax.fori_loop(0, N_EXPERTS, per_expert, 0)

    @pl.when(first)
    def _():
        for_each_pad_run(lambda cp: cp.start())

    scatter = lambda t, k: _row_copy(hn_ref, t, rows_ref, dest_ref[k, t], sem)
    _for_each_token_group(tm, lambda t0: _start_group_copies(scatter, t0))
    for _ in range(TOP_K):
        pltpu.make_async_copy(hn_ref, rows_ref.at[pl.ds(0, tm * ROW_TILES), :], sem).wait()

    @pl.when(first)
    def _():
        for_each_pad_run(lambda cp: cp.wait())


def _dispatch(dest, pad_off, pad_len, hn, n_rows):
    n_tok = hn.shape[0] // ROW_TILES
    tm = math.gcd(DISPATCH_TILE, n_tok)
    assert tm >= EXPERT_BLOCK_ROWS // 2, "pad runs are sourced from the first token tile"
    grid_spec = pltpu.PrefetchScalarGridSpec(
        num_scalar_prefetch=2,
        grid=(n_tok // tm,),
        in_specs=[pl.BlockSpec((TOP_K, tm), lambda i, po, pn: (0, i), memory_space=pltpu.SMEM),
                  pl.BlockSpec((tm * ROW_TILES, LANES), lambda i, po, pn: (i, 0))],
        out_specs=pl.BlockSpec(memory_space=pl.ANY),
        scratch_shapes=[pltpu.SemaphoreType.DMA(()), pltpu.SemaphoreType.DMA(())],
    )
    return pl.pallas_call(
        _dispatch_body,
        grid_spec=grid_spec,
        out_shape=jax.ShapeDtypeStruct((n_rows * ROW_TILES, LANES), F32),
        compiler_params=pltpu.CompilerParams(dimension_semantics=("arbitrary",), disable_bounds_checks=True),
        name="dispatch",
    )(pad_off, pad_len, dest, hn)


def _expert_body(blk_expert_ref, n_used_ref, next_expert_ref, slot_ref, x_ref, wgu_hbm, bgu_ref, wd_hbm, bd_ref, y_ref,
                 wgu_buf, wd_buf, wgu_s, wd_s, sems):
    i = pl.program_id(0)
    live = i < n_used_ref[0]
    expert = blk_expert_ref[i]
    new_expert = (i == 0) | (expert != blk_expert_ref[jnp.maximum(i - 1, 0)])

    def weight_copies(e, slot):
        return (pltpu.make_async_copy(wgu_hbm.at[e], wgu_buf.at[slot], sems.at[0, slot]),
                pltpu.make_async_copy(wd_hbm.at[e], wd_buf.at[slot], sems.at[1, slot]))

    @pl.when(live & new_expert)
    def _():
        slot = slot_ref[expert]
        upcoming = next_expert_ref[expert]

        @pl.when(i == 0)
        def _():
            for cp in weight_copies(expert, slot):
                cp.start()

        @pl.when(upcoming >= 0)
        def _():
            for cp in weight_copies(upcoming, 1 - slot):
                cp.start()

        for cp in weight_copies(expert, slot):
            cp.wait()
        for r in range(0, D_MODEL, WEIGHT_CAST_ROWS):
            wgu_s[r:r + WEIGHT_CAST_ROWS, :] = wgu_buf[slot, r:r + WEIGHT_CAST_ROWS, :].astype(BF16)
        for r in range(0, D_EXPERT, WEIGHT_CAST_ROWS):
            wd_s[r:r + WEIGHT_CAST_ROWS, :] = wd_buf[slot, r:r + WEIGHT_CAST_ROWS, :].astype(BF16)

    @pl.when(live)
    def _():
        gu = jnp.dot(_load_row_tiles(x_ref).astype(BF16), wgu_s[...], preferred_element_type=F32) + bgu_ref[...]
        x_glu = jnp.minimum(gu[:, :D_EXPERT], SWIGLU_LIMIT)
        x_lin = jnp.clip(gu[:, D_EXPERT:], -SWIGLU_LIMIT, SWIGLU_LIMIT)
        act = x_glu * jax.nn.sigmoid(SWIGLU_ALPHA * x_glu) * (x_lin + 1.0)
        _store_row_tiles(y_ref, jnp.dot(act.astype(BF16), wd_s[...], preferred_element_type=F32) + bd_ref[...])

    @pl.when(jnp.logical_not(live))
    def _():
        y_ref[...] = jnp.zeros_like(y_ref)


def _experts(blk_expert, n_used, next_expert, slot_of, x_rows, w_gu, b_gu, w_down, b_down):
    n_rows = x_rows.shape[0] // ROW_TILES
    rb = EXPERT_BLOCK_ROWS
    grid_spec = pltpu.PrefetchScalarGridSpec(
        num_scalar_prefetch=4,
        grid=(n_rows // rb,),
        in_specs=[
            pl.BlockSpec((rb * ROW_TILES, LANES), lambda i, be, nu, nx, sl: (jnp.minimum(i, nu[0] - 1), 0)),
            pl.BlockSpec(memory_space=pl.ANY),
            pl.BlockSpec((None, 1, 2 * D_EXPERT), lambda i, be, nu, nx, sl: (be[i], 0, 0)),
            pl.BlockSpec(memory_space=pl.ANY),
            pl.BlockSpec((None, 1, D_MODEL), lambda i, be, nu, nx, sl: (be[i], 0, 0)),
        ],
        out_specs=pl.BlockSpec((rb * ROW_TILES, LANES), lambda i, be, nu, nx, sl: (i, 0)),
        scratch_shapes=[pltpu.VMEM((2, D_MODEL, 2 * D_EXPERT), F32), pltpu.VMEM((2, D_EXPERT, D_MODEL), F32),
                        pltpu.VMEM((D_MODEL, 2 * D_EXPERT), BF16), pltpu.VMEM((D_EXPERT, D_MODEL), BF16),
                        pltpu.SemaphoreType.DMA((2, 2))],
    )
    return pl.pallas_call(
        _expert_body,
        grid_spec=grid_spec,
        out_shape=jax.ShapeDtypeStruct((n_rows * ROW_TILES, LANES), F32),
        compiler_params=pltpu.CompilerParams(dimension_semantics=("arbitrary",), vmem_limit_bytes=VMEM_LIMIT),
        name="experts",
    )(blk_expert, n_used, next_expert, slot_of, x_rows, w_gu, b_gu.reshape(N_EXPERTS, 1, -1), w_down,
      b_down.reshape(N_EXPERTS, 1, -1))


def _combine_body(dest_ref, dest_next_ref, gate_ref, x1_ref, y_ref, gfin_ref, out_ref, ybuf, sems):
    tm = x1_ref.shape[0]
    step = pl.program_id(0)
    slot = step % 2

    def gather(d_ref, into):
        fetch = lambda t, k: _row_copy(y_ref, d_ref[k, t], ybuf.at[into, k], t, sems.at[into])
        _for_each_token_group(tm, lambda t0: _start_group_copies(fetch, t0))

    @pl.when(step == 0)
    def _():
        gather(dest_ref, 0)

    @pl.when(step + 1 < pl.num_programs(0))
    def _():
        gather(dest_next_ref, 1 - slot)

    for k in range(TOP_K):
        pltpu.make_async_copy(y_ref.at[pl.ds(0, tm * ROW_TILES), :], ybuf.at[slot, k], sems.at[slot]).wait()
    gates = gate_ref[...]
    x2 = x1_ref[...]
    for k in range(TOP_K):
        x2 = x2 + gates[:, k:k + 1] * _load_row_tiles(ybuf.at[slot, k])
    out_ref[...] = x2 * lax.rsqrt(jnp.mean(x2 * x2, axis=-1, keepdims=True) + EPS) * gfin_ref[...]


def _combine(dest, gates_tok, x1, y_rows, norm_final_g):
    n_tok = x1.shape[0]
    tm = GATHER_TILE
    row = pl.BlockSpec((tm, D_MODEL), lambda i: (i, 0))
    n_tiles = n_tok // tm
    return pl.pallas_call(
        _combine_body,
        grid=(n_tiles,),
        in_specs=[pl.BlockSpec((TOP_K, tm), lambda i: (0, i), memory_space=pltpu.SMEM),
                  pl.BlockSpec((TOP_K, tm), lambda i: (0, jnp.minimum(i + 1, n_tiles - 1)), memory_space=pltpu.SMEM),
                  pl.BlockSpec((tm, TOP_K), lambda i: (i, 0)),
                  row,
                  pl.BlockSpec(memory_space=pl.ANY),
                  pl.BlockSpec((1, D_MODEL), lambda i: (0, 0))],
        out_specs=row,
        out_shape=jax.ShapeDtypeStruct((n_tok, D_MODEL), F32),
        scratch_shapes=[pltpu.VMEM((2, TOP_K, tm * ROW_TILES, LANES), F32), pltpu.SemaphoreType.DMA((2,))],
        compiler_params=pltpu.CompilerParams(dimension_semantics=("arbitrary",), vmem_limit_bytes=VMEM_LIMIT,
                                             disable_bounds_checks=True),
        name="combine",
    )(dest, dest, gates_tok, x1, y_rows, norm_final_g.reshape(1, D_MODEL))


def _routing_plan(top_idx, rank, counts, n_tok):
    padded = ((counts + EXPERT_BLOCK_ROWS - 1) // EXPERT_BLOCK_ROWS) * EXPERT_BLOCK_ROWS
    pend = jnp.cumsum(padded)
    pstart = pend - padded
    e_id = jnp.arange(N_EXPERTS, dtype=jnp.int32)[:, None, None]
    dest = jnp.sum(jnp.where(top_idx[None] == e_id, pstart[:, None, None], 0), axis=0) + rank
    n_rows = n_tok * TOP_K + N_EXPERTS * EXPERT_BLOCK_ROWS
    blk_start = jnp.arange(n_rows // EXPERT_BLOCK_ROWS, dtype=jnp.int32) * EXPERT_BLOCK_ROWS
    blk_expert = jnp.minimum(jnp.sum(blk_start[:, None] >= pend[None, :], axis=1), N_EXPERTS - 1)
    n_used = (pend[-1] // EXPERT_BLOCK_ROWS).reshape(1)
    pad_off, pad_len = pstart + counts, padded - counts
    used = counts > 0
    ids = jnp.arange(N_EXPERTS, dtype=jnp.int32)
    later = jnp.where(used[None, :] & (ids[None, :] > ids[:, None]), ids[None, :], N_EXPERTS)
    next_expert = jnp.min(later, axis=1)
    next_expert = jnp.where(next_expert == N_EXPERTS, -1, next_expert)
    slot_of = (jnp.cumsum(used) - used) % 2
    return (dest.astype(jnp.int32), blk_expert.astype(jnp.int32), n_used.astype(jnp.int32),
            next_expert.astype(jnp.int32), slot_of.astype(jnp.int32),
            pad_off.astype(jnp.int32), pad_len.astype(jnp.int32), n_rows)


def kernel(x, norm_mix_g, w_in, conv_w, conv_b, lru_wx, lru_bx, lru_wa, lru_ba, lru_a_param, w_proj_rnn, w_proj_attn, w_out, norm_ffn_g, router_w, router_b, expert_w_gu, expert_b_gu, expert_w_down, expert_b_down, norm_final_g):
    bsz, seq, _ = x.shape
    n_tok = bsz * seq
    assert w_in.shape[0] == 1, "single-layer problem: the final RMSNorm is fused into the combine stage"
    assert seq % ROW_TILE == 0 and seq % MOBA_BLOCK == 0
    layer = 0
    x2 = x.reshape(n_tok, D_MODEL)
    rx, rg, q, k, v, sr, sa = _in_proj(x2, norm_mix_g[layer], w_in[layer], seq)
    hg = _rnn(rx, rg, conv_w[layer], conv_b[layer], lru_wx[layer], lru_bx[layer], lru_wa[layer],
              lru_ba[layer], lru_a_param[layer], bsz, seq)
    attn = _moba(q, k, v)
    x1, hn, top_idx, gates, rank, cnt = _merge(x2, hg, attn, sr, sa, w_proj_rnn[layer], w_proj_attn[layer],
                                               w_out[layer], norm_ffn_g[layer], router_w[layer],
                                               router_b[layer], seq)
    dest, blk_expert, n_used, next_expert, slot_of, pad_off, pad_len, n_rows = _routing_plan(
        top_idx, rank, cnt[:, 0], n_tok)
    x_rows = _dispatch(dest, pad_off, pad_len, hn, n_rows)
    y_rows = _experts(blk_expert, n_used, next_expert, slot_of, x_rows, expert_w_gu[layer], expert_b_gu[layer],
                      expert_w_down[layer], expert_b_down[layer])
    out = _combine(dest, gates.T, x1, y_rows, norm_final_g)
    return out.reshape(bsz, seq, D_MODEL)
```

```python
import functools
import math

import jax
import jax.numpy as jnp
from jax import lax
from jax.experimental import pallas as pl
from jax.experimental.pallas import tpu as pltpu

F32 = jnp.float32
BF16 = jnp.bfloat16

D_MODEL = 1024
RNN_WIDTH = 1024
RNN_BLOCKS = 16
RNN_BLOCK_DIM = RNN_WIDTH // RNN_BLOCKS
CONV_WIDTH = 4
LRU_C = 8.0
N_HEADS = 8
HEAD_DIM = 128
ROPE_DIM = HEAD_DIM // 4
ROPE_THETA = 500000.0
MOBA_BLOCK = 256
MOBA_TOPK = 3
N_EXPERTS = 32
TOP_K = 4
D_EXPERT = 1024
SWIGLU_LIMIT = 7.0
SWIGLU_ALPHA = 1.702
EPS = 1e-6
NEG = -1e30

V7X_VMEM_BYTES = 64 * 1024 * 1024
V7X_MXU_DIM = 256
SUBLANES = 8
LANES = 128

ROW_TILE = 512
EXPERT_BLOCK_ROWS = 512
WEIGHT_CAST_ROWS = 128
MOBA_HEADS_PER_STEP = 2
DISPATCH_TILE = 1024
GATHER_TILE = 256
ISSUE_UNROLL = 8
DMA_QUEUES = 2
LRU_GROUP = V7X_MXU_DIM // RNN_BLOCK_DIM
VMEM_LIMIT = V7X_VMEM_BYTES - 8 * 1024 * 1024


def _nt_dot(a, b, **kw):
    return lax.dot_general(a, b, (((1,), (1,)), ((), ())), preferred_element_type=F32, **kw)


ROW_TILES = D_MODEL // LANES
assert ROW_TILES == SUBLANES


def _store_row_tiles(ref, val):
    rows = val.shape[0]
    for s in range(ROW_TILES):
        ref[pl.ds(s, rows, stride=ROW_TILES), :] = val[:, s * LANES:(s + 1) * LANES]


def _load_row_tiles(ref, first_row=0, rows=None):
    rows = ref.shape[0] // ROW_TILES if rows is None else rows
    base = first_row * ROW_TILES
    return jnp.concatenate([ref[pl.ds(base + s, rows, stride=ROW_TILES), :] for s in range(ROW_TILES)], axis=1)


def _row_slab(ref, row):
    return ref.at[pl.ds(pl.multiple_of(row * ROW_TILES, ROW_TILES), ROW_TILES), :]


def _gelu_tanh(x):
    return 0.5 * x * (1.0 + jnp.tanh(math.sqrt(2.0 / math.pi) * (x + 0.044715 * (x * x * x))))


def _rotary_tables(seq):
    half = ROPE_DIM // 2
    inv_freq = ROPE_THETA ** (-jnp.arange(half, dtype=F32) * (2.0 / ROPE_DIM))
    ang = jnp.arange(seq, dtype=F32)[:, None] * inv_freq[None, :]
    cos, sin = jnp.cos(ang), jnp.sin(ang)
    pad = HEAD_DIM - ROPE_DIM
    cos_t = jnp.concatenate([cos, cos, jnp.ones((seq, pad), F32)], axis=1)
    sin_hi = jnp.concatenate([-sin, jnp.zeros((seq, half + pad), F32)], axis=1)
    sin_lo = jnp.concatenate([jnp.zeros((seq, half), F32), sin, jnp.zeros((seq, pad), F32)], axis=1)
    return jnp.stack([cos_t, sin_hi, sin_lo])


def _block_diag_gates(wx, wa):
    n_grp = RNN_BLOCKS // LRU_GROUP

    def bd(w):
        w = w.reshape(n_grp, LRU_GROUP, RNN_BLOCK_DIM, RNN_BLOCK_DIM)
        eye = jnp.eye(LRU_GROUP, dtype=w.dtype)
        full = jnp.einsum('gaij,ab->gaibj', w, eye)
        return full.reshape(n_grp, V7X_MXU_DIM, V7X_MXU_DIM)

    return jnp.concatenate([bd(wx), bd(wa)], axis=2).astype(BF16)


def _mixer_in_body(x_ref, g_ref, w_ref, rot_ref, cw_ref, cb_ref, wbd_ref, bx_ref, ba_ref, ap_ref,
                   hg_ref, q_ref, k_ref, v_ref, sr_ref, sa_ref,
                   xe_s, alast_s, ulast_s, hin_s, h_s, hprev_s):
    s = pl.program_id(1)
    tm = x_ref.shape[0]
    n_halo = xe_s.shape[1] - tm
    n_slab = xe_s.shape[0]
    n_grp = tm // SUBLANES
    x = x_ref[...]
    ms = jnp.mean(x * x, axis=-1, keepdims=True)
    xn = (x * lax.rsqrt(ms + EPS) * g_ref[...]).astype(BF16)

    def proj(j):
        return jnp.dot(xn, w_ref[:, j * D_MODEL:(j + 1) * D_MODEL], preferred_element_type=F32)

    cos, sin_hi, sin_lo = rot_ref[0], rot_ref[1], rot_ref[2]
    half = ROPE_DIM // 2

    def rotary_heads(z, out_ref, scale):
        for h in range(N_HEADS):
            zh = z[:, h * HEAD_DIM:(h + 1) * HEAD_DIM]
            r = zh * cos + pltpu.roll(zh, HEAD_DIM - half, axis=1) * sin_hi + pltpu.roll(zh, half, axis=1) * sin_lo
            out_ref[h] = (r * scale).astype(BF16)

    rx = proj(0)
    for c in range(n_slab):
        tail = xe_s[c, tm:tm + n_halo, :]
        xe_s[c, 0:n_halo, :] = jnp.where(s == 0, 0.0, tail)
        xe_s[c, n_halo:, :] = rx[:, c * LANES:(c + 1) * LANES]
    gate_gelu = _gelu_tanh(proj(1))

    def phase(d):
        return jnp.concatenate([xe_s[c, pl.ds(n_halo + d, n_grp, stride=SUBLANES), :] for c in range(n_slab)], axis=1)

    shifted = {d: phase(d) for d in range(-(CONV_WIDTH - 1), SUBLANES)}
    blocks = []
    for j in range(SUBLANES):
        blk = cb_ref[...]
        for tap in range(CONV_WIDTH):
            blk = blk + cw_ref[tap:tap + 1, :] * shifted[j - (CONV_WIDTH - 1) + tap]
        blocks.append(blk)
    xc = jnp.concatenate(blocks, axis=0)

    rotary_heads(proj(2), q_ref, 1.0 / math.sqrt(HEAD_DIM))

    xcb = xc.astype(BF16)
    gx, ga = [], []
    for g in range(RNN_WIDTH // V7X_MXU_DIM):
        r = jnp.dot(xcb[:, g * V7X_MXU_DIM:(g + 1) * V7X_MXU_DIM], wbd_ref[g], preferred_element_type=F32)
        gx.append(r[:, :V7X_MXU_DIM])
        ga.append(r[:, V7X_MXU_DIM:])
    gate_x = jax.nn.sigmoid(jnp.concatenate(gx, axis=1) + bx_ref[...])
    gate_a = jax.nn.sigmoid(jnp.concatenate(ga, axis=1) + ba_ref[...])
    rotary_heads(proj(3), k_ref, 1.0)
    z = -ap_ref[...]
    softplus = jnp.maximum(z, 0.0) + jnp.log1p(jnp.exp(-jnp.abs(z)))
    log_a = -LRU_C * gate_a * softplus
    a = jnp.exp(log_a)
    u = jnp.sqrt(-jnp.tanh(log_a) * (a * a + 1.0)) * (gate_x * xc)
    sr_ref[...] = jax.nn.sigmoid(proj(5)).astype(BF16)

    cums = []
    a_cum = u_cum = None
    for j in range(SUBLANES):
        rows = slice(j * n_grp, (j + 1) * n_grp)
        a_j, u_j = a[rows, :], u[rows, :]
        a_cum, u_cum = (a_j, u_j) if j == 0 else (a_j * a_cum, a_j * u_cum + u_j)
        cums.append((a_cum, u_cum))
    alast_s[...] = a_cum
    ulast_s[...] = u_cum
    sa_ref[...] = jax.nn.sigmoid(proj(6)).astype(BF16)
    zv = proj(4)
    for h in range(N_HEADS):
        v_ref[h] = zv[:, h * HEAD_DIM:(h + 1) * HEAD_DIM].astype(BF16)

    h = jnp.where(s == 0, 0.0, hprev_s[...])
    for g in range(n_grp):
        hin_s[g:g + 1, :] = h
        h = ulast_s[g:g + 1, :] + alast_s[g:g + 1, :] * h
    hprev_s[...] = h
    h_in = hin_s[0:n_grp, :]
    for j, (a_c, u_c) in enumerate(cums):
        h_j = u_c + a_c * h_in
        for c in range(n_slab):
            h_s[c, pl.ds(j, n_grp, stride=SUBLANES), :] = h_j[:, c * LANES:(c + 1) * LANES]
    h_all = jnp.concatenate([h_s[c] for c in range(n_slab)], axis=1)
    hg_ref[...] = (h_all * gate_gelu).astype(BF16)


def _mixer_in(x2, norm_g, w_in, conv_w, conv_b, wx, bx, wa, ba, a_param, bsz, seq):
    n_tok = x2.shape[0]
    tm = ROW_TILE
    tiles_per_seq = seq // tm
    in_cols = w_in.shape[1]
    halo_rows = SUBLANES
    const2 = lambda b, s: (0, 0)
    row = pl.BlockSpec((tm, D_MODEL), lambda b, s: (b * tiles_per_seq + s, 0))
    head = pl.BlockSpec((None, N_HEADS, tm, HEAD_DIM), lambda b, s: (b, 0, s, 0))
    vec = pl.BlockSpec((1, RNN_WIDTH), const2)
    n_grp = RNN_WIDTH // V7X_MXU_DIM
    tok_bf16 = jax.ShapeDtypeStruct((n_tok, D_MODEL), BF16)
    head_bf16 = jax.ShapeDtypeStruct((bsz, N_HEADS, seq, HEAD_DIM), BF16)
    return pl.pallas_call(
        _mixer_in_body,
        grid=(bsz, tiles_per_seq),
        in_specs=[
            row,
            pl.BlockSpec((1, D_MODEL), const2),
            pl.BlockSpec((D_MODEL, in_cols), const2, pipeline_mode=pl.Buffered(1)),
            pl.BlockSpec((3, tm, HEAD_DIM), lambda b, s: (0, s, 0)),
            pl.BlockSpec((CONV_WIDTH, RNN_WIDTH), const2), vec,
            pl.BlockSpec((n_grp, V7X_MXU_DIM, 2 * V7X_MXU_DIM), lambda b, s: (0, 0, 0)),
            vec, vec, vec,
        ],
        out_specs=[row, head, head, head, row, row],
        out_shape=[tok_bf16, head_bf16, head_bf16, head_bf16, tok_bf16, tok_bf16],
        scratch_shapes=[pltpu.VMEM((RNN_WIDTH // LANES, halo_rows + tm, LANES), F32),
                        pltpu.VMEM((tm // SUBLANES, RNN_WIDTH), F32), pltpu.VMEM((tm // SUBLANES, RNN_WIDTH), F32),
                        pltpu.VMEM((tm // SUBLANES, RNN_WIDTH), F32),
                        pltpu.VMEM((RNN_WIDTH // LANES, tm, LANES), F32), pltpu.VMEM((1, RNN_WIDTH), F32)],
        compiler_params=pltpu.CompilerParams(dimension_semantics=("parallel", "arbitrary"),
                                             vmem_limit_bytes=VMEM_LIMIT),
        name="mixer_in",
    )(x2, norm_g.reshape(1, D_MODEL), w_in.astype(BF16), _rotary_tables(seq), conv_w, conv_b.reshape(1, -1),
      _block_diag_gates(wx, wa), bx.reshape(1, -1), ba.reshape(1, -1), a_param.reshape(1, -1))


def _moba_body(q_all, k_all, v_all, o_all, kaug_all, qaug_all):
    n_heads_here, seq = q_all.shape[0], q_all.shape[1]
    nb = seq // MOBA_BLOCK
    nb_pad = -(-nb // SUBLANES) * SUBLANES
    key_blk = lax.broadcasted_iota(jnp.int32, (seq, LANES), 0) // MOBA_BLOCK
    lane = lax.broadcasted_iota(jnp.int32, (seq, LANES), 1)
    r_id = lax.broadcasted_iota(jnp.int32, (MOBA_BLOCK, MOBA_BLOCK), 0)
    c_id = lax.broadcasted_iota(jnp.int32, (MOBA_BLOCK, MOBA_BLOCK), 1)
    causal = c_id <= r_id
    eye = (c_id == r_id).astype(BF16)
    j_id = lax.broadcasted_iota(jnp.int32, (nb_pad, seq), 0)
    own = lax.broadcasted_iota(jnp.int32, (1, seq), 1) // MOBA_BLOCK

    def prepare(q_ref, k_ref, kaug_s, qaug_s):
        kmean = jnp.mean(k_ref[...].astype(F32).reshape(nb, MOBA_BLOCK, HEAD_DIM), axis=1)
        if nb_pad > nb:
            kmean = jnp.concatenate([kmean, jnp.zeros((nb_pad - nb, HEAD_DIM), F32)], axis=0)
        kaug_s[:, :HEAD_DIM] = k_ref[...]
        kaug_s[:, HEAD_DIM:] = (key_blk == lane).astype(BF16)
        gate = _nt_dot(kmean, q_ref[...].astype(F32), precision=lax.Precision.HIGHEST)
        rank = jnp.zeros((nb_pad, seq), F32)
        for jp in range(nb - 1):
            row = gate[jp:jp + 1, :]
            beats = ((row > gate) | ((row == gate) & (jp < j_id))) & (jp < own)
            rank = rank + beats.astype(F32)
        allowed = (j_id >= own) | (rank < float(MOBA_TOPK))
        bias_t = jnp.where(allowed, 0.0, NEG)
        bias_t = jnp.concatenate([bias_t, jnp.zeros((LANES - nb_pad, seq), F32)], axis=0).astype(BF16)
        qaug_s[:, :HEAD_DIM] = q_ref[...]
        for n in range(nb):
            cols = slice(n * MOBA_BLOCK, (n + 1) * MOBA_BLOCK)
            qaug_s[cols, HEAD_DIM:] = _nt_dot(eye, bias_t[:, cols]).astype(BF16)

    def masked_scores(h, n):
        s = _nt_dot(qaug_all[h, n * MOBA_BLOCK:(n + 1) * MOBA_BLOCK, :], kaug_all[h, 0:(n + 1) * MOBA_BLOCK, :])
        parts = [s[:, j * MOBA_BLOCK:(j + 1) * MOBA_BLOCK] for j in range(n + 1)]
        parts[n] = jnp.where(causal, parts[n], NEG)
        return parts

    def attend(h, n, parts):
        m = functools.reduce(jnp.maximum, parts)
        m = jnp.max(m, axis=1, keepdims=True)
        probs = [jnp.exp(part - m) for part in parts]
        l = jnp.sum(functools.reduce(jnp.add, probs), axis=1, keepdims=True)
        p_all = jnp.concatenate([p.astype(BF16) for p in probs], axis=1)
        acc = jnp.dot(p_all, v_all[h, 0:(n + 1) * MOBA_BLOCK, :], preferred_element_type=F32)
        o_all[h, n * MOBA_BLOCK:(n + 1) * MOBA_BLOCK, :] = (acc * (1.0 / l)).astype(BF16)

    for h in range(n_heads_here):
        prepare(q_all.at[h], k_all.at[h], kaug_all.at[h], qaug_all.at[h])
    ahead = [masked_scores(h, 0) for h in range(n_heads_here)]
    for n in range(nb):
        for h in range(n_heads_here):
            parts = ahead[h]
            if n + 1 < nb:
                ahead[h] = masked_scores(h, n + 1)
            attend(h, n, parts)


def _moba(q, k, v):
    bsz, n_heads, seq, hd = q.shape
    g = MOBA_HEADS_PER_STEP
    spec = pl.BlockSpec((None, g, seq, hd), lambda b, h: (b, h, 0, 0))
    return pl.pallas_call(
        _moba_body,
        grid=(bsz, n_heads // g),
        in_specs=[spec, spec, spec],
        out_specs=spec,
        out_shape=jax.ShapeDtypeStruct(q.shape, BF16),
        scratch_shapes=[pltpu.VMEM((g, seq, 2 * hd), BF16), pltpu.VMEM((g, seq, 2 * hd), BF16)],
        compiler_params=pltpu.CompilerParams(dimension_semantics=("parallel", "parallel"),
                                             vmem_limit_bytes=VMEM_LIMIT),
        name="moba",
    )(q, k, v)


def _merge_body(x_ref, hg_ref, at_ref, sr_ref, sa_ref, wpr_ref, wpa_ref, wo_ref, gffn_ref, rwt_ref, rb_ref,
                x1_ref, hn_ref, idx_ref, gate_ref, rank_ref, cnt_ref, carry_s):
    tm = x_ref.shape[0]

    @pl.when(pl.program_id(0) == 0)
    def _():
        carry_s[...] = jnp.zeros_like(carry_s)

    y_rnn = jnp.dot(hg_ref[...], wpr_ref[...], preferred_element_type=F32)
    attn = jnp.concatenate([at_ref[h] for h in range(N_HEADS)], axis=1)
    y_attn = jnp.dot(attn, wpa_ref[...], preferred_element_type=F32)
    mixed = sr_ref[...].astype(F32) * y_rnn + sa_ref[...].astype(F32) * y_attn
    x1 = x_ref[...] + jnp.dot(mixed.astype(BF16), wo_ref[...], preferred_element_type=F32)
    x1_ref[...] = x1
    hn = x1 * lax.rsqrt(jnp.mean(x1 * x1, axis=-1, keepdims=True) + EPS) * gffn_ref[...]
    _store_row_tiles(hn_ref, hn)

    logits = _nt_dot(rwt_ref[...], hn.astype(BF16)) + rb_ref[...]
    e_id = lax.broadcasted_iota(jnp.int32, (N_EXPERTS, tm), 0)
    vals = logits
    onehots, top_vals, top_ids = [], [], []
    for _ in range(TOP_K):
        best = jnp.max(vals, axis=0, keepdims=True)
        best_id = jnp.min(jnp.where(vals == best, e_id, N_EXPERTS), axis=0, keepdims=True)
        hit = e_id == best_id
        vals = jnp.where(hit, -jnp.inf, vals)
        onehots.append(hit)
        top_vals.append(best)
        top_ids.append(best_id)
    exps = [jnp.exp(v - top_vals[0]) for v in top_vals]
    denom = exps[0] + exps[1] + exps[2] + exps[3]
    idx_ref[...] = jnp.concatenate(top_ids, axis=0)
    gate_ref[...] = jnp.concatenate([e / denom for e in exps], axis=0)

    chosen = onehots[0] | onehots[1] | onehots[2] | onehots[3]
    t_row = lax.broadcasted_iota(jnp.int32, (tm, tm), 0)
    t_col = lax.broadcasted_iota(jnp.int32, (tm, tm), 1)
    before = (t_row < t_col).astype(BF16)
    chosen_f = chosen.astype(F32)
    prior = jnp.dot(chosen_f.astype(BF16), before, preferred_element_type=F32) + carry_s[:, 0:1]
    ranks = [jnp.sum(jnp.where(hit, prior, 0.0), axis=0, keepdims=True) for hit in onehots]
    rank_ref[...] = jnp.concatenate(ranks, axis=0).astype(jnp.int32)
    carry_s[...] = carry_s[...] + jnp.sum(chosen_f, axis=1, keepdims=True)
    cnt_ref[...] = carry_s[...].astype(jnp.int32)


def _merge(x2, hg, attn, sr, sa, w_proj_rnn, w_proj_attn, w_out, norm_ffn_g, router_w, router_b, seq):
    n_tok = x2.shape[0]
    tm = ROW_TILE
    tiles_per_seq = seq // tm
    row = pl.BlockSpec((tm, D_MODEL), lambda i: (i, 0))
    head = pl.BlockSpec((None, N_HEADS, tm, HEAD_DIM), lambda i: (i // tiles_per_seq, 0, i % tiles_per_seq, 0))
    mat = pl.BlockSpec((D_MODEL, D_MODEL), lambda i: (0, 0))
    topk = pl.BlockSpec((TOP_K, tm), lambda i: (0, i))
    return pl.pallas_call(
        _merge_body,
        grid=(n_tok // tm,),
        in_specs=[row, row, head, row, row, mat, mat, mat,
                  pl.BlockSpec((1, D_MODEL), lambda i: (0, 0)),
                  pl.BlockSpec((N_EXPERTS, D_MODEL), lambda i: (0, 0)),
                  pl.BlockSpec((N_EXPERTS, 1), lambda i: (0, 0))],
        out_specs=[row, pl.BlockSpec((tm * ROW_TILES, LANES), lambda i: (i, 0)), topk, topk, topk,
                   pl.BlockSpec((N_EXPERTS, LANES), lambda i: (0, 0))],
        out_shape=[jax.ShapeDtypeStruct((n_tok, D_MODEL), F32),
                   jax.ShapeDtypeStruct((n_tok * ROW_TILES, LANES), F32),
                   jax.ShapeDtypeStruct((TOP_K, n_tok), jnp.int32), jax.ShapeDtypeStruct((TOP_K, n_tok), F32),
                   jax.ShapeDtypeStruct((TOP_K, n_tok), jnp.int32),
                   jax.ShapeDtypeStruct((N_EXPERTS, LANES), jnp.int32)],
        scratch_shapes=[pltpu.VMEM((N_EXPERTS, LANES), F32)],
        compiler_params=pltpu.CompilerParams(dimension_semantics=("arbitrary",), vmem_limit_bytes=VMEM_LIMIT),
        name="merge_route",
    )(x2, hg, attn, sr, sa, w_proj_rnn.astype(BF16), w_proj_attn.astype(BF16), w_out.astype(BF16),
      norm_ffn_g.reshape(1, D_MODEL), router_w.T.astype(BF16), router_b.reshape(N_EXPERTS, 1))


def _row_copy(src_ref, src_row, dst_ref, dst_row, sem):
    return pltpu.make_async_copy(_row_slab(src_ref, src_row), _row_slab(dst_ref, dst_row), sem)


def _for_each_token_group(tm, body):
    def group(g, c):
        body(pl.multiple_of(g * ISSUE_UNROLL, ISSUE_UNROLL))
        return c

    lax.fori_loop(0, tm // ISSUE_UNROLL, group, 0)


def _start_group_copies(copy_of, t0):
    for i in range(ISSUE_UNROLL):
        for k in range(TOP_K):
            copy_of(t0 + i, k).start(priority=(i * TOP_K + k) % DMA_QUEUES)


def _dispatch_body(pad_off_ref, pad_len_ref, dest_ref, hn_ref, rows_ref, sem, pad_sem):
    tm = hn_ref.shape[0] // ROW_TILES
    first = pl.program_id(0) == 0

    def for_each_pad_run(act):
        def per_expert(e, c):
            off, length = pad_off_ref[e], pad_len_ref[e]
            for bit in range(EXPERT_BLOCK_ROWS.bit_length() - 1):
                rows = 1 << bit

                @pl.when(((length >> bit) & 1) == 1)
                def _():
                    start = pl.multiple_of((off + (length & (rows - 1))) * ROW_TILES, ROW_TILES)
                    act(pltpu.make_async_copy(hn_ref.at[pl.ds(0, rows * ROW_TILES), :],
                                              rows_ref.at[pl.ds(start, rows * ROW_TILES), :], pad_sem))
            return c

        lax.fori_loop(0, N_EXPERTS, per_expert, 0)

    @pl.when(first)
    def _():
        for_each_pad_run(lambda cp: cp.start())

    scatter = lambda t, k: _row_copy(hn_ref, t, rows_ref, dest_ref[k, t], sem)
    _for_each_token_group(tm, lambda t0: _start_group_copies(scatter, t0))
    for _ in range(TOP_K):
        pltpu.make_async_copy(hn_ref, rows_ref.at[pl.ds(0, tm * ROW_TILES), :], sem).wait()

    @pl.when(first)
    def _():
        for_each_pad_run(lambda cp: cp.wait())


def _dispatch(dest, pad_off, pad_len, hn, n_rows):
    n_tok = hn.shape[0] // ROW_TILES
    tm = math.gcd(DISPATCH_TILE, n_tok)
    assert tm >= EXPERT_BLOCK_ROWS // 2, "pad runs are sourced from the first token tile"
    grid_spec = pltpu.PrefetchScalarGridSpec(
        num_scalar_prefetch=2,
        grid=(n_tok // tm,),
        in_specs=[pl.BlockSpec((TOP_K, tm), lambda i, po, pn: (0, i), memory_space=pltpu.SMEM),
                  pl.BlockSpec((tm * ROW_TILES, LANES), lambda i, po, pn: (i, 0))],
        out_specs=pl.BlockSpec(memory_space=pl.ANY),
        scratch_shapes=[pltpu.SemaphoreType.DMA(()), pltpu.SemaphoreType.DMA(())],
    )
    return pl.pallas_call(
        _dispatch_body,
        grid_spec=grid_spec,
        out_shape=jax.ShapeDtypeStruct((n_rows * ROW_TILES, LANES), F32),
        compiler_params=pltpu.CompilerParams(dimension_semantics=("arbitrary",), disable_bounds_checks=True),
        name="dispatch",
    )(pad_off, pad_len, dest, hn)


def _expert_body(blk_expert_ref, n_used_ref, next_expert_ref, slot_ref, x_ref, wgu_hbm, bgu_ref, wd_hbm, bd_ref, y_ref,
                 wgu_buf, wd_buf, wgu_s, wd_s, sems):
    i = pl.program_id(0)
    live = i < n_used_ref[0]
    expert = blk_expert_ref[i]
    new_expert = (i == 0) | (expert != blk_expert_ref[jnp.maximum(i - 1, 0)])

    def weight_copies(e, slot):
        return (pltpu.make_async_copy(wgu_hbm.at[e], wgu_buf.at[slot], sems.at[0, slot]),
                pltpu.make_async_copy(wd_hbm.at[e], wd_buf.at[slot], sems.at[1, slot]))

    @pl.when(live & new_expert)
    def _():
        slot = slot_ref[expert]
        upcoming = next_expert_ref[expert]

        @pl.when(i == 0)
        def _():
            for cp in weight_copies(expert, slot):
                cp.start()

        @pl.when(upcoming >= 0)
        def _():
            for cp in weight_copies(upcoming, 1 - slot):
                cp.start()

        for cp in weight_copies(expert, slot):
            cp.wait()
        for r in range(0, D_MODEL, WEIGHT_CAST_ROWS):
            wgu_s[r:r + WEIGHT_CAST_ROWS, :] = wgu_buf[slot, r:r + WEIGHT_CAST_ROWS, :].astype(BF16)
        for r in range(0, D_EXPERT, WEIGHT_CAST_ROWS):
            wd_s[r:r + WEIGHT_CAST_ROWS, :] = wd_buf[slot, r:r + WEIGHT_CAST_ROWS, :].astype(BF16)

    @pl.when(live)
    def _():
        gu = jnp.dot(_load_row_tiles(x_ref).astype(BF16), wgu_s[...], preferred_element_type=F32) + bgu_ref[...]
        x_glu = jnp.minimum(gu[:, :D_EXPERT], SWIGLU_LIMIT)
        x_lin = jnp.clip(gu[:, D_EXPERT:], -SWIGLU_LIMIT, SWIGLU_LIMIT)
        act = x_glu * jax.nn.sigmoid(SWIGLU_ALPHA * x_glu) * (x_lin + 1.0)
        _store_row_tiles(y_ref, jnp.dot(act.astype(BF16), wd_s[...], preferred_element_type=F32) + bd_ref[...])

    @pl.when(jnp.logical_not(live))
    def _():
        y_ref[...] = jnp.zeros_like(y_ref)


def _experts(blk_expert, n_used, next_expert, slot_of, x_rows, w_gu, b_gu, w_down, b_down):
    n_rows = x_rows.shape[0] // ROW_TILES
    rb = EXPERT_BLOCK_ROWS
    grid_spec = pltpu.PrefetchScalarGridSpec(
        num_scalar_prefetch=4,
        grid=(n_rows // rb,),
        in_specs=[
            pl.BlockSpec((rb * ROW_TILES, LANES), lambda i, be, nu, nx, sl: (jnp.minimum(i, nu[0] - 1), 0)),
            pl.BlockSpec(memory_space=pl.ANY),
            pl.BlockSpec((None, 1, 2 * D_EXPERT), lambda i, be, nu, nx, sl: (be[i], 0, 0)),
            pl.BlockSpec(memory_space=pl.ANY),
            pl.BlockSpec((None, 1, D_MODEL), lambda i, be, nu, nx, sl: (be[i], 0, 0)),
        ],
        out_specs=pl.BlockSpec((rb * ROW_TILES, LANES), lambda i, be, nu, nx, sl: (i, 0)),
        scratch_shapes=[pltpu.VMEM((2, D_MODEL, 2 * D_EXPERT), F32), pltpu.VMEM((2, D_EXPERT, D_MODEL), F32),
                        pltpu.VMEM((D_MODEL, 2 * D_EXPERT), BF16), pltpu.VMEM((D_EXPERT, D_MODEL), BF16),
                        pltpu.SemaphoreType.DMA((2, 2))],
    )
    return pl.pallas_call(
        _expert_body,
        grid_spec=grid_spec,
        out_shape=jax.ShapeDtypeStruct((n_rows * ROW_TILES, LANES), F32),
        compiler_params=pltpu.CompilerParams(dimension_semantics=("arbitrary",), vmem_limit_bytes=VMEM_LIMIT),
        name="experts",
    )(blk_expert, n_used, next_expert, slot_of, x_rows, w_gu, b_gu.reshape(N_EXPERTS, 1, -1), w_down,
      b_down.reshape(N_EXPERTS, 1, -1))


def _combine_body(dest_ref, dest_next_ref, gate_ref, x1_ref, y_ref, gfin_ref, out_ref, ybuf, sems):
    tm = x1_ref.shape[0]
    step = pl.program_id(0)
    slot = step % 2

    def gather(d_ref, into):
        fetch = lambda t, k: _row_copy(y_ref, d_ref[k, t], ybuf.at[into, k], t, sems.at[into])
        _for_each_token_group(tm, lambda t0: _start_group_copies(fetch, t0))

    @pl.when(step == 0)
    def _():
        gather(dest_ref, 0)

    @pl.when(step + 1 < pl.num_programs(0))
    def _():
        gather(dest_next_ref, 1 - slot)

    for k in range(TOP_K):
        pltpu.make_async_copy(y_ref.at[pl.ds(0, tm * ROW_TILES), :], ybuf.at[slot, k], sems.at[slot]).wait()
    gates = gate_ref[...]
    x2 = x1_ref[...]
    for k in range(TOP_K):
        x2 = x2 + gates[:, k:k + 1] * _load_row_tiles(ybuf.at[slot, k])
    out_ref[...] = x2 * lax.rsqrt(jnp.mean(x2 * x2, axis=-1, keepdims=True) + EPS) * gfin_ref[...]


def _combine(dest, gates_tok, x1, y_rows, norm_final_g):
    n_tok = x1.shape[0]
    tm = GATHER_TILE
    row = pl.BlockSpec((tm, D_MODEL), lambda i: (i, 0))
    n_tiles = n_tok // tm
    return pl.pallas_call(
        _combine_body,
        grid=(n_tiles,),
        in_specs=[pl.BlockSpec((TOP_K, tm), lambda i: (0, i), memory_space=pltpu.SMEM),
                  pl.BlockSpec((TOP_K, tm), lambda i: (0, jnp.minimum(i + 1, n_tiles - 1)), memory_space=pltpu.SMEM),
                  pl.BlockSpec((tm, TOP_K), lambda i: (i, 0)),
                  row,
                  pl.BlockSpec(memory_space=pl.ANY),
                  pl.BlockSpec((1, D_MODEL), lambda i: (0, 0))],
        out_specs=row,
        out_shape=jax.ShapeDtypeStruct((n_tok, D_MODEL), F32),
        scratch_shapes=[pltpu.VMEM((2, TOP_K, tm * ROW_TILES, LANES), F32), pltpu.SemaphoreType.DMA((2,))],
        compiler_params=pltpu.CompilerParams(dimension_semantics=("arbitrary",), vmem_limit_bytes=VMEM_LIMIT,
                                             disable_bounds_checks=True),
        name="combine",
    )(dest, dest, gates_tok, x1, y_rows, norm_final_g.reshape(1, D_MODEL))


def _routing_plan(top_idx, rank, counts, n_tok):
    padded = ((counts + EXPERT_BLOCK_ROWS - 1) // EXPERT_BLOCK_ROWS) * EXPERT_BLOCK_ROWS
    pend = jnp.cumsum(padded)
    pstart = pend - padded
    e_id = jnp.arange(N_EXPERTS, dtype=jnp.int32)[:, None, None]
    dest = jnp.sum(jnp.where(top_idx[None] == e_id, pstart[:, None, None], 0), axis=0) + rank
    n_rows = n_tok * TOP_K + N_EXPERTS * EXPERT_BLOCK_ROWS
    blk_start = jnp.arange(n_rows // EXPERT_BLOCK_ROWS, dtype=jnp.int32) * EXPERT_BLOCK_ROWS
    blk_expert = jnp.minimum(jnp.sum(blk_start[:, None] >= pend[None, :], axis=1), N_EXPERTS - 1)
    n_used = (pend[-1] // EXPERT_BLOCK_ROWS).reshape(1)
    pad_off, pad_len = pstart + counts, padded - counts
    used = counts > 0
    ids = jnp.arange(N_EXPERTS, dtype=jnp.int32)
    later = jnp.where(used[None, :] & (ids[None, :] > ids[:, None]), ids[None, :], N_EXPERTS)
    next_expert = jnp.min(later, axis=1)
    next_expert = jnp.where(next_expert == N_EXPERTS, -1, next_expert)
    slot_of = (jnp.cumsum(used) - used) % 2
    return (dest.astype(jnp.int32), blk_expert.astype(jnp.int32), n_used.astype(jnp.int32),
            next_expert.astype(jnp.int32), slot_of.astype(jnp.int32),
            pad_off.astype(jnp.int32), pad_len.astype(jnp.int32), n_rows)


def kernel(x, norm_mix_g, w_in, conv_w, conv_b, lru_wx, lru_bx, lru_wa, lru_ba, lru_a_param, w_proj_rnn, w_proj_attn, w_out, norm_ffn_g, router_w, router_b, expert_w_gu, expert_b_gu, expert_w_down, expert_b_down, norm_final_g):
    bsz, seq, _ = x.shape
    n_tok = bsz * seq
    assert w_in.shape[0] == 1, "single-layer problem: the final RMSNorm is fused into the combine stage"
    assert seq % ROW_TILE == 0 and seq % MOBA_BLOCK == 0
    layer = 0
    x2 = x.reshape(n_tok, D_MODEL)
    hg, q, k, v, sr, sa = _mixer_in(x2, norm_mix_g[layer], w_in[layer], conv_w[layer], conv_b[layer],
                                    lru_wx[layer], lru_bx[layer], lru_wa[layer], lru_ba[layer],
                                    lru_a_param[layer], bsz, seq)
    attn = _moba(q, k, v)
    x1, hn, top_idx, gates, rank, cnt = _merge(x2, hg, attn, sr, sa, w_proj_rnn[layer], w_proj_attn[layer],
                                               w_out[layer], norm_ffn_g[layer], router_w[layer],
                                               router_b[layer], seq)
    dest, blk_expert, n_used, next_expert, slot_of, pad_off, pad_len, n_rows = _routing_plan(
        top_idx, rank, cnt[:, 0], n_tok)
    x_rows = _dispatch(dest, pad_off, pad_len, hn, n_rows)
    y_rows = _experts(blk_expert, n_used, next_expert, slot_of, x_rows, expert_w_gu[layer], expert_b_gu[layer],
                      expert_w_down[layer], expert_b_down[layer])
    out = _combine(dest, gates.T, x1, y_rows, norm_final_g)
    return out.reshape(bsz, seq, D_MODEL)
```

```python
import functools
import math

import jax
import jax.numpy as jnp
from jax import lax
from jax.experimental import pallas as pl
from jax.experimental.pallas import tpu as pltpu

F32 = jnp.float32
BF16 = jnp.bfloat16

D_MODEL = 1024
RNN_WIDTH = 1024
RNN_BLOCKS = 16
RNN_BLOCK_DIM = RNN_WIDTH // RNN_BLOCKS
CONV_WIDTH = 4
LRU_C = 8.0
N_HEADS = 8
HEAD_DIM = 128
ROPE_DIM = HEAD_DIM // 4
ROPE_THETA = 500000.0
MOBA_BLOCK = 256
MOBA_TOPK = 3
N_EXPERTS = 32
TOP_K = 4
D_EXPERT = 1024
SWIGLU_LIMIT = 7.0
SWIGLU_ALPHA = 1.702
EPS = 1e-6
NEG = -1e30

V7X_VMEM_BYTES = 64 * 1024 * 1024
V7X_MXU_DIM = 256
SUBLANES = 8
LANES = 128

ROW_TILE = 512
EXPERT_BLOCK_ROWS = 512
WEIGHT_CAST_ROWS = 128
MOBA_HEADS_PER_STEP = 2
DISPATCH_TILE = 1024
DISPATCH_SLOTS = 3
GATHER_TILE = 256
ISSUE_UNROLL = 8
DMA_QUEUES = 2
LRU_GROUP = V7X_MXU_DIM // RNN_BLOCK_DIM
VMEM_LIMIT = V7X_VMEM_BYTES - 8 * 1024 * 1024


def _nt_dot(a, b, **kw):
    return lax.dot_general(a, b, (((1,), (1,)), ((), ())), preferred_element_type=F32, **kw)


ROW_TILES = D_MODEL // LANES
assert ROW_TILES == SUBLANES


def _store_row_tiles(ref, val):
    rows = val.shape[0]
    for s in range(ROW_TILES):
        ref[pl.ds(s, rows, stride=ROW_TILES), :] = val[:, s * LANES:(s + 1) * LANES]


def _load_row_tiles(ref, first_row=0, rows=None):
    rows = ref.shape[0] // ROW_TILES if rows is None else rows
    base = first_row * ROW_TILES
    return jnp.concatenate([ref[pl.ds(base + s, rows, stride=ROW_TILES), :] for s in range(ROW_TILES)], axis=1)


def _row_slab(ref, row):
    return ref.at[pl.ds(pl.multiple_of(row * ROW_TILES, ROW_TILES), ROW_TILES), :]


def _gelu_tanh(x):
    return 0.5 * x * (1.0 + jnp.tanh(math.sqrt(2.0 / math.pi) * (x + 0.044715 * (x * x * x))))


def _rotary_tables(seq):
    half = ROPE_DIM // 2
    inv_freq = ROPE_THETA ** (-jnp.arange(half, dtype=F32) * (2.0 / ROPE_DIM))
    ang = jnp.arange(seq, dtype=F32)[:, None] * inv_freq[None, :]
    cos, sin = jnp.cos(ang), jnp.sin(ang)
    pad = HEAD_DIM - ROPE_DIM
    cos_t = jnp.concatenate([cos, cos, jnp.ones((seq, pad), F32)], axis=1)
    sin_hi = jnp.concatenate([-sin, jnp.zeros((seq, half + pad), F32)], axis=1)
    sin_lo = jnp.concatenate([jnp.zeros((seq, half), F32), sin, jnp.zeros((seq, pad), F32)], axis=1)
    return jnp.stack([cos_t, sin_hi, sin_lo])


def _block_diag_gates(wx, wa):
    n_grp = RNN_BLOCKS // LRU_GROUP

    def bd(w):
        w = w.reshape(n_grp, LRU_GROUP, RNN_BLOCK_DIM, RNN_BLOCK_DIM)
        eye = jnp.eye(LRU_GROUP, dtype=w.dtype)
        full = jnp.einsum('gaij,ab->gaibj', w, eye)
        return full.reshape(n_grp, V7X_MXU_DIM, V7X_MXU_DIM)

    return jnp.concatenate([bd(wx), bd(wa)], axis=2).astype(BF16)


def _mixer_in_body(x_ref, g_ref, w_ref, rot_ref, cw_ref, cb_ref, wbd_ref, bx_ref, ba_ref, ap_ref,
                   hg_ref, q_ref, k_ref, v_ref, sr_ref, sa_ref,
                   xe_s, alast_s, ulast_s, hin_s, h_s, hprev_s):
    s = pl.program_id(1)
    tm = x_ref.shape[0]
    n_halo = xe_s.shape[1] - tm
    n_slab = xe_s.shape[0]
    n_grp = tm // SUBLANES
    x = x_ref[...]
    ms = jnp.mean(x * x, axis=-1, keepdims=True)
    xn = (x * lax.rsqrt(ms + EPS) * g_ref[...]).astype(BF16)

    def proj(j):
        return jnp.dot(xn, w_ref[:, j * D_MODEL:(j + 1) * D_MODEL], preferred_element_type=F32)

    cos, sin_hi, sin_lo = rot_ref[0], rot_ref[1], rot_ref[2]
    half = ROPE_DIM // 2

    def rotary_heads(z, out_ref, scale):
        for h in range(N_HEADS):
            zh = z[:, h * HEAD_DIM:(h + 1) * HEAD_DIM]
            r = zh * cos + pltpu.roll(zh, HEAD_DIM - half, axis=1) * sin_hi + pltpu.roll(zh, half, axis=1) * sin_lo
            out_ref[h] = (r * scale).astype(BF16)

    rx = proj(0)
    for c in range(n_slab):
        tail = xe_s[c, tm:tm + n_halo, :]
        xe_s[c, 0:n_halo, :] = jnp.where(s == 0, 0.0, tail)
        xe_s[c, n_halo:, :] = rx[:, c * LANES:(c + 1) * LANES]
    gate_gelu = _gelu_tanh(proj(1))

    def recurrent_group(g):
        cols = slice(g * V7X_MXU_DIM, (g + 1) * V7X_MXU_DIM)
        slabs = range(g * V7X_MXU_DIM // LANES, (g + 1) * V7X_MXU_DIM // LANES)

        def phase(d):
            return jnp.concatenate([xe_s[c, pl.ds(n_halo + d, n_grp, stride=SUBLANES), :] for c in slabs], axis=1)

        shifted = {d: phase(d) for d in range(-(CONV_WIDTH - 1), SUBLANES)}
        blocks = []
        for j in range(SUBLANES):
            blk = cb_ref[:, cols]
            for tap in range(CONV_WIDTH):
                blk = blk + cw_ref[tap:tap + 1, cols] * shifted[j - (CONV_WIDTH - 1) + tap]
            blocks.append(blk)
        xc = jnp.concatenate(blocks, axis=0)

        r = jnp.dot(xc.astype(BF16), wbd_ref[g], preferred_element_type=F32)
        gate_x = jax.nn.sigmoid(r[:, :V7X_MXU_DIM] + bx_ref[:, cols])
        gate_a = jax.nn.sigmoid(r[:, V7X_MXU_DIM:] + ba_ref[:, cols])
        z = -ap_ref[:, cols]
        softplus = jnp.maximum(z, 0.0) + jnp.log1p(jnp.exp(-jnp.abs(z)))
        log_a = -LRU_C * gate_a * softplus
        a = jnp.exp(log_a)
        u = jnp.sqrt(-jnp.tanh(log_a) * (a * a + 1.0)) * (gate_x * xc)

        cums = []
        a_cum = u_cum = None
        for j in range(SUBLANES):
            rows = slice(j * n_grp, (j + 1) * n_grp)
            a_j, u_j = a[rows, :], u[rows, :]
            a_cum, u_cum = (a_j, u_j) if j == 0 else (a_j * a_cum, a_j * u_cum + u_j)
            cums.append((a_cum, u_cum))
        alast_s[:, cols] = a_cum
        ulast_s[:, cols] = u_cum
        h = jnp.where(s == 0, 0.0, hprev_s[:, cols])
        for i in range(n_grp):
            hin_s[i:i + 1, cols] = h
            h = ulast_s[i:i + 1, cols] + alast_s[i:i + 1, cols] * h
        hprev_s[:, cols] = h
        h_in = hin_s[0:n_grp, cols]
        for j, (a_c, u_c) in enumerate(cums):
            h_j = u_c + a_c * h_in
            for k, c in enumerate(slabs):
                h_s[c, pl.ds(j, n_grp, stride=SUBLANES), :] = h_j[:, k * LANES:(k + 1) * LANES]
        h_grp = jnp.concatenate([h_s[c] for c in slabs], axis=1)
        hg_ref[:, cols] = (h_grp * gate_gelu[:, cols]).astype(BF16)

    recurrent_group(0)
    rotary_heads(proj(2), q_ref, 1.0 / math.sqrt(HEAD_DIM))
    recurrent_group(1)
    rotary_heads(proj(3), k_ref, 1.0)
    recurrent_group(2)
    sr_ref[...] = jax.nn.sigmoid(proj(5)).astype(BF16)
    sa_ref[...] = jax.nn.sigmoid(proj(6)).astype(BF16)
    recurrent_group(3)
    zv = proj(4)
    for h in range(N_HEADS):
        v_ref[h] = zv[:, h * HEAD_DIM:(h + 1) * HEAD_DIM].astype(BF16)


def _mixer_in(x2, norm_g, w_in, conv_w, conv_b, wx, bx, wa, ba, a_param, bsz, seq):
    n_tok = x2.shape[0]
    tm = ROW_TILE
    tiles_per_seq = seq // tm
    in_cols = w_in.shape[1]
    halo_rows = SUBLANES
    const2 = lambda b, s: (0, 0)
    row = pl.BlockSpec((tm, D_MODEL), lambda b, s: (b * tiles_per_seq + s, 0))
    head = pl.BlockSpec((None, N_HEADS, tm, HEAD_DIM), lambda b, s: (b, 0, s, 0))
    vec = pl.BlockSpec((1, RNN_WIDTH), const2)
    n_grp = RNN_WIDTH // V7X_MXU_DIM
    tok_bf16 = jax.ShapeDtypeStruct((n_tok, D_MODEL), BF16)
    head_bf16 = jax.ShapeDtypeStruct((bsz, N_HEADS, seq, HEAD_DIM), BF16)
    return pl.pallas_call(
        _mixer_in_body,
        grid=(bsz, tiles_per_seq),
        in_specs=[
            row,
            pl.BlockSpec((1, D_MODEL), const2),
            pl.BlockSpec((D_MODEL, in_cols), const2, pipeline_mode=pl.Buffered(1)),
            pl.BlockSpec((3, tm, HEAD_DIM), lambda b, s: (0, s, 0)),
            pl.BlockSpec((CONV_WIDTH, RNN_WIDTH), const2), vec,
            pl.BlockSpec((n_grp, V7X_MXU_DIM, 2 * V7X_MXU_DIM), lambda b, s: (0, 0, 0)),
            vec, vec, vec,
        ],
        out_specs=[row, head, head, head, row, row],
        out_shape=[tok_bf16, head_bf16, head_bf16, head_bf16, tok_bf16, tok_bf16],
        scratch_shapes=[pltpu.VMEM((RNN_WIDTH // LANES, halo_rows + tm, LANES), F32),
                        pltpu.VMEM((tm // SUBLANES, RNN_WIDTH), F32), pltpu.VMEM((tm // SUBLANES, RNN_WIDTH), F32),
                        pltpu.VMEM((tm // SUBLANES, RNN_WIDTH), F32),
                        pltpu.VMEM((RNN_WIDTH // LANES, tm, LANES), F32), pltpu.VMEM((1, RNN_WIDTH), F32)],
        compiler_params=pltpu.CompilerParams(dimension_semantics=("parallel", "arbitrary"),
                                             vmem_limit_bytes=VMEM_LIMIT),
        name="mixer_in",
    )(x2, norm_g.reshape(1, D_MODEL), w_in.astype(BF16), _rotary_tables(seq), conv_w, conv_b.reshape(1, -1),
      _block_diag_gates(wx, wa), bx.reshape(1, -1), ba.reshape(1, -1), a_param.reshape(1, -1))


def _moba_body(q_all, k_all, v_all, o_all, kaug_all, qaug_all):
    n_heads_here, seq = q_all.shape[0], q_all.shape[1]
    nb = seq // MOBA_BLOCK
    nb_pad = -(-nb // SUBLANES) * SUBLANES
    key_blk = lax.broadcasted_iota(jnp.int32, (seq, LANES), 0) // MOBA_BLOCK
    lane = lax.broadcasted_iota(jnp.int32, (seq, LANES), 1)
    r_id = lax.broadcasted_iota(jnp.int32, (MOBA_BLOCK, MOBA_BLOCK), 0)
    c_id = lax.broadcasted_iota(jnp.int32, (MOBA_BLOCK, MOBA_BLOCK), 1)
    causal = c_id <= r_id
    eye = (c_id == r_id).astype(BF16)
    j_id = lax.broadcasted_iota(jnp.int32, (nb_pad, seq), 0)
    own = lax.broadcasted_iota(jnp.int32, (1, seq), 1) // MOBA_BLOCK

    def prepare(q_ref, k_ref, kaug_s, qaug_s):
        kmean = jnp.mean(k_ref[...].astype(F32).reshape(nb, MOBA_BLOCK, HEAD_DIM), axis=1)
        if nb_pad > nb:
            kmean = jnp.concatenate([kmean, jnp.zeros((nb_pad - nb, HEAD_DIM), F32)], axis=0)
        kaug_s[:, :HEAD_DIM] = k_ref[...]
        kaug_s[:, HEAD_DIM:] = (key_blk == lane).astype(BF16)
        gate = _nt_dot(kmean, q_ref[...].astype(F32), precision=lax.Precision.HIGHEST)
        rank = jnp.zeros((nb_pad, seq), F32)
        for jp in range(nb - 1):
            row = gate[jp:jp + 1, :]
            beats = ((row > gate) | ((row == gate) & (jp < j_id))) & (jp < own)
            rank = rank + beats.astype(F32)
        allowed = (j_id >= own) | (rank < float(MOBA_TOPK))
        bias_t = jnp.where(allowed, 0.0, NEG)
        bias_t = jnp.concatenate([bias_t, jnp.zeros((LANES - nb_pad, seq), F32)], axis=0).astype(BF16)
        qaug_s[:, :HEAD_DIM] = q_ref[...]
        for n in range(nb):
            cols = slice(n * MOBA_BLOCK, (n + 1) * MOBA_BLOCK)
            qaug_s[cols, HEAD_DIM:] = _nt_dot(eye, bias_t[:, cols]).astype(BF16)

    def masked_scores(h, n):
        s = _nt_dot(qaug_all[h, n * MOBA_BLOCK:(n + 1) * MOBA_BLOCK, :], kaug_all[h, 0:(n + 1) * MOBA_BLOCK, :])
        parts = [s[:, j * MOBA_BLOCK:(j + 1) * MOBA_BLOCK] for j in range(n + 1)]
        parts[n] = jnp.where(causal, parts[n], NEG)
        return parts

    def attend(h, n, parts):
        m = functools.reduce(jnp.maximum, parts)
        m = jnp.max(m, axis=1, keepdims=True)
        probs = [jnp.exp(part - m) for part in parts]
        l = jnp.sum(functools.reduce(jnp.add, probs), axis=1, keepdims=True)
        p_all = jnp.concatenate([p.astype(BF16) for p in probs], axis=1)
        acc = jnp.dot(p_all, v_all[h, 0:(n + 1) * MOBA_BLOCK, :], preferred_element_type=F32)
        o_all[h, n * MOBA_BLOCK:(n + 1) * MOBA_BLOCK, :] = (acc * (1.0 / l)).astype(BF16)

    for h in range(n_heads_here):
        prepare(q_all.at[h], k_all.at[h], kaug_all.at[h], qaug_all.at[h])
    ahead = [masked_scores(h, 0) for h in range(n_heads_here)]
    for n in range(nb):
        for h in range(n_heads_here):
            parts = ahead[h]
            if n + 1 < nb:
                ahead[h] = masked_scores(h, n + 1)
            attend(h, n, parts)


def _moba(q, k, v):
    bsz, n_heads, seq, hd = q.shape
    g = MOBA_HEADS_PER_STEP
    spec = pl.BlockSpec((None, g, seq, hd), lambda b, h: (b, h, 0, 0))
    return pl.pallas_call(
        _moba_body,
        grid=(bsz, n_heads // g),
        in_specs=[spec, spec, spec],
        out_specs=spec,
        out_shape=jax.ShapeDtypeStruct(q.shape, BF16),
        scratch_shapes=[pltpu.VMEM((g, seq, 2 * hd), BF16), pltpu.VMEM((g, seq, 2 * hd), BF16)],
        compiler_params=pltpu.CompilerParams(dimension_semantics=("parallel", "parallel"),
                                             vmem_limit_bytes=VMEM_LIMIT),
        name="moba",
    )(q, k, v)


def _merge_body(x_ref, hg_ref, at_ref, sr_ref, sa_ref, wpr_ref, wpa_ref, wo_ref, gffn_ref, rwt_ref, rb_ref,
                x1_ref, hn_ref, idx_ref, gate_ref, rank_ref, cnt_ref, carry_s):
    tm = x_ref.shape[0]

    @pl.when(pl.program_id(0) == 0)
    def _():
        carry_s[...] = jnp.zeros_like(carry_s)

    y_rnn = jnp.dot(hg_ref[...], wpr_ref[...], preferred_element_type=F32)
    attn = jnp.concatenate([at_ref[h] for h in range(N_HEADS)], axis=1)
    y_attn = jnp.dot(attn, wpa_ref[...], preferred_element_type=F32)
    mixed = sr_ref[...].astype(F32) * y_rnn + sa_ref[...].astype(F32) * y_attn
    x1 = x_ref[...] + jnp.dot(mixed.astype(BF16), wo_ref[...], preferred_element_type=F32)
    x1_ref[...] = x1
    hn = x1 * lax.rsqrt(jnp.mean(x1 * x1, axis=-1, keepdims=True) + EPS) * gffn_ref[...]
    _store_row_tiles(hn_ref, hn)

    logits = _nt_dot(rwt_ref[...], hn.astype(BF16)) + rb_ref[...]
    e_id = lax.broadcasted_iota(jnp.int32, (N_EXPERTS, tm), 0)
    vals = logits
    onehots, top_vals, top_ids = [], [], []
    for _ in range(TOP_K):
        best = jnp.max(vals, axis=0, keepdims=True)
        best_id = jnp.min(jnp.where(vals == best, e_id, N_EXPERTS), axis=0, keepdims=True)
        hit = e_id == best_id
        vals = jnp.where(hit, -jnp.inf, vals)
        onehots.append(hit)
        top_vals.append(best)
        top_ids.append(best_id)
    exps = [jnp.exp(v - top_vals[0]) for v in top_vals]
    denom = exps[0] + exps[1] + exps[2] + exps[3]
    idx_ref[...] = jnp.concatenate(top_ids, axis=0)
    gate_ref[...] = jnp.concatenate([e / denom for e in exps], axis=0)

    chosen = onehots[0] | onehots[1] | onehots[2] | onehots[3]
    t_row = lax.broadcasted_iota(jnp.int32, (tm, tm), 0)
    t_col = lax.broadcasted_iota(jnp.int32, (tm, tm), 1)
    before = (t_row < t_col).astype(BF16)
    chosen_f = chosen.astype(F32)
    prior = jnp.dot(chosen_f.astype(BF16), before, preferred_element_type=F32) + carry_s[:, 0:1]
    ranks = [jnp.sum(jnp.where(hit, prior, 0.0), axis=0, keepdims=True) for hit in onehots]
    rank_ref[...] = jnp.concatenate(ranks, axis=0).astype(jnp.int32)
    carry_s[...] = carry_s[...] + jnp.sum(chosen_f, axis=1, keepdims=True)
    cnt_ref[...] = carry_s[...].astype(jnp.int32)


def _merge(x2, hg, attn, sr, sa, w_proj_rnn, w_proj_attn, w_out, norm_ffn_g, router_w, router_b, seq):
    n_tok = x2.shape[0]
    tm = ROW_TILE
    tiles_per_seq = seq // tm
    row = pl.BlockSpec((tm, D_MODEL), lambda i: (i, 0))
    head = pl.BlockSpec((None, N_HEADS, tm, HEAD_DIM), lambda i: (i // tiles_per_seq, 0, i % tiles_per_seq, 0))
    mat = pl.BlockSpec((D_MODEL, D_MODEL), lambda i: (0, 0))
    topk = pl.BlockSpec((TOP_K, tm), lambda i: (0, i))
    return pl.pallas_call(
        _merge_body,
        grid=(n_tok // tm,),
        in_specs=[row, row, head, row, row, mat, mat, mat,
                  pl.BlockSpec((1, D_MODEL), lambda i: (0, 0)),
                  pl.BlockSpec((N_EXPERTS, D_MODEL), lambda i: (0, 0)),
                  pl.BlockSpec((N_EXPERTS, 1), lambda i: (0, 0))],
        out_specs=[row, pl.BlockSpec((tm * ROW_TILES, LANES), lambda i: (i, 0)), topk, topk, topk,
                   pl.BlockSpec((N_EXPERTS, LANES), lambda i: (0, 0))],
        out_shape=[jax.ShapeDtypeStruct((n_tok, D_MODEL), F32),
                   jax.ShapeDtypeStruct((n_tok * ROW_TILES, LANES), F32),
                   jax.ShapeDtypeStruct((TOP_K, n_tok), jnp.int32), jax.ShapeDtypeStruct((TOP_K, n_tok), F32),
                   jax.ShapeDtypeStruct((TOP_K, n_tok), jnp.int32),
                   jax.ShapeDtypeStruct((N_EXPERTS, LANES), jnp.int32)],
        scratch_shapes=[pltpu.VMEM((N_EXPERTS, LANES), F32)],
        compiler_params=pltpu.CompilerParams(dimension_semantics=("arbitrary",), vmem_limit_bytes=VMEM_LIMIT),
        name="merge_route",
    )(x2, hg, attn, sr, sa, w_proj_rnn.astype(BF16), w_proj_attn.astype(BF16), w_out.astype(BF16),
      norm_ffn_g.reshape(1, D_MODEL), router_w.T.astype(BF16), router_b.reshape(N_EXPERTS, 1))


def _row_copy(src_ref, src_row, dst_ref, dst_row, sem):
    return pltpu.make_async_copy(_row_slab(src_ref, src_row), _row_slab(dst_ref, dst_row), sem)


def _for_each_token_group(tm, body):
    def group(g, c):
        body(pl.multiple_of(g * ISSUE_UNROLL, ISSUE_UNROLL))
        return c

    lax.fori_loop(0, tm // ISSUE_UNROLL, group, 0)


def _start_group_copies(copy_of, t0):
    for i in range(ISSUE_UNROLL):
        for k in range(TOP_K):
            copy_of(t0 + i, k).start(priority=(i * TOP_K + k) % DMA_QUEUES)


def _dispatch_body(pad_off_ref, pad_len_ref, dest_ref, hn_hbm, rows_ref, hbuf, load_sems, scat_sems, pad_sem):
    tm = hbuf.shape[1] // ROW_TILES
    tile_rows = tm * ROW_TILES
    step = pl.program_id(0)
    n_steps = pl.num_programs(0)
    slot = step % DISPATCH_SLOTS
    prev_slot = (step + DISPATCH_SLOTS - 1) % DISPATCH_SLOTS

    def load(tile, into):
        start = pl.multiple_of(tile * tile_rows, tile_rows)
        return pltpu.make_async_copy(hn_hbm.at[pl.ds(start, tile_rows), :], hbuf.at[into], load_sems.at[into])

    def wait_scatters(of_slot):
        for _ in range(TOP_K):
            pltpu.make_async_copy(hbuf.at[of_slot], rows_ref.at[pl.ds(0, tile_rows), :], scat_sems.at[of_slot]).wait()

    def for_each_pad_run(act):
        def per_expert(e, c):
            off, length = pad_off_ref[e], pad_len_ref[e]
            for bit in range(EXPERT_BLOCK_ROWS.bit_length() - 1):
                rows = 1 << bit

                @pl.when(((length >> bit) & 1) == 1)
                def _():
                    start = pl.multiple_of((off + (length & (rows - 1))) * ROW_TILES, ROW_TILES)
                    act(pltpu.make_async_copy(hbuf.at[0, pl.ds(0, rows * ROW_TILES), :],
                                              rows_ref.at[pl.ds(start, rows * ROW_TILES), :], pad_sem))
            return c

        lax.fori_loop(0, N_EXPERTS, per_expert, 0)

    @pl.when(step == 0)
    def _():
        load(0, 0).start()

        @pl.when(n_steps > 1)
        def _():
            load(1, 1).start()

    load(step, slot).wait()

    @pl.when(step == 0)
    def _():
        for_each_pad_run(lambda cp: cp.start())

    src = hbuf.at[slot]
    scatter = lambda t, k: _row_copy(src, t, rows_ref, dest_ref[k, t], scat_sems.at[slot])
    _for_each_token_group(tm, lambda t0: _start_group_copies(scatter, t0))

    @pl.when(step == 0)
    def _():
        for_each_pad_run(lambda cp: cp.wait())

    @pl.when(step >= 1)
    def _():
        wait_scatters(prev_slot)

    @pl.when(step + 2 < n_steps)
    def _():
        load(step + 2, prev_slot).start()

    @pl.when(step == n_steps - 1)
    def _():
        wait_scatters(slot)


def _dispatch(dest, pad_off, pad_len, hn, n_rows):
    n_tok = hn.shape[0] // ROW_TILES
    tm = math.gcd(DISPATCH_TILE, n_tok)
    assert tm >= EXPERT_BLOCK_ROWS // 2, "pad runs are sourced from the first token tile"
    grid_spec = pltpu.PrefetchScalarGridSpec(
        num_scalar_prefetch=2,
        grid=(n_tok // tm,),
        in_specs=[pl.BlockSpec((TOP_K, tm), lambda i, po, pn: (0, i), memory_space=pltpu.SMEM),
                  pl.BlockSpec(memory_space=pl.ANY)],
        out_specs=pl.BlockSpec(memory_space=pl.ANY),
        scratch_shapes=[pltpu.VMEM((DISPATCH_SLOTS, tm * ROW_TILES, LANES), F32),
                        pltpu.SemaphoreType.DMA((DISPATCH_SLOTS,)), pltpu.SemaphoreType.DMA((DISPATCH_SLOTS,)),
                        pltpu.SemaphoreType.DMA(())],
    )
    return pl.pallas_call(
        _dispatch_body,
        grid_spec=grid_spec,
        out_shape=jax.ShapeDtypeStruct((n_rows * ROW_TILES, LANES), F32),
        compiler_params=pltpu.CompilerParams(dimension_semantics=("arbitrary",), disable_bounds_checks=True,
                                             vmem_limit_bytes=VMEM_LIMIT),
        name="dispatch",
    )(pad_off, pad_len, dest, hn)


def _expert_body(blk_expert_ref, n_used_ref, next_expert_ref, slot_ref, x_ref, wgu_hbm, bgu_ref, wd_hbm, bd_ref, y_ref,
                 wgu_buf, wd_buf, wgu_s, wd_s, sems):
    i = pl.program_id(0)
    live = i < n_used_ref[0]
    expert = blk_expert_ref[i]
    new_expert = (i == 0) | (expert != blk_expert_ref[jnp.maximum(i - 1, 0)])

    def weight_copies(e, slot):
        return (pltpu.make_async_copy(wgu_hbm.at[e], wgu_buf.at[slot], sems.at[0, slot]),
                pltpu.make_async_copy(wd_hbm.at[e], wd_buf.at[slot], sems.at[1, slot]))

    @pl.when(live & new_expert)
    def _():
        slot = slot_ref[expert]
        upcoming = next_expert_ref[expert]

        @pl.when(i == 0)
        def _():
            for cp in weight_copies(expert, slot):
                cp.start()

        @pl.when(upcoming >= 0)
        def _():
            for cp in weight_copies(upcoming, 1 - slot):
                cp.start()

        for cp in weight_copies(expert, slot):
            cp.wait()
        for r in range(0, D_MODEL, WEIGHT_CAST_ROWS):
            wgu_s[r:r + WEIGHT_CAST_ROWS, :] = wgu_buf[slot, r:r + WEIGHT_CAST_ROWS, :].astype(BF16)
        for r in range(0, D_EXPERT, WEIGHT_CAST_ROWS):
            wd_s[r:r + WEIGHT_CAST_ROWS, :] = wd_buf[slot, r:r + WEIGHT_CAST_ROWS, :].astype(BF16)

    @pl.when(live)
    def _():
        gu = jnp.dot(_load_row_tiles(x_ref).astype(BF16), wgu_s[...], preferred_element_type=F32) + bgu_ref[...]
        x_glu = jnp.minimum(gu[:, :D_EXPERT], SWIGLU_LIMIT)
        x_lin = jnp.clip(gu[:, D_EXPERT:], -SWIGLU_LIMIT, SWIGLU_LIMIT)
        act = x_glu * jax.nn.sigmoid(SWIGLU_ALPHA * x_glu) * (x_lin + 1.0)
        _store_row_tiles(y_ref, jnp.dot(act.astype(BF16), wd_s[...], preferred_element_type=F32) + bd_ref[...])

    @pl.when(jnp.logical_not(live))
    def _():
        y_ref[...] = jnp.zeros_like(y_ref)


def _experts(blk_expert, n_used, next_expert, slot_of, x_rows, w_gu, b_gu, w_down, b_down):
    n_rows = x_rows.shape[0] // ROW_TILES
    rb = EXPERT_BLOCK_ROWS
    grid_spec = pltpu.PrefetchScalarGridSpec(
        num_scalar_prefetch=4,
        grid=(n_rows // rb,),
        in_specs=[
            pl.BlockSpec((rb * ROW_TILES, LANES), lambda i, be, nu, nx, sl: (jnp.minimum(i, nu[0] - 1), 0)),
            pl.BlockSpec(memory_space=pl.ANY),
            pl.BlockSpec((None, 1, 2 * D_EXPERT), lambda i, be, nu, nx, sl: (be[i], 0, 0)),
            pl.BlockSpec(memory_space=pl.ANY),
            pl.BlockSpec((None, 1, D_MODEL), lambda i, be, nu, nx, sl: (be[i], 0, 0)),
        ],
        out_specs=pl.BlockSpec((rb * ROW_TILES, LANES), lambda i, be, nu, nx, sl: (i, 0)),
        scratch_shapes=[pltpu.VMEM((2, D_MODEL, 2 * D_EXPERT), F32), pltpu.VMEM((2, D_EXPERT, D_MODEL), F32),
                        pltpu.VMEM((D_MODEL, 2 * D_EXPERT), BF16), pltpu.VMEM((D_EXPERT, D_MODEL), BF16),
                        pltpu.SemaphoreType.DMA((2, 2))],
    )
    return pl.pallas_call(
        _expert_body,
        grid_spec=grid_spec,
        out_shape=jax.ShapeDtypeStruct((n_rows * ROW_TILES, LANES), F32),
        compiler_params=pltpu.CompilerParams(dimension_semantics=("arbitrary",), vmem_limit_bytes=VMEM_LIMIT),
        name="experts",
    )(blk_expert, n_used, next_expert, slot_of, x_rows, w_gu, b_gu.reshape(N_EXPERTS, 1, -1), w_down,
      b_down.reshape(N_EXPERTS, 1, -1))


def _combine_body(dest_ref, dest_next_ref, gate_ref, x1_ref, y_ref, gfin_ref, out_ref, ybuf, sems):
    tm = x1_ref.shape[0]
    step = pl.program_id(0)
    slot = step % 2

    def gather(d_ref, into):
        fetch = lambda t, k: _row_copy(y_ref, d_ref[k, t], ybuf.at[into, k], t, sems.at[into])
        _for_each_token_group(tm, lambda t0: _start_group_copies(fetch, t0))

    @pl.when(step == 0)
    def _():
        gather(dest_ref, 0)

    @pl.when(step + 1 < pl.num_programs(0))
    def _():
        gather(dest_next_ref, 1 - slot)

    for k in range(TOP_K):
        pltpu.make_async_copy(y_ref.at[pl.ds(0, tm * ROW_TILES), :], ybuf.at[slot, k], sems.at[slot]).wait()
    gates = gate_ref[...]
    x2 = x1_ref[...]
    for k in range(TOP_K):
        x2 = x2 + gates[:, k:k + 1] * _load_row_tiles(ybuf.at[slot, k])
    out_ref[...] = x2 * lax.rsqrt(jnp.mean(x2 * x2, axis=-1, keepdims=True) + EPS) * gfin_ref[...]


def _combine(dest, gates_tok, x1, y_rows, norm_final_g):
    n_tok = x1.shape[0]
    tm = GATHER_TILE
    row = pl.BlockSpec((tm, D_MODEL), lambda i: (i, 0))
    n_tiles = n_tok // tm
    return pl.pallas_call(
        _combine_body,
        grid=(n_tiles,),
        in_specs=[pl.BlockSpec((TOP_K, tm), lambda i: (0, i), memory_space=pltpu.SMEM),
                  pl.BlockSpec((TOP_K, tm), lambda i: (0, jnp.minimum(i + 1, n_tiles - 1)), memory_space=pltpu.SMEM),
                  pl.BlockSpec((tm, TOP_K), lambda i: (i, 0)),
                  row,
                  pl.BlockSpec(memory_space=pl.ANY),
                  pl.BlockSpec((1, D_MODEL), lambda i: (0, 0))],
        out_specs=row,
        out_shape=jax.ShapeDtypeStruct((n_tok, D_MODEL), F32),
        scratch_shapes=[pltpu.VMEM((2, TOP_K, tm * ROW_TILES, LANES), F32), pltpu.SemaphoreType.DMA((2,))],
        compiler_params=pltpu.CompilerParams(dimension_semantics=("arbitrary",), vmem_limit_bytes=VMEM_LIMIT,
                                             disable_bounds_checks=True),
        name="combine",
    )(dest, dest, gates_tok, x1, y_rows, norm_final_g.reshape(1, D_MODEL))


def _routing_plan(top_idx, rank, counts, n_tok):
    padded = ((counts + EXPERT_BLOCK_ROWS - 1) // EXPERT_BLOCK_ROWS) * EXPERT_BLOCK_ROWS
    pend = jnp.cumsum(padded)
    pstart = pend - padded
    e_id = jnp.arange(N_EXPERTS, dtype=jnp.int32)[:, None, None]
    dest = jnp.sum(jnp.where(top_idx[None] == e_id, pstart[:, None, None], 0), axis=0) + rank
    n_rows = n_tok * TOP_K + N_EXPERTS * EXPERT_BLOCK_ROWS
    blk_start = jnp.arange(n_rows // EXPERT_BLOCK_ROWS, dtype=jnp.int32) * EXPERT_BLOCK_ROWS
    blk_expert = jnp.minimum(jnp.sum(blk_start[:, None] >= pend[None, :], axis=1), N_EXPERTS - 1)
    n_used = (pend[-1] // EXPERT_BLOCK_ROWS).reshape(1)
    pad_off, pad_len = pstart + counts, padded - counts
    used = counts > 0
    ids = jnp.arange(N_EXPERTS, dtype=jnp.int32)
    later = jnp.where(used[None, :] & (ids[None, :] > ids[:, None]), ids[None, :], N_EXPERTS)
    next_expert = jnp.min(later, axis=1)
    next_expert = jnp.where(next_expert == N_EXPERTS, -1, next_expert)
    slot_of = (jnp.cumsum(used) - used) % 2
    return (dest.astype(jnp.int32), blk_expert.astype(jnp.int32), n_used.astype(jnp.int32),
            next_expert.astype(jnp.int32), slot_of.astype(jnp.int32),
            pad_off.astype(jnp.int32), pad_len.astype(jnp.int32), n_rows)


def kernel(x, norm_mix_g, w_in, conv_w, conv_b, lru_wx, lru_bx, lru_wa, lru_ba, lru_a_param, w_proj_rnn, w_proj_attn, w_out, norm_ffn_g, router_w, router_b, expert_w_gu, expert_b_gu, expert_w_down, expert_b_down, norm_final_g):
    bsz, seq, _ = x.shape
    n_tok = bsz * seq
    assert w_in.shape[0] == 1, "single-layer problem: the final RMSNorm is fused into the combine stage"
    assert seq % ROW_TILE == 0 and seq % MOBA_BLOCK == 0
    layer = 0
    x2 = x.reshape(n_tok, D_MODEL)
    hg, q, k, v, sr, sa = _mixer_in(x2, norm_mix_g[layer], w_in[layer], conv_w[layer], conv_b[layer],
                                    lru_wx[layer], lru_bx[layer], lru_wa[layer], lru_ba[layer],
                                    lru_a_param[layer], bsz, seq)
    attn = _moba(q, k, v)
    x1, hn, top_idx, gates, rank, cnt = _merge(x2, hg, attn, sr, sa, w_proj_rnn[layer], w_proj_attn[layer],
                                               w_out[layer], norm_ffn_g[layer], router_w[layer],
                                               router_b[layer], seq)
    dest, blk_expert, n_used, next_expert, slot_of, pad_off, pad_len, n_rows = _routing_plan(
        top_idx, rank, cnt[:, 0], n_tok)
    x_rows = _dispatch(dest, pad_off, pad_len, hn, n_rows)
    y_rows = _experts(blk_expert, n_used, next_expert, slot_of, x_rows, expert_w_gu[layer], expert_b_gu[layer],
                      expert_w_down[layer], expert_b_down[layer])
    out = _combine(dest, gates.T, x1, y_rows, norm_final_g)
    return out.reshape(bsz, seq, D_MODEL)
```

```python
import functools
import math

import jax
import jax.numpy as jnp
from jax import lax
from jax.experimental import pallas as pl
from jax.experimental.pallas import tpu as pltpu

F32 = jnp.float32
BF16 = jnp.bfloat16

D_MODEL = 1024
RNN_WIDTH = 1024
RNN_BLOCKS = 16
RNN_BLOCK_DIM = RNN_WIDTH // RNN_BLOCKS
CONV_WIDTH = 4
LRU_C = 8.0
N_HEADS = 8
HEAD_DIM = 128
ROPE_DIM = HEAD_DIM // 4
ROPE_THETA = 500000.0
MOBA_BLOCK = 256
MOBA_TOPK = 3
N_EXPERTS = 32
TOP_K = 4
D_EXPERT = 1024
SWIGLU_LIMIT = 7.0
SWIGLU_ALPHA = 1.702
EPS = 1e-6
NEG = -1e30

V7X_VMEM_BYTES = 64 * 1024 * 1024
V7X_MXU_DIM = 256
SUBLANES = 8
LANES = 128

ROW_TILE = 512
EXPERT_BLOCK_ROWS = 512
WEIGHT_CAST_ROWS = 128
MOBA_HEADS_PER_STEP = 2
DISPATCH_TILE = 1024
DISPATCH_SLOTS = 3
GATHER_TILE = 256
ISSUE_UNROLL = 8
DMA_QUEUES = 2
LRU_GROUP = V7X_MXU_DIM // RNN_BLOCK_DIM
VMEM_LIMIT = V7X_VMEM_BYTES - 8 * 1024 * 1024


def _nt_dot(a, b, **kw):
    return lax.dot_general(a, b, (((1,), (1,)), ((), ())), preferred_element_type=F32, **kw)


ROW_TILES = D_MODEL // LANES
assert ROW_TILES == SUBLANES


def _store_row_tiles(ref, val):
    rows = val.shape[0]
    for s in range(ROW_TILES):
        ref[pl.ds(s, rows, stride=ROW_TILES), :] = val[:, s * LANES:(s + 1) * LANES]


def _load_row_tiles(ref, first_row=0, rows=None):
    rows = ref.shape[0] // ROW_TILES if rows is None else rows
    base = first_row * ROW_TILES
    return jnp.concatenate([ref[pl.ds(base + s, rows, stride=ROW_TILES), :] for s in range(ROW_TILES)], axis=1)


def _row_slab(ref, row):
    return ref.at[pl.ds(pl.multiple_of(row * ROW_TILES, ROW_TILES), ROW_TILES), :]


def _gelu_tanh(x):
    return 0.5 * x * (1.0 + jnp.tanh(math.sqrt(2.0 / math.pi) * (x + 0.044715 * (x * x * x))))


def _rotary_tables(seq):
    half = ROPE_DIM // 2
    inv_freq = ROPE_THETA ** (-jnp.arange(half, dtype=F32) * (2.0 / ROPE_DIM))
    ang = jnp.arange(seq, dtype=F32)[:, None] * inv_freq[None, :]
    cos, sin = jnp.cos(ang), jnp.sin(ang)
    pad = HEAD_DIM - ROPE_DIM
    cos_t = jnp.concatenate([cos, cos, jnp.ones((seq, pad), F32)], axis=1)
    sin_hi = jnp.concatenate([-sin, jnp.zeros((seq, half + pad), F32)], axis=1)
    sin_lo = jnp.concatenate([jnp.zeros((seq, half), F32), sin, jnp.zeros((seq, pad), F32)], axis=1)
    return jnp.stack([cos_t, sin_hi, sin_lo])


def _block_diag_gates(wx, wa):
    n_grp = RNN_BLOCKS // LRU_GROUP

    def bd(w):
        w = w.reshape(n_grp, LRU_GROUP, RNN_BLOCK_DIM, RNN_BLOCK_DIM)
        eye = jnp.eye(LRU_GROUP, dtype=w.dtype)
        full = jnp.einsum('gaij,ab->gaibj', w, eye)
        return full.reshape(n_grp, V7X_MXU_DIM, V7X_MXU_DIM)

    return jnp.concatenate([bd(wx), bd(wa)], axis=2).astype(BF16)


def _mixer_in_body(x_ref, g_ref, w_ref, rot_ref, cw_ref, cb_ref, wbd_ref, bx_ref, ba_ref, ap_ref,
                   hg_ref, q_ref, k_ref, v_ref, sr_ref, sa_ref,
                   xe_s, alast_s, ulast_s, hin_s, h_s, hprev_s):
    s = pl.program_id(1)
    tm = x_ref.shape[0]
    n_halo = xe_s.shape[1] - tm
    n_slab = xe_s.shape[0]
    n_grp = tm // SUBLANES
    x = x_ref[...]
    ms = jnp.mean(x * x, axis=-1, keepdims=True)
    xn = (x * lax.rsqrt(ms + EPS) * g_ref[...]).astype(BF16)

    def proj(j):
        return jnp.dot(xn, w_ref[:, j * D_MODEL:(j + 1) * D_MODEL], preferred_element_type=F32)

    cos, sin_hi, sin_lo = rot_ref[0], rot_ref[1], rot_ref[2]
    half = ROPE_DIM // 2

    def rotary_heads(z, out_ref, scale):
        for h in range(N_HEADS):
            zh = z[:, h * HEAD_DIM:(h + 1) * HEAD_DIM]
            r = zh * cos + pltpu.roll(zh, HEAD_DIM - half, axis=1) * sin_hi + pltpu.roll(zh, half, axis=1) * sin_lo
            out_ref[h] = (r * scale).astype(BF16)

    rx = proj(0)
    for c in range(n_slab):
        tail = xe_s[c, tm:tm + n_halo, :]
        xe_s[c, 0:n_halo, :] = jnp.where(s == 0, 0.0, tail)
        xe_s[c, n_halo:, :] = rx[:, c * LANES:(c + 1) * LANES]
    gate_gelu = _gelu_tanh(proj(1))

    def recurrent_group(g):
        cols = slice(g * V7X_MXU_DIM, (g + 1) * V7X_MXU_DIM)
        slabs = range(g * V7X_MXU_DIM // LANES, (g + 1) * V7X_MXU_DIM // LANES)

        def phase(d):
            return jnp.concatenate([xe_s[c, pl.ds(n_halo + d, n_grp, stride=SUBLANES), :] for c in slabs], axis=1)

        shifted = {d: phase(d) for d in range(-(CONV_WIDTH - 1), SUBLANES)}
        blocks = []
        for j in range(SUBLANES):
            blk = cb_ref[:, cols]
            for tap in range(CONV_WIDTH):
                blk = blk + cw_ref[tap:tap + 1, cols] * shifted[j - (CONV_WIDTH - 1) + tap]
            blocks.append(blk)
        xc = jnp.concatenate(blocks, axis=0)

        r = jnp.dot(xc.astype(BF16), wbd_ref[g], preferred_element_type=F32)
        gate_x = jax.nn.sigmoid(r[:, :V7X_MXU_DIM] + bx_ref[:, cols])
        gate_a = jax.nn.sigmoid(r[:, V7X_MXU_DIM:] + ba_ref[:, cols])
        z = -ap_ref[:, cols]
        softplus = jnp.maximum(z, 0.0) + jnp.log1p(jnp.exp(-jnp.abs(z)))
        log_a = -LRU_C * gate_a * softplus
        a = jnp.exp(log_a)
        u = jnp.sqrt(-jnp.tanh(log_a) * (a * a + 1.0)) * (gate_x * xc)

        cums = []
        a_cum = u_cum = None
        for j in range(SUBLANES):
            rows = slice(j * n_grp, (j + 1) * n_grp)
            a_j, u_j = a[rows, :], u[rows, :]
            a_cum, u_cum = (a_j, u_j) if j == 0 else (a_j * a_cum, a_j * u_cum + u_j)
            cums.append((a_cum, u_cum))
        alast_s[:, cols] = a_cum
        ulast_s[:, cols] = u_cum
        h = jnp.where(s == 0, 0.0, hprev_s[:, cols])
        for i in range(n_grp):
            hin_s[i:i + 1, cols] = h
            h = ulast_s[i:i + 1, cols] + alast_s[i:i + 1, cols] * h
        hprev_s[:, cols] = h
        h_in = hin_s[0:n_grp, cols]
        for j, (a_c, u_c) in enumerate(cums):
            h_j = u_c + a_c * h_in
            for k, c in enumerate(slabs):
                h_s[c, pl.ds(j, n_grp, stride=SUBLANES), :] = h_j[:, k * LANES:(k + 1) * LANES]
        h_grp = jnp.concatenate([h_s[c] for c in slabs], axis=1)
        hg_ref[:, cols] = (h_grp * gate_gelu[:, cols]).astype(BF16)

    recurrent_group(0)
    rotary_heads(proj(2), q_ref, 1.0 / math.sqrt(HEAD_DIM))
    recurrent_group(1)
    rotary_heads(proj(3), k_ref, 1.0)
    recurrent_group(2)
    sr_ref[...] = jax.nn.sigmoid(proj(5)).astype(BF16)
    sa_ref[...] = jax.nn.sigmoid(proj(6)).astype(BF16)
    recurrent_group(3)
    zv = proj(4)
    for h in range(N_HEADS):
        v_ref[h] = zv[:, h * HEAD_DIM:(h + 1) * HEAD_DIM].astype(BF16)


def _mixer_in(x2, norm_g, w_in, conv_w, conv_b, wx, bx, wa, ba, a_param, bsz, seq):
    n_tok = x2.shape[0]
    tm = ROW_TILE
    tiles_per_seq = seq // tm
    in_cols = w_in.shape[1]
    halo_rows = SUBLANES
    const2 = lambda b, s: (0, 0)
    row = pl.BlockSpec((tm, D_MODEL), lambda b, s: (b * tiles_per_seq + s, 0))
    head = pl.BlockSpec((None, N_HEADS, tm, HEAD_DIM), lambda b, s: (b, 0, s, 0))
    vec = pl.BlockSpec((1, RNN_WIDTH), const2)
    n_grp = RNN_WIDTH // V7X_MXU_DIM
    tok_bf16 = jax.ShapeDtypeStruct((n_tok, D_MODEL), BF16)
    head_bf16 = jax.ShapeDtypeStruct((bsz, N_HEADS, seq, HEAD_DIM), BF16)
    return pl.pallas_call(
        _mixer_in_body,
        grid=(bsz, tiles_per_seq),
        in_specs=[
            row,
            pl.BlockSpec((1, D_MODEL), const2),
            pl.BlockSpec((D_MODEL, in_cols), const2, pipeline_mode=pl.Buffered(1)),
            pl.BlockSpec((3, tm, HEAD_DIM), lambda b, s: (0, s, 0)),
            pl.BlockSpec((CONV_WIDTH, RNN_WIDTH), const2), vec,
            pl.BlockSpec((n_grp, V7X_MXU_DIM, 2 * V7X_MXU_DIM), lambda b, s: (0, 0, 0)),
            vec, vec, vec,
        ],
        out_specs=[row, head, head, head, row, row],
        out_shape=[tok_bf16, head_bf16, head_bf16, head_bf16, tok_bf16, tok_bf16],
        scratch_shapes=[pltpu.VMEM((RNN_WIDTH // LANES, halo_rows + tm, LANES), F32),
                        pltpu.VMEM((tm // SUBLANES, RNN_WIDTH), F32), pltpu.VMEM((tm // SUBLANES, RNN_WIDTH), F32),
                        pltpu.VMEM((tm // SUBLANES, RNN_WIDTH), F32),
                        pltpu.VMEM((RNN_WIDTH // LANES, tm, LANES), F32), pltpu.VMEM((1, RNN_WIDTH), F32)],
        compiler_params=pltpu.CompilerParams(dimension_semantics=("parallel", "arbitrary"),
                                             vmem_limit_bytes=VMEM_LIMIT),
        name="mixer_in",
    )(x2, norm_g.reshape(1, D_MODEL), w_in.astype(BF16), _rotary_tables(seq), conv_w, conv_b.reshape(1, -1),
      _block_diag_gates(wx, wa), bx.reshape(1, -1), ba.reshape(1, -1), a_param.reshape(1, -1))


def _moba_body(q_all, k_all, v_all, o_all, kaug_all, qaug_all):
    n_heads_here, seq = q_all.shape[0], q_all.shape[1]
    nb = seq // MOBA_BLOCK
    nb_pad = -(-nb // SUBLANES) * SUBLANES
    key_blk = lax.broadcasted_iota(jnp.int32, (seq, LANES), 0) // MOBA_BLOCK
    lane = lax.broadcasted_iota(jnp.int32, (seq, LANES), 1)
    r_id = lax.broadcasted_iota(jnp.int32, (MOBA_BLOCK, MOBA_BLOCK), 0)
    c_id = lax.broadcasted_iota(jnp.int32, (MOBA_BLOCK, MOBA_BLOCK), 1)
    causal = c_id <= r_id
    eye = (c_id == r_id).astype(BF16)
    j_id = lax.broadcasted_iota(jnp.int32, (nb_pad, seq), 0)
    own = lax.broadcasted_iota(jnp.int32, (1, seq), 1) // MOBA_BLOCK

    def prepare(q_ref, k_ref, kaug_s, qaug_s):
        kmean = jnp.mean(k_ref[...].astype(F32).reshape(nb, MOBA_BLOCK, HEAD_DIM), axis=1)
        if nb_pad > nb:
            kmean = jnp.concatenate([kmean, jnp.zeros((nb_pad - nb, HEAD_DIM), F32)], axis=0)
        kaug_s[:, :HEAD_DIM] = k_ref[...]
        kaug_s[:, HEAD_DIM:] = (key_blk == lane).astype(BF16)
        gate = _nt_dot(kmean, q_ref[...].astype(F32), precision=lax.Precision.HIGHEST)
        rank = jnp.zeros((nb_pad, seq), F32)
        for jp in range(nb - 1):
            row = gate[jp:jp + 1, :]
            beats = ((row > gate) | ((row == gate) & (jp < j_id))) & (jp < own)
            rank = rank + beats.astype(F32)
        allowed = (j_id >= own) | (rank < float(MOBA_TOPK))
        bias_t = jnp.where(allowed, 0.0, NEG)
        bias_t = jnp.concatenate([bias_t, jnp.zeros((LANES - nb_pad, seq), F32)], axis=0).astype(BF16)
        qaug_s[:, :HEAD_DIM] = q_ref[...]
        for n in range(nb):
            cols = slice(n * MOBA_BLOCK, (n + 1) * MOBA_BLOCK)
            qaug_s[cols, HEAD_DIM:] = _nt_dot(eye, bias_t[:, cols]).astype(BF16)

    def masked_scores(h, n):
        s = _nt_dot(qaug_all[h, n * MOBA_BLOCK:(n + 1) * MOBA_BLOCK, :], kaug_all[h, 0:(n + 1) * MOBA_BLOCK, :])
        parts = [s[:, j * MOBA_BLOCK:(j + 1) * MOBA_BLOCK] for j in range(n + 1)]
        parts[n] = jnp.where(causal, parts[n], NEG)
        return parts

    def attend(h, n, parts):
        m = functools.reduce(jnp.maximum, parts)
        m = jnp.max(m, axis=1, keepdims=True)
        probs = [jnp.exp(part - m) for part in parts]
        l = jnp.sum(functools.reduce(jnp.add, probs), axis=1, keepdims=True)
        p_all = jnp.concatenate([p.astype(BF16) for p in probs], axis=1)
        acc = jnp.dot(p_all, v_all[h, 0:(n + 1) * MOBA_BLOCK, :], preferred_element_type=F32)
        o_all[h, n * MOBA_BLOCK:(n + 1) * MOBA_BLOCK, :] = (acc * (1.0 / l)).astype(BF16)

    for h in range(n_heads_here):
        prepare(q_all.at[h], k_all.at[h], kaug_all.at[h], qaug_all.at[h])
    ahead = [masked_scores(h, 0) for h in range(n_heads_here)]
    for n in range(nb):
        for h in range(n_heads_here):
            parts = ahead[h]
            if n + 1 < nb:
                ahead[h] = masked_scores(h, n + 1)
            attend(h, n, parts)


def _moba(q, k, v):
    bsz, n_heads, seq, hd = q.shape
    g = MOBA_HEADS_PER_STEP
    spec = pl.BlockSpec((None, g, seq, hd), lambda b, h: (b, h, 0, 0))
    return pl.pallas_call(
        _moba_body,
        grid=(bsz, n_heads // g),
        in_specs=[spec, spec, spec],
        out_specs=spec,
        out_shape=jax.ShapeDtypeStruct(q.shape, BF16),
        scratch_shapes=[pltpu.VMEM((g, seq, 2 * hd), BF16), pltpu.VMEM((g, seq, 2 * hd), BF16)],
        compiler_params=pltpu.CompilerParams(dimension_semantics=("parallel", "parallel"),
                                             vmem_limit_bytes=VMEM_LIMIT),
        name="moba",
    )(q, k, v)


def _merge_body(x_ref, hg_ref, at_ref, sr_ref, sa_ref, wpr_ref, wpa_ref, wo_ref, gffn_ref, rwt_ref, rb_ref,
                x1_ref, hn_ref, idx_ref, gate_ref, rank_ref, cnt_ref, carry_s):
    tm = x_ref.shape[0]

    @pl.when(pl.program_id(0) == 0)
    def _():
        carry_s[...] = jnp.zeros_like(carry_s)

    y_rnn = jnp.dot(hg_ref[...], wpr_ref[...], preferred_element_type=F32)
    attn = jnp.concatenate([at_ref[h] for h in range(N_HEADS)], axis=1)
    y_attn = jnp.dot(attn, wpa_ref[...], preferred_element_type=F32)
    mixed = sr_ref[...].astype(F32) * y_rnn + sa_ref[...].astype(F32) * y_attn
    x1 = x_ref[...] + jnp.dot(mixed.astype(BF16), wo_ref[...], preferred_element_type=F32)
    x1_ref[...] = x1
    hn = x1 * lax.rsqrt(jnp.mean(x1 * x1, axis=-1, keepdims=True) + EPS) * gffn_ref[...]
    _store_row_tiles(hn_ref, hn)

    logits = _nt_dot(rwt_ref[...], hn.astype(BF16)) + rb_ref[...]
    e_id = lax.broadcasted_iota(jnp.int32, (N_EXPERTS, tm), 0)
    vals = logits
    onehots, top_vals, top_ids = [], [], []
    for _ in range(TOP_K):
        best = jnp.max(vals, axis=0, keepdims=True)
        best_id = jnp.min(jnp.where(vals == best, e_id, N_EXPERTS), axis=0, keepdims=True)
        hit = e_id == best_id
        vals = jnp.where(hit, -jnp.inf, vals)
        onehots.append(hit)
        top_vals.append(best)
        top_ids.append(best_id)
    exps = [jnp.exp(v - top_vals[0]) for v in top_vals]
    denom = exps[0] + exps[1] + exps[2] + exps[3]
    idx_ref[...] = jnp.concatenate(top_ids, axis=0)
    gate_ref[...] = jnp.concatenate([e / denom for e in exps], axis=0)

    chosen = onehots[0] | onehots[1] | onehots[2] | onehots[3]
    t_row = lax.broadcasted_iota(jnp.int32, (tm, tm), 0)
    t_col = lax.broadcasted_iota(jnp.int32, (tm, tm), 1)
    before = (t_row < t_col).astype(BF16)
    chosen_f = chosen.astype(F32)
    prior = jnp.dot(chosen_f.astype(BF16), before, preferred_element_type=F32) + carry_s[:, 0:1]
    ranks = [jnp.sum(jnp.where(hit, prior, 0.0), axis=0, keepdims=True) for hit in onehots]
    rank_ref[...] = jnp.concatenate(ranks, axis=0).astype(jnp.int32)
    carry_s[...] = carry_s[...] + jnp.sum(chosen_f, axis=1, keepdims=True)
    cnt_ref[...] = carry_s[...].astype(jnp.int32)


def _merge(x2, hg, attn, sr, sa, w_proj_rnn, w_proj_attn, w_out, norm_ffn_g, router_w, router_b, seq):
    n_tok = x2.shape[0]
    tm = ROW_TILE
    tiles_per_seq = seq // tm
    row = pl.BlockSpec((tm, D_MODEL), lambda i: (i, 0))
    head = pl.BlockSpec((None, N_HEADS, tm, HEAD_DIM), lambda i: (i // tiles_per_seq, 0, i % tiles_per_seq, 0))
    mat = pl.BlockSpec((D_MODEL, D_MODEL), lambda i: (0, 0))
    topk = pl.BlockSpec((TOP_K, tm), lambda i: (0, i))
    return pl.pallas_call(
        _merge_body,
        grid=(n_tok // tm,),
        in_specs=[row, row, head, row, row, mat, mat, mat,
                  pl.BlockSpec((1, D_MODEL), lambda i: (0, 0)),
                  pl.BlockSpec((N_EXPERTS, D_MODEL), lambda i: (0, 0)),
                  pl.BlockSpec((N_EXPERTS, 1), lambda i: (0, 0))],
        out_specs=[row, pl.BlockSpec((tm * ROW_TILES, LANES), lambda i: (i, 0)), topk, topk, topk,
                   pl.BlockSpec((N_EXPERTS, LANES), lambda i: (0, 0))],
        out_shape=[jax.ShapeDtypeStruct((n_tok, D_MODEL), F32),
                   jax.ShapeDtypeStruct((n_tok * ROW_TILES, LANES), F32),
                   jax.ShapeDtypeStruct((TOP_K, n_tok), jnp.int32), jax.ShapeDtypeStruct((TOP_K, n_tok), F32),
                   jax.ShapeDtypeStruct((TOP_K, n_tok), jnp.int32),
                   jax.ShapeDtypeStruct((N_EXPERTS, LANES), jnp.int32)],
        scratch_shapes=[pltpu.VMEM((N_EXPERTS, LANES), F32)],
        compiler_params=pltpu.CompilerParams(dimension_semantics=("arbitrary",), vmem_limit_bytes=VMEM_LIMIT),
        name="merge_route",
    )(x2, hg, attn, sr, sa, w_proj_rnn.astype(BF16), w_proj_attn.astype(BF16), w_out.astype(BF16),
      norm_ffn_g.reshape(1, D_MODEL), router_w.T.astype(BF16), router_b.reshape(N_EXPERTS, 1))


def _row_copy(src_ref, src_row, dst_ref, dst_row, sem):
    return pltpu.make_async_copy(_row_slab(src_ref, src_row), _row_slab(dst_ref, dst_row), sem)


def _tile_flat(dest, tm):
    return dest.reshape(TOP_K, -1, tm).transpose(1, 0, 2).reshape(-1)


def _for_each_token_group(tm, body):
    def group(g, c):
        body(pl.multiple_of(g * ISSUE_UNROLL, ISSUE_UNROLL))
        return c

    lax.fori_loop(0, tm // ISSUE_UNROLL, group, 0)


def _start_group_copies(copy_of, t0):
    for i in range(ISSUE_UNROLL):
        for k in range(TOP_K):
            copy_of(t0 + i, k).start(priority=(i * TOP_K + k) % DMA_QUEUES)


def _dispatch_body(pad_off_ref, pad_len_ref, dest_ref, hn_hbm, rows_ref, hbuf, load_sems, scat_sems, pad_sem):
    tm = hbuf.shape[1] // ROW_TILES
    tile_rows = tm * ROW_TILES
    step = pl.program_id(0)
    n_steps = pl.num_programs(0)
    slot = step % DISPATCH_SLOTS
    prev_slot = (step + DISPATCH_SLOTS - 1) % DISPATCH_SLOTS

    def load(tile, into):
        start = pl.multiple_of(tile * tile_rows, tile_rows)
        return pltpu.make_async_copy(hn_hbm.at[pl.ds(start, tile_rows), :], hbuf.at[into], load_sems.at[into])

    def wait_scatters(of_slot):
        for _ in range(TOP_K):
            pltpu.make_async_copy(hbuf.at[of_slot], rows_ref.at[pl.ds(0, tile_rows), :], scat_sems.at[of_slot]).wait()

    def for_each_pad_run(act):
        def per_expert(e, c):
            off, length = pad_off_ref[e], pad_len_ref[e]
            for bit in range(EXPERT_BLOCK_ROWS.bit_length() - 1):
                rows = 1 << bit

                @pl.when(((length >> bit) & 1) == 1)
                def _():
                    start = pl.multiple_of((off + (length & (rows - 1))) * ROW_TILES, ROW_TILES)
                    act(pltpu.make_async_copy(hbuf.at[0, pl.ds(0, rows * ROW_TILES), :],
                                              rows_ref.at[pl.ds(start, rows * ROW_TILES), :], pad_sem))
            return c

        lax.fori_loop(0, N_EXPERTS, per_expert, 0)

    @pl.when(step == 0)
    def _():
        load(0, 0).start()

        @pl.when(n_steps > 1)
        def _():
            load(1, 1).start()

    load(step, slot).wait()

    @pl.when(step == 0)
    def _():
        for_each_pad_run(lambda cp: cp.start())

    src = hbuf.at[slot]
    scatter = lambda t, k: _row_copy(src, t, rows_ref, dest_ref[k * tm + t], scat_sems.at[slot])
    _for_each_token_group(tm, lambda t0: _start_group_copies(scatter, t0))

    @pl.when(step == 0)
    def _():
        for_each_pad_run(lambda cp: cp.wait())

    @pl.when(step >= 1)
    def _():
        wait_scatters(prev_slot)

    @pl.when(step + 2 < n_steps)
    def _():
        load(step + 2, prev_slot).start()

    @pl.when(step == n_steps - 1)
    def _():
        wait_scatters(slot)


def _dispatch(dest, pad_off, pad_len, hn, n_rows):
    n_tok = hn.shape[0] // ROW_TILES
    tm = math.gcd(DISPATCH_TILE, n_tok)
    assert tm >= EXPERT_BLOCK_ROWS // 2, "pad runs are sourced from the first token tile"
    grid_spec = pltpu.PrefetchScalarGridSpec(
        num_scalar_prefetch=2,
        grid=(n_tok // tm,),
        in_specs=[pl.BlockSpec((TOP_K * tm,), lambda i, po, pn: (i,), memory_space=pltpu.SMEM),
                  pl.BlockSpec(memory_space=pl.ANY)],
        out_specs=pl.BlockSpec(memory_space=pl.ANY),
        scratch_shapes=[pltpu.VMEM((DISPATCH_SLOTS, tm * ROW_TILES, LANES), F32),
                        pltpu.SemaphoreType.DMA((DISPATCH_SLOTS,)), pltpu.SemaphoreType.DMA((DISPATCH_SLOTS,)),
                        pltpu.SemaphoreType.DMA(())],
    )
    return pl.pallas_call(
        _dispatch_body,
        grid_spec=grid_spec,
        out_shape=jax.ShapeDtypeStruct((n_rows * ROW_TILES, LANES), F32),
        compiler_params=pltpu.CompilerParams(dimension_semantics=("arbitrary",), disable_bounds_checks=True,
                                             vmem_limit_bytes=VMEM_LIMIT),
        name="dispatch",
    )(pad_off, pad_len, _tile_flat(dest, tm), hn)


def _expert_body(blk_expert_ref, n_used_ref, next_expert_ref, slot_ref, x_ref, wgu_hbm, bgu_ref, wd_hbm, bd_ref, y_ref,
                 wgu_buf, wd_buf, wgu_s, wd_s, sems):
    i = pl.program_id(0)
    live = i < n_used_ref[0]
    expert = blk_expert_ref[i]
    new_expert = (i == 0) | (expert != blk_expert_ref[jnp.maximum(i - 1, 0)])

    def weight_copies(e, slot):
        return (pltpu.make_async_copy(wgu_hbm.at[e], wgu_buf.at[slot], sems.at[0, slot]),
                pltpu.make_async_copy(wd_hbm.at[e], wd_buf.at[slot], sems.at[1, slot]))

    @pl.when(live & new_expert)
    def _():
        slot = slot_ref[expert]
        upcoming = next_expert_ref[expert]

        @pl.when(i == 0)
        def _():
            for cp in weight_copies(expert, slot):
                cp.start()

        @pl.when(upcoming >= 0)
        def _():
            for cp in weight_copies(upcoming, 1 - slot):
                cp.start()

        for cp in weight_copies(expert, slot):
            cp.wait()
        for r in range(0, D_MODEL, WEIGHT_CAST_ROWS):
            wgu_s[r:r + WEIGHT_CAST_ROWS, :] = wgu_buf[slot, r:r + WEIGHT_CAST_ROWS, :].astype(BF16)
        for r in range(0, D_EXPERT, WEIGHT_CAST_ROWS):
            wd_s[r:r + WEIGHT_CAST_ROWS, :] = wd_buf[slot, r:r + WEIGHT_CAST_ROWS, :].astype(BF16)

    @pl.when(live)
    def _():
        gu = jnp.dot(_load_row_tiles(x_ref).astype(BF16), wgu_s[...], preferred_element_type=F32) + bgu_ref[...]
        x_glu = jnp.minimum(gu[:, :D_EXPERT], SWIGLU_LIMIT)
        x_lin = jnp.clip(gu[:, D_EXPERT:], -SWIGLU_LIMIT, SWIGLU_LIMIT)
        act = x_glu * jax.nn.sigmoid(SWIGLU_ALPHA * x_glu) * (x_lin + 1.0)
        _store_row_tiles(y_ref, jnp.dot(act.astype(BF16), wd_s[...], preferred_element_type=F32) + bd_ref[...])

    @pl.when(jnp.logical_not(live))
    def _():
        y_ref[...] = jnp.zeros_like(y_ref)


def _experts(blk_expert, n_used, next_expert, slot_of, x_rows, w_gu, b_gu, w_down, b_down):
    n_rows = x_rows.shape[0] // ROW_TILES
    rb = EXPERT_BLOCK_ROWS
    grid_spec = pltpu.PrefetchScalarGridSpec(
        num_scalar_prefetch=4,
        grid=(n_rows // rb,),
        in_specs=[
            pl.BlockSpec((rb * ROW_TILES, LANES), lambda i, be, nu, nx, sl: (jnp.minimum(i, nu[0] - 1), 0)),
            pl.BlockSpec(memory_space=pl.ANY),
            pl.BlockSpec((None, 1, 2 * D_EXPERT), lambda i, be, nu, nx, sl: (be[i], 0, 0)),
            pl.BlockSpec(memory_space=pl.ANY),
            pl.BlockSpec((None, 1, D_MODEL), lambda i, be, nu, nx, sl: (be[i], 0, 0)),
        ],
        out_specs=pl.BlockSpec((rb * ROW_TILES, LANES), lambda i, be, nu, nx, sl: (i, 0)),
        scratch_shapes=[pltpu.VMEM((2, D_MODEL, 2 * D_EXPERT), F32), pltpu.VMEM((2, D_EXPERT, D_MODEL), F32),
                        pltpu.VMEM((D_MODEL, 2 * D_EXPERT), BF16), pltpu.VMEM((D_EXPERT, D_MODEL), BF16),
                        pltpu.SemaphoreType.DMA((2, 2))],
    )
    return pl.pallas_call(
        _expert_body,
        grid_spec=grid_spec,
        out_shape=jax.ShapeDtypeStruct((n_rows * ROW_TILES, LANES), F32),
        compiler_params=pltpu.CompilerParams(dimension_semantics=("arbitrary",), vmem_limit_bytes=VMEM_LIMIT),
        name="experts",
    )(blk_expert, n_used, next_expert, slot_of, x_rows, w_gu, b_gu.reshape(N_EXPERTS, 1, -1), w_down,
      b_down.reshape(N_EXPERTS, 1, -1))


def _combine_body(dest_ref, dest_next_ref, gate_ref, x1_ref, y_ref, gfin_ref, out_ref, ybuf, sems):
    tm = x1_ref.shape[0]
    step = pl.program_id(0)
    slot = step % 2

    def gather(d_ref, into):
        fetch = lambda t, k: _row_copy(y_ref, d_ref[k * tm + t], ybuf.at[into, k], t, sems.at[into])
        _for_each_token_group(tm, lambda t0: _start_group_copies(fetch, t0))

    @pl.when(step == 0)
    def _():
        gather(dest_ref, 0)

    @pl.when(step + 1 < pl.num_programs(0))
    def _():
        gather(dest_next_ref, 1 - slot)

    for k in range(TOP_K):
        pltpu.make_async_copy(y_ref.at[pl.ds(0, tm * ROW_TILES), :], ybuf.at[slot, k], sems.at[slot]).wait()
    gates = gate_ref[...]
    x2 = x1_ref[...]
    for k in range(TOP_K):
        x2 = x2 + gates[:, k:k + 1] * _load_row_tiles(ybuf.at[slot, k])
    out_ref[...] = x2 * lax.rsqrt(jnp.mean(x2 * x2, axis=-1, keepdims=True) + EPS) * gfin_ref[...]


def _combine(dest, gates_tok, x1, y_rows, norm_final_g):
    n_tok = x1.shape[0]
    tm = GATHER_TILE
    row = pl.BlockSpec((tm, D_MODEL), lambda i: (i, 0))
    n_tiles = n_tok // tm
    return pl.pallas_call(
        _combine_body,
        grid=(n_tiles,),
        in_specs=[pl.BlockSpec((TOP_K * tm,), lambda i: (i,), memory_space=pltpu.SMEM),
                  pl.BlockSpec((TOP_K * tm,), lambda i: (jnp.minimum(i + 1, n_tiles - 1),), memory_space=pltpu.SMEM),
                  pl.BlockSpec((tm, TOP_K), lambda i: (i, 0)),
                  row,
                  pl.BlockSpec(memory_space=pl.ANY),
                  pl.BlockSpec((1, D_MODEL), lambda i: (0, 0))],
        out_specs=row,
        out_shape=jax.ShapeDtypeStruct((n_tok, D_MODEL), F32),
        scratch_shapes=[pltpu.VMEM((2, TOP_K, tm * ROW_TILES, LANES), F32), pltpu.SemaphoreType.DMA((2,))],
        compiler_params=pltpu.CompilerParams(dimension_semantics=("arbitrary",), vmem_limit_bytes=VMEM_LIMIT,
                                             disable_bounds_checks=True),
        name="combine",
    )(_tile_flat(dest, tm), _tile_flat(dest, tm), gates_tok, x1, y_rows, norm_final_g.reshape(1, D_MODEL))


def _routing_plan(top_idx, rank, counts, n_tok):
    padded = ((counts + EXPERT_BLOCK_ROWS - 1) // EXPERT_BLOCK_ROWS) * EXPERT_BLOCK_ROWS
    pend = jnp.cumsum(padded)
    pstart = pend - padded
    e_id = jnp.arange(N_EXPERTS, dtype=jnp.int32)[:, None, None]
    dest = jnp.sum(jnp.where(top_idx[None] == e_id, pstart[:, None, None], 0), axis=0) + rank
    n_rows = n_tok * TOP_K + N_EXPERTS * EXPERT_BLOCK_ROWS
    blk_start = jnp.arange(n_rows // EXPERT_BLOCK_ROWS, dtype=jnp.int32) * EXPERT_BLOCK_ROWS
    blk_expert = jnp.minimum(jnp.sum(blk_start[:, None] >= pend[None, :], axis=1), N_EXPERTS - 1)
    n_used = (pend[-1] // EXPERT_BLOCK_ROWS).reshape(1)
    pad_off, pad_len = pstart + counts, padded - counts
    used = counts > 0
    ids = jnp.arange(N_EXPERTS, dtype=jnp.int32)
    later = jnp.where(used[None, :] & (ids[None, :] > ids[:, None]), ids[None, :], N_EXPERTS)
    next_expert = jnp.min(later, axis=1)
    next_expert = jnp.where(next_expert == N_EXPERTS, -1, next_expert)
    slot_of = (jnp.cumsum(used) - used) % 2
    return (dest.astype(jnp.int32), blk_expert.astype(jnp.int32), n_used.astype(jnp.int32),
            next_expert.astype(jnp.int32), slot_of.astype(jnp.int32),
            pad_off.astype(jnp.int32), pad_len.astype(jnp.int32), n_rows)


def kernel(x, norm_mix_g, w_in, conv_w, conv_b, lru_wx, lru_bx, lru_wa, lru_ba, lru_a_param, w_proj_rnn, w_proj_attn, w_out, norm_ffn_g, router_w, router_b, expert_w_gu, expert_b_gu, expert_w_down, expert_b_down, norm_final_g):
    bsz, seq, _ = x.shape
    n_tok = bsz * seq
    assert w_in.shape[0] == 1, "single-layer problem: the final RMSNorm is fused into the combine stage"
    assert seq % ROW_TILE == 0 and seq % MOBA_BLOCK == 0
    layer = 0
    x2 = x.reshape(n_tok, D_MODEL)
    hg, q, k, v, sr, sa = _mixer_in(x2, norm_mix_g[layer], w_in[layer], conv_w[layer], conv_b[layer],
                                    lru_wx[layer], lru_bx[layer], lru_wa[layer], lru_ba[layer],
                                    lru_a_param[layer], bsz, seq)
    attn = _moba(q, k, v)
    x1, hn, top_idx, gates, rank, cnt = _merge(x2, hg, attn, sr, sa, w_proj_rnn[layer], w_proj_attn[layer],
                                               w_out[layer], norm_ffn_g[layer], router_w[layer],
                                               router_b[layer], seq)
    dest, blk_expert, n_used, next_expert, slot_of, pad_off, pad_len, n_rows = _routing_plan(
        top_idx, rank, cnt[:, 0], n_tok)
    x_rows = _dispatch(dest, pad_off, pad_len, hn, n_rows)
    y_rows = _experts(blk_expert, n_used, next_expert, slot_of, x_rows, expert_w_gu[layer], expert_b_gu[layer],
                      expert_w_down[layer], expert_b_down[layer])
    out = _combine(dest, gates.T, x1, y_rows, norm_final_g)
    return out.reshape(bsz, seq, D_MODEL)
```

```python
import functools
import math

import jax
import jax.numpy as jnp
from jax import lax
from jax.experimental import pallas as pl
from jax.experimental.pallas import tpu as pltpu

F32 = jnp.float32
BF16 = jnp.bfloat16

D_MODEL = 1024
RNN_WIDTH = 1024
RNN_BLOCKS = 16
RNN_BLOCK_DIM = RNN_WIDTH // RNN_BLOCKS
CONV_WIDTH = 4
LRU_C = 8.0
N_HEADS = 8
HEAD_DIM = 128
ROPE_DIM = HEAD_DIM // 4
ROPE_THETA = 500000.0
MOBA_BLOCK = 256
MOBA_TOPK = 3
N_EXPERTS = 32
TOP_K = 4
D_EXPERT = 1024
SWIGLU_LIMIT = 7.0
SWIGLU_ALPHA = 1.702
EPS = 1e-6
NEG = -1e30

V7X_VMEM_BYTES = 64 * 1024 * 1024
V7X_MXU_DIM = 256
SUBLANES = 8
LANES = 128

ROW_TILE = 512
EXPERT_BLOCK_ROWS = 512
WEIGHT_CAST_ROWS = 128
MOBA_HEADS_PER_STEP = 2
DISPATCH_TILE = 1024
DISPATCH_SLOTS = 3
GATHER_TILE = 256
ISSUE_UNROLL = 8
DMA_QUEUES = 2
LRU_GROUP = V7X_MXU_DIM // RNN_BLOCK_DIM
VMEM_LIMIT = V7X_VMEM_BYTES - 8 * 1024 * 1024


def _nt_dot(a, b, **kw):
    return lax.dot_general(a, b, (((1,), (1,)), ((), ())), preferred_element_type=F32, **kw)


ROW_TILES = D_MODEL // LANES
assert ROW_TILES == SUBLANES


def _store_row_tiles(ref, val):
    rows = val.shape[0]
    for s in range(ROW_TILES):
        ref[pl.ds(s, rows, stride=ROW_TILES), :] = val[:, s * LANES:(s + 1) * LANES]


def _load_row_tiles(ref):
    rows = ref.shape[0] // ROW_TILES
    return jnp.concatenate([ref[pl.ds(s, rows, stride=ROW_TILES), :] for s in range(ROW_TILES)], axis=1)


def _row_slab(ref, row):
    return ref.at[pl.ds(pl.multiple_of(row * ROW_TILES, ROW_TILES), ROW_TILES), :]


def _gelu_tanh(x):
    return 0.5 * x * (1.0 + jnp.tanh(math.sqrt(2.0 / math.pi) * (x + 0.044715 * (x * x * x))))


def _rotary_tables(seq):
    half = ROPE_DIM // 2
    inv_freq = ROPE_THETA ** (-jnp.arange(half, dtype=F32) * (2.0 / ROPE_DIM))
    ang = jnp.arange(seq, dtype=F32)[:, None] * inv_freq[None, :]
    cos, sin = jnp.cos(ang), jnp.sin(ang)
    pad = HEAD_DIM - ROPE_DIM
    cos_t = jnp.concatenate([cos, cos, jnp.ones((seq, pad), F32)], axis=1)
    sin_hi = jnp.concatenate([-sin, jnp.zeros((seq, half + pad), F32)], axis=1)
    sin_lo = jnp.concatenate([jnp.zeros((seq, half), F32), sin, jnp.zeros((seq, pad), F32)], axis=1)
    return jnp.stack([cos_t, sin_hi, sin_lo])


def _block_diag_gates(wx, wa):
    n_grp = RNN_BLOCKS // LRU_GROUP

    def bd(w):
        w = w.reshape(n_grp, LRU_GROUP, RNN_BLOCK_DIM, RNN_BLOCK_DIM)
        eye = jnp.eye(LRU_GROUP, dtype=w.dtype)
        full = jnp.einsum('gaij,ab->gaibj', w, eye)
        return full.reshape(n_grp, V7X_MXU_DIM, V7X_MXU_DIM)

    return jnp.concatenate([bd(wx), bd(wa)], axis=2).astype(BF16)


def _mixer_in_body(x_ref, g_ref, w_ref, rot_ref, cw_ref, cb_ref, wbd_ref, bx_ref, ba_ref, ap_ref,
                   hg_ref, q_ref, k_ref, v_ref, sr_ref, sa_ref,
                   xe_s, alast_s, ulast_s, hin_s, h_s, hprev_s):
    s = pl.program_id(1)
    tm = x_ref.shape[0]
    n_halo = xe_s.shape[1] - tm
    n_slab = xe_s.shape[0]
    n_grp = tm // SUBLANES
    x = x_ref[...]
    ms = jnp.mean(x * x, axis=-1, keepdims=True)
    xn = (x * lax.rsqrt(ms + EPS) * g_ref[...]).astype(BF16)

    def proj(j):
        return jnp.dot(xn, w_ref[:, j * D_MODEL:(j + 1) * D_MODEL], preferred_element_type=F32)

    cos, sin_hi, sin_lo = rot_ref[0], rot_ref[1], rot_ref[2]
    half = ROPE_DIM // 2

    def rotary_heads(z, out_ref, scale):
        for h in range(N_HEADS):
            zh = z[:, h * HEAD_DIM:(h + 1) * HEAD_DIM]
            r = zh * cos + pltpu.roll(zh, HEAD_DIM - half, axis=1) * sin_hi + pltpu.roll(zh, half, axis=1) * sin_lo
            out_ref[h] = (r * scale).astype(BF16)

    rx = proj(0)
    for c in range(n_slab):
        tail = xe_s[c, tm:tm + n_halo, :]
        xe_s[c, 0:n_halo, :] = jnp.where(s == 0, 0.0, tail)
        xe_s[c, n_halo:, :] = rx[:, c * LANES:(c + 1) * LANES]
    gate_gelu = _gelu_tanh(proj(1))

    def recurrent_group(g):
        cols = slice(g * V7X_MXU_DIM, (g + 1) * V7X_MXU_DIM)
        slabs = range(g * V7X_MXU_DIM // LANES, (g + 1) * V7X_MXU_DIM // LANES)

        def phase(d):
            return jnp.concatenate([xe_s[c, pl.ds(n_halo + d, n_grp, stride=SUBLANES), :] for c in slabs], axis=1)

        shifted = {d: phase(d) for d in range(-(CONV_WIDTH - 1), SUBLANES)}
        blocks = []
        for j in range(SUBLANES):
            blk = cb_ref[:, cols]
            for tap in range(CONV_WIDTH):
                blk = blk + cw_ref[tap:tap + 1, cols] * shifted[j - (CONV_WIDTH - 1) + tap]
            blocks.append(blk)
        xc = jnp.concatenate(blocks, axis=0)

        r = jnp.dot(xc.astype(BF16), wbd_ref[g], preferred_element_type=F32)
        gate_x = jax.nn.sigmoid(r[:, :V7X_MXU_DIM] + bx_ref[:, cols])
        gate_a = jax.nn.sigmoid(r[:, V7X_MXU_DIM:] + ba_ref[:, cols])
        z = -ap_ref[:, cols]
        softplus = jnp.maximum(z, 0.0) + jnp.log1p(jnp.exp(-jnp.abs(z)))
        log_a = -LRU_C * gate_a * softplus
        a = jnp.exp(log_a)
        u = jnp.sqrt(-jnp.tanh(log_a) * (a * a + 1.0)) * (gate_x * xc)

        cums = []
        a_cum = u_cum = None
        for j in range(SUBLANES):
            rows = slice(j * n_grp, (j + 1) * n_grp)
            a_j, u_j = a[rows, :], u[rows, :]
            a_cum, u_cum = (a_j, u_j) if j == 0 else (a_j * a_cum, a_j * u_cum + u_j)
            cums.append((a_cum, u_cum))
        alast_s[:, cols] = a_cum
        ulast_s[:, cols] = u_cum
        h = jnp.where(s == 0, 0.0, hprev_s[:, cols])
        for i in range(n_grp):
            hin_s[i:i + 1, cols] = h
            h = ulast_s[i:i + 1, cols] + alast_s[i:i + 1, cols] * h
        hprev_s[:, cols] = h
        h_in = hin_s[0:n_grp, cols]
        for j, (a_c, u_c) in enumerate(cums):
            h_j = u_c + a_c * h_in
            for k, c in enumerate(slabs):
                h_s[c, pl.ds(j, n_grp, stride=SUBLANES), :] = h_j[:, k * LANES:(k + 1) * LANES]
        h_grp = jnp.concatenate([h_s[c] for c in slabs], axis=1)
        hg_ref[:, cols] = (h_grp * gate_gelu[:, cols]).astype(BF16)

    recurrent_group(0)
    rotary_heads(proj(2), q_ref, 1.0 / math.sqrt(HEAD_DIM))
    recurrent_group(1)
    rotary_heads(proj(3), k_ref, 1.0)
    recurrent_group(2)
    sr_ref[...] = jax.nn.sigmoid(proj(5)).astype(BF16)
    sa_ref[...] = jax.nn.sigmoid(proj(6)).astype(BF16)
    recurrent_group(3)
    zv = proj(4)
    for h in range(N_HEADS):
        v_ref[h] = zv[:, h * HEAD_DIM:(h + 1) * HEAD_DIM].astype(BF16)


def _mixer_in(x2, norm_g, w_in, conv_w, conv_b, wx, bx, wa, ba, a_param, bsz, seq):
    n_tok = x2.shape[0]
    tm = ROW_TILE
    tiles_per_seq = seq // tm
    in_cols = w_in.shape[1]
    halo_rows = SUBLANES
    const2 = lambda b, s: (0, 0)
    row = pl.BlockSpec((tm, D_MODEL), lambda b, s: (b * tiles_per_seq + s, 0))
    head = pl.BlockSpec((None, N_HEADS, tm, HEAD_DIM), lambda b, s: (b, 0, s, 0))
    vec = pl.BlockSpec((1, RNN_WIDTH), const2)
    n_grp = RNN_WIDTH // V7X_MXU_DIM
    tok_bf16 = jax.ShapeDtypeStruct((n_tok, D_MODEL), BF16)
    head_bf16 = jax.ShapeDtypeStruct((bsz, N_HEADS, seq, HEAD_DIM), BF16)
    return pl.pallas_call(
        _mixer_in_body,
        grid=(bsz, tiles_per_seq),
        in_specs=[
            row,
            pl.BlockSpec((1, D_MODEL), const2),
            pl.BlockSpec((D_MODEL, in_cols), const2, pipeline_mode=pl.Buffered(1)),
            pl.BlockSpec((3, tm, HEAD_DIM), lambda b, s: (0, s, 0)),
            pl.BlockSpec((CONV_WIDTH, RNN_WIDTH), const2), vec,
            pl.BlockSpec((n_grp, V7X_MXU_DIM, 2 * V7X_MXU_DIM), lambda b, s: (0, 0, 0)),
            vec, vec, vec,
        ],
        out_specs=[row, head, head, head, row, row],
        out_shape=[tok_bf16, head_bf16, head_bf16, head_bf16, tok_bf16, tok_bf16],
        scratch_shapes=[pltpu.VMEM((RNN_WIDTH // LANES, halo_rows + tm, LANES), F32),
                        pltpu.VMEM((tm // SUBLANES, RNN_WIDTH), F32), pltpu.VMEM((tm // SUBLANES, RNN_WIDTH), F32),
                        pltpu.VMEM((tm // SUBLANES, RNN_WIDTH), F32),
                        pltpu.VMEM((RNN_WIDTH // LANES, tm, LANES), F32), pltpu.VMEM((1, RNN_WIDTH), F32)],
        compiler_params=pltpu.CompilerParams(dimension_semantics=("parallel", "arbitrary"),
                                             vmem_limit_bytes=VMEM_LIMIT),
        name="mixer_in",
    )(x2, norm_g.reshape(1, D_MODEL), w_in.astype(BF16), _rotary_tables(seq), conv_w, conv_b.reshape(1, -1),
      _block_diag_gates(wx, wa), bx.reshape(1, -1), ba.reshape(1, -1), a_param.reshape(1, -1))


def _moba_body(q_all, k_all, v_all, o_all, kaug_all, qaug_all):
    n_heads_here, seq = q_all.shape[0], q_all.shape[1]
    nb = seq // MOBA_BLOCK
    nb_pad = -(-nb // SUBLANES) * SUBLANES
    key_blk = lax.broadcasted_iota(jnp.int32, (seq, LANES), 0) // MOBA_BLOCK
    lane = lax.broadcasted_iota(jnp.int32, (seq, LANES), 1)
    r_id = lax.broadcasted_iota(jnp.int32, (MOBA_BLOCK, MOBA_BLOCK), 0)
    c_id = lax.broadcasted_iota(jnp.int32, (MOBA_BLOCK, MOBA_BLOCK), 1)
    causal = c_id <= r_id
    eye = (c_id == r_id).astype(BF16)
    j_id = lax.broadcasted_iota(jnp.int32, (nb_pad, seq), 0)
    own = lax.broadcasted_iota(jnp.int32, (1, seq), 1) // MOBA_BLOCK

    def prepare(q_ref, k_ref, kaug_s, qaug_s):
        kmean = jnp.mean(k_ref[...].astype(F32).reshape(nb, MOBA_BLOCK, HEAD_DIM), axis=1)
        if nb_pad > nb:
            kmean = jnp.concatenate([kmean, jnp.zeros((nb_pad - nb, HEAD_DIM), F32)], axis=0)
        kaug_s[:, :HEAD_DIM] = k_ref[...]
        kaug_s[:, HEAD_DIM:] = (key_blk == lane).astype(BF16)
        gate = _nt_dot(kmean, q_ref[...].astype(F32), precision=lax.Precision.HIGHEST)
        rank = jnp.zeros((nb_pad, seq), F32)
        for jp in range(nb - 1):
            row = gate[jp:jp + 1, :]
            beats = ((row > gate) | ((row == gate) & (jp < j_id))) & (jp < own)
            rank = rank + beats.astype(F32)
        allowed = (j_id >= own) | (rank < float(MOBA_TOPK))
        bias_t = jnp.where(allowed, 0.0, NEG)
        bias_t = jnp.concatenate([bias_t, jnp.zeros((LANES - nb_pad, seq), F32)], axis=0).astype(BF16)
        qaug_s[:, :HEAD_DIM] = q_ref[...]
        for n in range(nb):
            cols = slice(n * MOBA_BLOCK, (n + 1) * MOBA_BLOCK)
            qaug_s[cols, HEAD_DIM:] = _nt_dot(eye, bias_t[:, cols]).astype(BF16)

    def masked_scores(h, n):
        s = _nt_dot(qaug_all[h, n * MOBA_BLOCK:(n + 1) * MOBA_BLOCK, :], kaug_all[h, 0:(n + 1) * MOBA_BLOCK, :])
        parts = [s[:, j * MOBA_BLOCK:(j + 1) * MOBA_BLOCK] for j in range(n + 1)]
        parts[n] = jnp.where(causal, parts[n], NEG)
        return parts

    def attend(h, n, parts):
        m = functools.reduce(jnp.maximum, parts)
        m = jnp.max(m, axis=1, keepdims=True)
        probs = [jnp.exp(part - m) for part in parts]
        l = jnp.sum(functools.reduce(jnp.add, probs), axis=1, keepdims=True)
        p_all = jnp.concatenate([p.astype(BF16) for p in probs], axis=1)
        acc = jnp.dot(p_all, v_all[h, 0:(n + 1) * MOBA_BLOCK, :], preferred_element_type=F32)
        o_all[h, n * MOBA_BLOCK:(n + 1) * MOBA_BLOCK, :] = (acc * (1.0 / l)).astype(BF16)

    for h in range(n_heads_here):
        prepare(q_all.at[h], k_all.at[h], kaug_all.at[h], qaug_all.at[h])
    ahead = [masked_scores(h, 0) for h in range(n_heads_here)]
    for n in range(nb):
        for h in range(n_heads_here):
            parts = ahead[h]
            if n + 1 < nb:
                ahead[h] = masked_scores(h, n + 1)
            attend(h, n, parts)


def _moba(q, k, v):
    bsz, n_heads, seq, hd = q.shape
    g = MOBA_HEADS_PER_STEP
    spec = pl.BlockSpec((None, g, seq, hd), lambda b, h: (b, h, 0, 0))
    return pl.pallas_call(
        _moba_body,
        grid=(bsz, n_heads // g),
        in_specs=[spec, spec, spec],
        out_specs=spec,
        out_shape=jax.ShapeDtypeStruct(q.shape, BF16),
        scratch_shapes=[pltpu.VMEM((g, seq, 2 * hd), BF16), pltpu.VMEM((g, seq, 2 * hd), BF16)],
        compiler_params=pltpu.CompilerParams(dimension_semantics=("parallel", "parallel"),
                                             vmem_limit_bytes=VMEM_LIMIT),
        name="moba",
    )(q, k, v)


def _merge_body(x_ref, hg_ref, at_ref, sr_ref, sa_ref, wpr_ref, wpa_ref, wo_ref, gffn_ref, rwt_ref, rb_ref,
                x1_ref, hn_ref, idx_ref, gate_ref, rank_ref, cnt_ref, carry_s):
    tm = x_ref.shape[0]

    @pl.when(pl.program_id(0) == 0)
    def _():
        carry_s[...] = jnp.zeros_like(carry_s)

    y_rnn = jnp.dot(hg_ref[...], wpr_ref[...], preferred_element_type=F32)
    attn = jnp.concatenate([at_ref[h] for h in range(N_HEADS)], axis=1)
    y_attn = jnp.dot(attn, wpa_ref[...], preferred_element_type=F32)
    mixed = sr_ref[...].astype(F32) * y_rnn + sa_ref[...].astype(F32) * y_attn
    x1 = x_ref[...] + jnp.dot(mixed.astype(BF16), wo_ref[...], preferred_element_type=F32)
    x1_ref[...] = x1
    hn = x1 * lax.rsqrt(jnp.mean(x1 * x1, axis=-1, keepdims=True) + EPS) * gffn_ref[...]
    _store_row_tiles(hn_ref, hn)

    logits = _nt_dot(rwt_ref[...], hn.astype(BF16)) + rb_ref[...]
    e_id = lax.broadcasted_iota(jnp.int32, (N_EXPERTS, tm), 0)
    vals = logits
    onehots, top_vals, top_ids = [], [], []
    for _ in range(TOP_K):
        best = jnp.max(vals, axis=0, keepdims=True)
        best_id = jnp.min(jnp.where(vals == best, e_id, N_EXPERTS), axis=0, keepdims=True)
        hit = e_id == best_id
        vals = jnp.where(hit, -jnp.inf, vals)
        onehots.append(hit)
        top_vals.append(best)
        top_ids.append(best_id)
    exps = [jnp.exp(v - top_vals[0]) for v in top_vals]
    denom = exps[0] + exps[1] + exps[2] + exps[3]
    idx_ref[...] = jnp.concatenate(top_ids, axis=0)
    gate_ref[...] = jnp.concatenate([e / denom for e in exps], axis=0)

    chosen = onehots[0] | onehots[1] | onehots[2] | onehots[3]
    t_row = lax.broadcasted_iota(jnp.int32, (tm, tm), 0)
    t_col = lax.broadcasted_iota(jnp.int32, (tm, tm), 1)
    before = (t_row < t_col).astype(BF16)
    chosen_f = chosen.astype(F32)
    prior = jnp.dot(chosen_f.astype(BF16), before, preferred_element_type=F32) + carry_s[:, 0:1]
    ranks = [jnp.sum(jnp.where(hit, prior, 0.0), axis=0, keepdims=True) for hit in onehots]
    rank_ref[...] = jnp.concatenate(ranks, axis=0).astype(jnp.int32)
    carry_s[...] = carry_s[...] + jnp.sum(chosen_f, axis=1, keepdims=True)
    cnt_ref[...] = carry_s[...].astype(jnp.int32)


def _merge(x2, hg, attn, sr, sa, w_proj_rnn, w_proj_attn, w_out, norm_ffn_g, router_w, router_b, seq):
    n_tok = x2.shape[0]
    tm = ROW_TILE
    tiles_per_seq = seq // tm
    row = pl.BlockSpec((tm, D_MODEL), lambda i: (i, 0))
    head = pl.BlockSpec((None, N_HEADS, tm, HEAD_DIM), lambda i: (i // tiles_per_seq, 0, i % tiles_per_seq, 0))
    mat = pl.BlockSpec((D_MODEL, D_MODEL), lambda i: (0, 0))
    topk = pl.BlockSpec((TOP_K, tm), lambda i: (0, i))
    return pl.pallas_call(
        _merge_body,
        grid=(n_tok // tm,),
        in_specs=[row, row, head, row, row, mat, mat, mat,
                  pl.BlockSpec((1, D_MODEL), lambda i: (0, 0)),
                  pl.BlockSpec((N_EXPERTS, D_MODEL), lambda i: (0, 0)),
                  pl.BlockSpec((N_EXPERTS, 1), lambda i: (0, 0))],
        out_specs=[row, pl.BlockSpec((tm * ROW_TILES, LANES), lambda i: (i, 0)), topk, topk, topk,
                   pl.BlockSpec((N_EXPERTS, LANES), lambda i: (0, 0))],
        out_shape=[jax.ShapeDtypeStruct((n_tok, D_MODEL), F32),
                   jax.ShapeDtypeStruct((n_tok * ROW_TILES, LANES), F32),
                   jax.ShapeDtypeStruct((TOP_K, n_tok), jnp.int32), jax.ShapeDtypeStruct((TOP_K, n_tok), F32),
                   jax.ShapeDtypeStruct((TOP_K, n_tok), jnp.int32),
                   jax.ShapeDtypeStruct((N_EXPERTS, LANES), jnp.int32)],
        scratch_shapes=[pltpu.VMEM((N_EXPERTS, LANES), F32)],
        compiler_params=pltpu.CompilerParams(dimension_semantics=("arbitrary",), vmem_limit_bytes=VMEM_LIMIT),
        name="merge_route",
    )(x2, hg, attn, sr, sa, w_proj_rnn.astype(BF16), w_proj_attn.astype(BF16), w_out.astype(BF16),
      norm_ffn_g.reshape(1, D_MODEL), router_w.T.astype(BF16), router_b.reshape(N_EXPERTS, 1))


def _row_copy(src_ref, src_row, dst_ref, dst_row, sem):
    return pltpu.make_async_copy(_row_slab(src_ref, src_row), _row_slab(dst_ref, dst_row), sem)


def _tile_flat(dest, tm):
    return dest.reshape(TOP_K, -1, tm).transpose(1, 0, 2).reshape(-1)


def _for_each_token_group(tm, body):
    def group(g, c):
        body(pl.multiple_of(g * ISSUE_UNROLL, ISSUE_UNROLL))
        return c

    lax.fori_loop(0, tm // ISSUE_UNROLL, group, 0)


def _start_group_copies(copy_of, t0):
    for i in range(ISSUE_UNROLL):
        for k in range(TOP_K):
            copy_of(t0 + i, k).start(priority=(i * TOP_K + k) % DMA_QUEUES)


def _dispatch_body(pad_off_ref, pad_len_ref, dest_ref, hn_hbm, rows_ref, hbuf, load_sems, scat_sems, pad_sem):
    tm = hbuf.shape[1] // ROW_TILES
    tile_rows = tm * ROW_TILES
    step = pl.program_id(0)
    n_steps = pl.num_programs(0)
    slot = step % DISPATCH_SLOTS
    prev_slot = (step + DISPATCH_SLOTS - 1) % DISPATCH_SLOTS

    def load(tile, into):
        start = pl.multiple_of(tile * tile_rows, tile_rows)
        return pltpu.make_async_copy(hn_hbm.at[pl.ds(start, tile_rows), :], hbuf.at[into], load_sems.at[into])

    def wait_scatters(of_slot):
        for _ in range(TOP_K):
            pltpu.make_async_copy(hbuf.at[of_slot], rows_ref.at[pl.ds(0, tile_rows), :], scat_sems.at[of_slot]).wait()

    def for_each_pad_run(act):
        def per_expert(e, c):
            off, length = pad_off_ref[e], pad_len_ref[e]
            for bit in range(EXPERT_BLOCK_ROWS.bit_length() - 1):
                rows = 1 << bit

                @pl.when(((length >> bit) & 1) == 1)
                def _():
                    start = pl.multiple_of((off + (length & (rows - 1))) * ROW_TILES, ROW_TILES)
                    act(pltpu.make_async_copy(hbuf.at[0, pl.ds(0, rows * ROW_TILES), :],
                                              rows_ref.at[pl.ds(start, rows * ROW_TILES), :], pad_sem))
            return c

        lax.fori_loop(0, N_EXPERTS, per_expert, 0)

    @pl.when(step == 0)
    def _():
        load(0, 0).start()

        @pl.when(n_steps > 1)
        def _():
            load(1, 1).start()

    load(step, slot).wait()

    @pl.when(step == 0)
    def _():
        for_each_pad_run(lambda cp: cp.start())

    src = hbuf.at[slot]
    scatter = lambda t, k: _row_copy(src, t, rows_ref, dest_ref[k * tm + t], scat_sems.at[slot])
    _for_each_token_group(tm, lambda t0: _start_group_copies(scatter, t0))

    @pl.when(step == 0)
    def _():
        for_each_pad_run(lambda cp: cp.wait())

    @pl.when(step >= 1)
    def _():
        wait_scatters(prev_slot)

    @pl.when(step + 2 < n_steps)
    def _():
        load(step + 2, prev_slot).start()

    @pl.when(step == n_steps - 1)
    def _():
        wait_scatters(slot)


def _dispatch(dest, pad_off, pad_len, hn, n_rows):
    n_tok = hn.shape[0] // ROW_TILES
    tm = math.gcd(DISPATCH_TILE, n_tok)
    assert tm >= EXPERT_BLOCK_ROWS // 2, "pad runs are sourced from the first token tile"
    grid_spec = pltpu.PrefetchScalarGridSpec(
        num_scalar_prefetch=2,
        grid=(n_tok // tm,),
        in_specs=[pl.BlockSpec((TOP_K * tm,), lambda i, po, pn: (i,), memory_space=pltpu.SMEM),
                  pl.BlockSpec(memory_space=pl.ANY)],
        out_specs=pl.BlockSpec(memory_space=pl.ANY),
        scratch_shapes=[pltpu.VMEM((DISPATCH_SLOTS, tm * ROW_TILES, LANES), F32),
                        pltpu.SemaphoreType.DMA((DISPATCH_SLOTS,)), pltpu.SemaphoreType.DMA((DISPATCH_SLOTS,)),
                        pltpu.SemaphoreType.DMA(())],
    )
    return pl.pallas_call(
        _dispatch_body,
        grid_spec=grid_spec,
        out_shape=jax.ShapeDtypeStruct((n_rows * ROW_TILES, LANES), F32),
        compiler_params=pltpu.CompilerParams(dimension_semantics=("arbitrary",), disable_bounds_checks=True,
                                             vmem_limit_bytes=VMEM_LIMIT),
        name="dispatch",
    )(pad_off, pad_len, _tile_flat(dest, tm), hn)


def _expert_body(blk_expert_ref, n_used_ref, next_expert_ref, slot_ref, x_ref, wgu_hbm, bgu_ref, wd_hbm, bd_ref, y_ref,
                 wgu_buf, wd_buf, wgu_s, wd_s, sems):
    i = pl.program_id(0)
    live = i < n_used_ref[0]
    expert = blk_expert_ref[i]
    new_expert = (i == 0) | (expert != blk_expert_ref[jnp.maximum(i - 1, 0)])

    def weight_copies(e, slot):
        return (pltpu.make_async_copy(wgu_hbm.at[e], wgu_buf.at[slot], sems.at[0, slot]),
                pltpu.make_async_copy(wd_hbm.at[e], wd_buf.at[slot], sems.at[1, slot]))

    @pl.when(live & new_expert)
    def _():
        slot = slot_ref[expert]
        upcoming = next_expert_ref[expert]

        @pl.when(i == 0)
        def _():
            for cp in weight_copies(expert, slot):
                cp.start()

        @pl.when(upcoming >= 0)
        def _():
            for cp in weight_copies(upcoming, 1 - slot):
                cp.start()

        for cp in weight_copies(expert, slot):
            cp.wait()
        for r in range(0, D_MODEL, WEIGHT_CAST_ROWS):
            wgu_s[r:r + WEIGHT_CAST_ROWS, :] = wgu_buf[slot, r:r + WEIGHT_CAST_ROWS, :].astype(BF16)
        for r in range(0, D_EXPERT, WEIGHT_CAST_ROWS):
            wd_s[r:r + WEIGHT_CAST_ROWS, :] = wd_buf[slot, r:r + WEIGHT_CAST_ROWS, :].astype(BF16)

    @pl.when(live)
    def _():
        gu = jnp.dot(_load_row_tiles(x_ref).astype(BF16), wgu_s[...], preferred_element_type=F32) + bgu_ref[...]
        x_glu = jnp.minimum(gu[:, :D_EXPERT], SWIGLU_LIMIT)
        x_lin = jnp.clip(gu[:, D_EXPERT:], -SWIGLU_LIMIT, SWIGLU_LIMIT)
        act = x_glu * jax.nn.sigmoid(SWIGLU_ALPHA * x_glu) * (x_lin + 1.0)
        _store_row_tiles(y_ref, jnp.dot(act.astype(BF16), wd_s[...], preferred_element_type=F32) + bd_ref[...])

    @pl.when(jnp.logical_not(live))
    def _():
        y_ref[...] = jnp.zeros_like(y_ref)


def _experts(blk_expert, n_used, next_expert, slot_of, x_rows, w_gu, b_gu, w_down, b_down):
    n_rows = x_rows.shape[0] // ROW_TILES
    rb = EXPERT_BLOCK_ROWS
    grid_spec = pltpu.PrefetchScalarGridSpec(
        num_scalar_prefetch=4,
        grid=(n_rows // rb,),
        in_specs=[
            pl.BlockSpec((rb * ROW_TILES, LANES), lambda i, be, nu, nx, sl: (jnp.minimum(i, nu[0] - 1), 0)),
            pl.BlockSpec(memory_space=pl.ANY),
            pl.BlockSpec((None, 1, 2 * D_EXPERT), lambda i, be, nu, nx, sl: (be[i], 0, 0)),
            pl.BlockSpec(memory_space=pl.ANY),
            pl.BlockSpec((None, 1, D_MODEL), lambda i, be, nu, nx, sl: (be[i], 0, 0)),
        ],
        out_specs=pl.BlockSpec((rb * ROW_TILES, LANES), lambda i, be, nu, nx, sl: (i, 0)),
        scratch_shapes=[pltpu.VMEM((2, D_MODEL, 2 * D_EXPERT), F32), pltpu.VMEM((2, D_EXPERT, D_MODEL), F32),
                        pltpu.VMEM((D_MODEL, 2 * D_EXPERT), BF16), pltpu.VMEM((D_EXPERT, D_MODEL), BF16),
                        pltpu.SemaphoreType.DMA((2, 2))],
    )
    return pl.pallas_call(
        _expert_body,
        grid_spec=grid_spec,
        out_shape=jax.ShapeDtypeStruct((n_rows * ROW_TILES, LANES), F32),
        compiler_params=pltpu.CompilerParams(dimension_semantics=("arbitrary",), vmem_limit_bytes=VMEM_LIMIT),
        name="experts",
    )(blk_expert, n_used, next_expert, slot_of, x_rows, w_gu, b_gu.reshape(N_EXPERTS, 1, -1), w_down,
      b_down.reshape(N_EXPERTS, 1, -1))


def _combine_body(dest_ref, dest_next_ref, gate_ref, x1_ref, y_ref, gfin_ref, out_ref, ybuf, sems):
    tm = x1_ref.shape[0]
    step = pl.program_id(0)
    slot = step % 2

    def gather(d_ref, into):
        fetch = lambda t, k: _row_copy(y_ref, d_ref[k * tm + t], ybuf.at[into, k], t, sems.at[into])
        _for_each_token_group(tm, lambda t0: _start_group_copies(fetch, t0))

    @pl.when(step == 0)
    def _():
        gather(dest_ref, 0)

    @pl.when(step + 1 < pl.num_programs(0))
    def _():
        gather(dest_next_ref, 1 - slot)

    for k in range(TOP_K):
        pltpu.make_async_copy(y_ref.at[pl.ds(0, tm * ROW_TILES), :], ybuf.at[slot, k], sems.at[slot]).wait()
    gates = gate_ref[...]
    x2 = x1_ref[...]
    for k in range(TOP_K):
        x2 = x2 + gates[:, k:k + 1] * _load_row_tiles(ybuf.at[slot, k])
    out_ref[...] = x2 * lax.rsqrt(jnp.mean(x2 * x2, axis=-1, keepdims=True) + EPS) * gfin_ref[...]


def _combine(dest, gates_tok, x1, y_rows, norm_final_g):
    n_tok = x1.shape[0]
    tm = GATHER_TILE
    row = pl.BlockSpec((tm, D_MODEL), lambda i: (i, 0))
    n_tiles = n_tok // tm
    return pl.pallas_call(
        _combine_body,
        grid=(n_tiles,),
        in_specs=[pl.BlockSpec((TOP_K * tm,), lambda i: (i,), memory_space=pltpu.SMEM),
                  pl.BlockSpec((TOP_K * tm,), lambda i: (jnp.minimum(i + 1, n_tiles - 1),), memory_space=pltpu.SMEM),
                  pl.BlockSpec((tm, TOP_K), lambda i: (i, 0)),
                  row,
                  pl.BlockSpec(memory_space=pl.ANY),
                  pl.BlockSpec((1, D_MODEL), lambda i: (0, 0))],
        out_specs=row,
        out_shape=jax.ShapeDtypeStruct((n_tok, D_MODEL), F32),
        scratch_shapes=[pltpu.VMEM((2, TOP_K, tm * ROW_TILES, LANES), F32), pltpu.SemaphoreType.DMA((2,))],
        compiler_params=pltpu.CompilerParams(dimension_semantics=("arbitrary",), vmem_limit_bytes=VMEM_LIMIT,
                                             disable_bounds_checks=True),
        name="combine",
    )(_tile_flat(dest, tm), _tile_flat(dest, tm), gates_tok, x1, y_rows, norm_final_g.reshape(1, D_MODEL))


def _routing_plan(top_idx, rank, counts, n_tok):
    padded = ((counts + EXPERT_BLOCK_ROWS - 1) // EXPERT_BLOCK_ROWS) * EXPERT_BLOCK_ROWS
    pend = jnp.cumsum(padded)
    pstart = pend - padded
    e_id = jnp.arange(N_EXPERTS, dtype=jnp.int32)[:, None, None]
    dest = jnp.sum(jnp.where(top_idx[None] == e_id, pstart[:, None, None], 0), axis=0) + rank
    n_rows = n_tok * TOP_K + N_EXPERTS * EXPERT_BLOCK_ROWS
    blk_start = jnp.arange(n_rows // EXPERT_BLOCK_ROWS, dtype=jnp.int32) * EXPERT_BLOCK_ROWS
    blk_expert = jnp.minimum(jnp.sum(blk_start[:, None] >= pend[None, :], axis=1), N_EXPERTS - 1)
    n_used = (pend[-1] // EXPERT_BLOCK_ROWS).reshape(1)
    pad_off, pad_len = pstart + counts, padded - counts
    used = counts > 0
    ids = jnp.arange(N_EXPERTS, dtype=jnp.int32)
    later = jnp.where(used[None, :] & (ids[None, :] > ids[:, None]), ids[None, :], N_EXPERTS)
    next_expert = jnp.min(later, axis=1)
    next_expert = jnp.where(next_expert == N_EXPERTS, -1, next_expert)
    slot_of = (jnp.cumsum(used) - used) % 2
    return (dest.astype(jnp.int32), blk_expert.astype(jnp.int32), n_used.astype(jnp.int32),
            next_expert.astype(jnp.int32), slot_of.astype(jnp.int32),
            pad_off.astype(jnp.int32), pad_len.astype(jnp.int32), n_rows)


def kernel(x, norm_mix_g, w_in, conv_w, conv_b, lru_wx, lru_bx, lru_wa, lru_ba, lru_a_param, w_proj_rnn, w_proj_attn, w_out, norm_ffn_g, router_w, router_b, expert_w_gu, expert_b_gu, expert_w_down, expert_b_down, norm_final_g):
    bsz, seq, _ = x.shape
    n_tok = bsz * seq
    assert w_in.shape[0] == 1, "single-layer problem: the final RMSNorm is fused into the combine stage"
    assert seq % ROW_TILE == 0 and seq % MOBA_BLOCK == 0
    layer = 0
    x2 = x.reshape(n_tok, D_MODEL)
    hg, q, k, v, sr, sa = _mixer_in(x2, norm_mix_g[layer], w_in[layer], conv_w[layer], conv_b[layer],
                                    lru_wx[layer], lru_bx[layer], lru_wa[layer], lru_ba[layer],
                                    lru_a_param[layer], bsz, seq)
    attn = _moba(q, k, v)
    x1, hn, top_idx, gates, rank, cnt = _merge(x2, hg, attn, sr, sa, w_proj_rnn[layer], w_proj_attn[layer],
                                               w_out[layer], norm_ffn_g[layer], router_w[layer],
                                               router_b[layer], seq)
    dest, blk_expert, n_used, next_expert, slot_of, pad_off, pad_len, n_rows = _routing_plan(
        top_idx, rank, cnt[:, 0], n_tok)
    x_rows = _dispatch(dest, pad_off, pad_len, hn, n_rows)
    y_rows = _experts(blk_expert, n_used, next_expert, slot_of, x_rows, expert_w_gu[layer], expert_b_gu[layer],
                      expert_w_down[layer], expert_b_down[layer])
    out = _combine(dest, gates.T, x1, y_rows, norm_final_g)
    return out.reshape(bsz, seq, D_MODEL)
```

```python
import functools
import math

import jax
import jax.numpy as jnp
from jax import lax
from jax.experimental import pallas as pl
from jax.experimental.pallas import tpu as pltpu

F32 = jnp.float32
BF16 = jnp.bfloat16

D_MODEL = 1024
RNN_WIDTH = 1024
RNN_BLOCKS = 16
RNN_BLOCK_DIM = RNN_WIDTH // RNN_BLOCKS
CONV_WIDTH = 4
LRU_C = 8.0
N_HEADS = 8
HEAD_DIM = 128
ROPE_DIM = HEAD_DIM // 4
ROPE_THETA = 500000.0
MOBA_BLOCK = 256
MOBA_TOPK = 3
N_EXPERTS = 32
TOP_K = 4
D_EXPERT = 1024
SWIGLU_LIMIT = 7.0
SWIGLU_ALPHA = 1.702
EPS = 1e-6
NEG = -1e30

V7X_VMEM_BYTES = 64 * 1024 * 1024
V7X_MXU_DIM = 256
SUBLANES = 8
LANES = 128

ROW_TILE = 512
EXPERT_BLOCK_ROWS = 512
WEIGHT_CAST_ROWS = 128
MOBA_HEADS_PER_STEP = 2
DISPATCH_TILE = 1024
DISPATCH_SLOTS = 3
GATHER_TILE = 256
ISSUE_UNROLL = 16
DMA_QUEUES = 2
LRU_GROUP = V7X_MXU_DIM // RNN_BLOCK_DIM
VMEM_LIMIT = V7X_VMEM_BYTES - 8 * 1024 * 1024


def _nt_dot(a, b, **kw):
    return lax.dot_general(a, b, (((1,), (1,)), ((), ())), preferred_element_type=F32, **kw)


ROW_TILES = D_MODEL // LANES
assert ROW_TILES == SUBLANES


def _store_row_tiles(ref, val):
    rows = val.shape[0]
    for s in range(ROW_TILES):
        ref[pl.ds(s, rows, stride=ROW_TILES), :] = val[:, s * LANES:(s + 1) * LANES]


def _load_row_tiles(ref):
    rows = ref.shape[0] // ROW_TILES
    return jnp.concatenate([ref[pl.ds(s, rows, stride=ROW_TILES), :] for s in range(ROW_TILES)], axis=1)


def _row_slab(ref, row):
    return ref.at[pl.ds(pl.multiple_of(row * ROW_TILES, ROW_TILES), ROW_TILES), :]


def _gelu_tanh(x):
    return 0.5 * x * (1.0 + jnp.tanh(math.sqrt(2.0 / math.pi) * (x + 0.044715 * (x * x * x))))


def _rotary_tables(seq):
    half = ROPE_DIM // 2
    inv_freq = ROPE_THETA ** (-jnp.arange(half, dtype=F32) * (2.0 / ROPE_DIM))
    ang = jnp.arange(seq, dtype=F32)[:, None] * inv_freq[None, :]
    cos, sin = jnp.cos(ang), jnp.sin(ang)
    pad = HEAD_DIM - ROPE_DIM
    cos_t = jnp.concatenate([cos, cos, jnp.ones((seq, pad), F32)], axis=1)
    sin_hi = jnp.concatenate([-sin, jnp.zeros((seq, half + pad), F32)], axis=1)
    sin_lo = jnp.concatenate([jnp.zeros((seq, half), F32), sin, jnp.zeros((seq, pad), F32)], axis=1)
    return jnp.stack([cos_t, sin_hi, sin_lo])


def _block_diag_gates(wx, wa):
    n_grp = RNN_BLOCKS // LRU_GROUP

    def bd(w):
        w = w.reshape(n_grp, LRU_GROUP, RNN_BLOCK_DIM, RNN_BLOCK_DIM)
        eye = jnp.eye(LRU_GROUP, dtype=w.dtype)
        full = jnp.einsum('gaij,ab->gaibj', w, eye)
        return full.reshape(n_grp, V7X_MXU_DIM, V7X_MXU_DIM)

    return jnp.concatenate([bd(wx), bd(wa)], axis=2).astype(BF16)


def _mixer_in_body(x_ref, g_ref, w_ref, rot_ref, cw_ref, cb_ref, wbd_ref, bx_ref, ba_ref, ap_ref,
                   hg_ref, q_ref, k_ref, v_ref, sr_ref, sa_ref,
                   xe_s, alast_s, ulast_s, hin_s, h_s, hprev_s):
    s = pl.program_id(1)
    tm = x_ref.shape[0]
    n_halo = xe_s.shape[1] - tm
    n_slab = xe_s.shape[0]
    n_grp = tm // SUBLANES
    x = x_ref[...]
    ms = jnp.mean(x * x, axis=-1, keepdims=True)
    xn = (x * lax.rsqrt(ms + EPS) * g_ref[...]).astype(BF16)

    def proj(j):
        return jnp.dot(xn, w_ref[:, j * D_MODEL:(j + 1) * D_MODEL], preferred_element_type=F32)

    cos, sin_hi, sin_lo = rot_ref[0], rot_ref[1], rot_ref[2]
    half = ROPE_DIM // 2

    def rotary_heads(z, out_ref, scale):
        for h in range(N_HEADS):
            zh = z[:, h * HEAD_DIM:(h + 1) * HEAD_DIM]
            r = zh * cos + pltpu.roll(zh, HEAD_DIM - half, axis=1) * sin_hi + pltpu.roll(zh, half, axis=1) * sin_lo
            out_ref[h] = (r * scale).astype(BF16)

    rx = proj(0)
    for c in range(n_slab):
        tail = xe_s[c, tm:tm + n_halo, :]
        xe_s[c, 0:n_halo, :] = jnp.where(s == 0, 0.0, tail)
        xe_s[c, n_halo:, :] = rx[:, c * LANES:(c + 1) * LANES]
    gate_gelu = _gelu_tanh(proj(1))

    def recurrent_group(g):
        cols = slice(g * V7X_MXU_DIM, (g + 1) * V7X_MXU_DIM)
        slabs = range(g * V7X_MXU_DIM // LANES, (g + 1) * V7X_MXU_DIM // LANES)

        def phase(d):
            return jnp.concatenate([xe_s[c, pl.ds(n_halo + d, n_grp, stride=SUBLANES), :] for c in slabs], axis=1)

        shifted = {d: phase(d) for d in range(-(CONV_WIDTH - 1), SUBLANES)}
        blocks = []
        for j in range(SUBLANES):
            blk = cb_ref[:, cols]
            for tap in range(CONV_WIDTH):
                blk = blk + cw_ref[tap:tap + 1, cols] * shifted[j - (CONV_WIDTH - 1) + tap]
            blocks.append(blk)
        xc = jnp.concatenate(blocks, axis=0)

        r = jnp.dot(xc.astype(BF16), wbd_ref[g], preferred_element_type=F32)
        gate_x = jax.nn.sigmoid(r[:, :V7X_MXU_DIM] + bx_ref[:, cols])
        gate_a = jax.nn.sigmoid(r[:, V7X_MXU_DIM:] + ba_ref[:, cols])
        z = -ap_ref[:, cols]
        softplus = jnp.maximum(z, 0.0) + jnp.log1p(jnp.exp(-jnp.abs(z)))
        log_a = -LRU_C * gate_a * softplus
        a = jnp.exp(log_a)
        u = jnp.sqrt(-jnp.tanh(log_a) * (a * a + 1.0)) * (gate_x * xc)

        cums = []
        a_cum = u_cum = None
        for j in range(SUBLANES):
            rows = slice(j * n_grp, (j + 1) * n_grp)
            a_j, u_j = a[rows, :], u[rows, :]
            a_cum, u_cum = (a_j, u_j) if j == 0 else (a_j * a_cum, a_j * u_cum + u_j)
            cums.append((a_cum, u_cum))
        alast_s[:, cols] = a_cum
        ulast_s[:, cols] = u_cum
        h = jnp.where(s == 0, 0.0, hprev_s[:, cols])
        for i in range(n_grp):
            hin_s[i:i + 1, cols] = h
            h = ulast_s[i:i + 1, cols] + alast_s[i:i + 1, cols] * h
        hprev_s[:, cols] = h
        h_in = hin_s[0:n_grp, cols]
        for j, (a_c, u_c) in enumerate(cums):
            h_j = u_c + a_c * h_in
            for k, c in enumerate(slabs):
                h_s[c, pl.ds(j, n_grp, stride=SUBLANES), :] = h_j[:, k * LANES:(k + 1) * LANES]
        h_grp = jnp.concatenate([h_s[c] for c in slabs], axis=1)
        hg_ref[:, cols] = (h_grp * gate_gelu[:, cols]).astype(BF16)

    recurrent_group(0)
    rotary_heads(proj(2), q_ref, 1.0 / math.sqrt(HEAD_DIM))
    recurrent_group(1)
    rotary_heads(proj(3), k_ref, 1.0)
    recurrent_group(2)
    sr_ref[...] = jax.nn.sigmoid(proj(5)).astype(BF16)
    sa_ref[...] = jax.nn.sigmoid(proj(6)).astype(BF16)
    recurrent_group(3)
    zv = proj(4)
    for h in range(N_HEADS):
        v_ref[h] = zv[:, h * HEAD_DIM:(h + 1) * HEAD_DIM].astype(BF16)


def _mixer_in(x2, norm_g, w_in, conv_w, conv_b, wx, bx, wa, ba, a_param, bsz, seq):
    n_tok = x2.shape[0]
    tm = ROW_TILE
    tiles_per_seq = seq // tm
    in_cols = w_in.shape[1]
    halo_rows = SUBLANES
    const2 = lambda b, s: (0, 0)
    row = pl.BlockSpec((tm, D_MODEL), lambda b, s: (b * tiles_per_seq + s, 0))
    head = pl.BlockSpec((None, N_HEADS, tm, HEAD_DIM), lambda b, s: (b, 0, s, 0))
    vec = pl.BlockSpec((1, RNN_WIDTH), const2)
    n_grp = RNN_WIDTH // V7X_MXU_DIM
    tok_bf16 = jax.ShapeDtypeStruct((n_tok, D_MODEL), BF16)
    head_bf16 = jax.ShapeDtypeStruct((bsz, N_HEADS, seq, HEAD_DIM), BF16)
    return pl.pallas_call(
        _mixer_in_body,
        grid=(bsz, tiles_per_seq),
        in_specs=[
            row,
            pl.BlockSpec((1, D_MODEL), const2),
            pl.BlockSpec((D_MODEL, in_cols), const2, pipeline_mode=pl.Buffered(1)),
            pl.BlockSpec((3, tm, HEAD_DIM), lambda b, s: (0, s, 0)),
            pl.BlockSpec((CONV_WIDTH, RNN_WIDTH), const2), vec,
            pl.BlockSpec((n_grp, V7X_MXU_DIM, 2 * V7X_MXU_DIM), lambda b, s: (0, 0, 0)),
            vec, vec, vec,
        ],
        out_specs=[row, head, head, head, row, row],
        out_shape=[tok_bf16, head_bf16, head_bf16, head_bf16, tok_bf16, tok_bf16],
        scratch_shapes=[pltpu.VMEM((RNN_WIDTH // LANES, halo_rows + tm, LANES), F32),
                        pltpu.VMEM((tm // SUBLANES, RNN_WIDTH), F32), pltpu.VMEM((tm // SUBLANES, RNN_WIDTH), F32),
                        pltpu.VMEM((tm // SUBLANES, RNN_WIDTH), F32),
                        pltpu.VMEM((RNN_WIDTH // LANES, tm, LANES), F32), pltpu.VMEM((1, RNN_WIDTH), F32)],
        compiler_params=pltpu.CompilerParams(dimension_semantics=("parallel", "arbitrary"),
                                             vmem_limit_bytes=VMEM_LIMIT),
        name="mixer_in",
    )(x2, norm_g.reshape(1, D_MODEL), w_in.astype(BF16), _rotary_tables(seq), conv_w, conv_b.reshape(1, -1),
      _block_diag_gates(wx, wa), bx.reshape(1, -1), ba.reshape(1, -1), a_param.reshape(1, -1))


def _moba_body(q_all, k_all, v_all, o_all, kaug_all, qaug_all):
    n_heads_here, seq = q_all.shape[0], q_all.shape[1]
    nb = seq // MOBA_BLOCK
    nb_pad = -(-nb // SUBLANES) * SUBLANES
    key_blk = lax.broadcasted_iota(jnp.int32, (seq, LANES), 0) // MOBA_BLOCK
    lane = lax.broadcasted_iota(jnp.int32, (seq, LANES), 1)
    r_id = lax.broadcasted_iota(jnp.int32, (MOBA_BLOCK, MOBA_BLOCK), 0)
    c_id = lax.broadcasted_iota(jnp.int32, (MOBA_BLOCK, MOBA_BLOCK), 1)
    causal = c_id <= r_id
    eye = (c_id == r_id).astype(BF16)
    j_id = lax.broadcasted_iota(jnp.int32, (nb_pad, seq), 0)
    own = lax.broadcasted_iota(jnp.int32, (1, seq), 1) // MOBA_BLOCK

    def prepare(q_ref, k_ref, kaug_s, qaug_s):
        kmean = jnp.mean(k_ref[...].astype(F32).reshape(nb, MOBA_BLOCK, HEAD_DIM), axis=1)
        if nb_pad > nb:
            kmean = jnp.concatenate([kmean, jnp.zeros((nb_pad - nb, HEAD_DIM), F32)], axis=0)
        kaug_s[:, :HEAD_DIM] = k_ref[...]
        kaug_s[:, HEAD_DIM:] = (key_blk == lane).astype(BF16)
        gate = _nt_dot(kmean, q_ref[...].astype(F32), precision=lax.Precision.HIGHEST)
        rank = jnp.zeros((nb_pad, seq), F32)
        for jp in range(nb - 1):
            row = gate[jp:jp + 1, :]
            beats = ((row > gate) | ((row == gate) & (jp < j_id))) & (jp < own)
            rank = rank + beats.astype(F32)
        allowed = (j_id >= own) | (rank < float(MOBA_TOPK))
        bias_t = jnp.where(allowed, 0.0, NEG)
        bias_t = jnp.concatenate([bias_t, jnp.zeros((LANES - nb_pad, seq), F32)], axis=0).astype(BF16)
        qaug_s[:, :HEAD_DIM] = q_ref[...]
        for n in range(nb):
            cols = slice(n * MOBA_BLOCK, (n + 1) * MOBA_BLOCK)
            qaug_s[cols, HEAD_DIM:] = _nt_dot(eye, bias_t[:, cols]).astype(BF16)

    def masked_scores(h, n):
        s = _nt_dot(qaug_all[h, n * MOBA_BLOCK:(n + 1) * MOBA_BLOCK, :], kaug_all[h, 0:(n + 1) * MOBA_BLOCK, :])
        parts = [s[:, j * MOBA_BLOCK:(j + 1) * MOBA_BLOCK] for j in range(n + 1)]
        parts[n] = jnp.where(causal, parts[n], NEG)
        return parts

    def attend(h, n, parts):
        m = functools.reduce(jnp.maximum, parts)
        m = jnp.max(m, axis=1, keepdims=True)
        probs = [jnp.exp(part - m) for part in parts]
        l = jnp.sum(functools.reduce(jnp.add, probs), axis=1, keepdims=True)
        p_all = jnp.concatenate([p.astype(BF16) for p in probs], axis=1)
        acc = jnp.dot(p_all, v_all[h, 0:(n + 1) * MOBA_BLOCK, :], preferred_element_type=F32)
        o_all[h, n * MOBA_BLOCK:(n + 1) * MOBA_BLOCK, :] = (acc * (1.0 / l)).astype(BF16)

    for h in range(n_heads_here):
        prepare(q_all.at[h], k_all.at[h], kaug_all.at[h], qaug_all.at[h])
    ahead = [masked_scores(h, 0) for h in range(n_heads_here)]
    for n in range(nb):
        for h in range(n_heads_here):
            parts = ahead[h]
            if n + 1 < nb:
                ahead[h] = masked_scores(h, n + 1)
            attend(h, n, parts)


def _moba(q, k, v):
    bsz, n_heads, seq, hd = q.shape
    g = MOBA_HEADS_PER_STEP
    spec = pl.BlockSpec((None, g, seq, hd), lambda b, h: (b, h, 0, 0))
    return pl.pallas_call(
        _moba_body,
        grid=(bsz, n_heads // g),
        in_specs=[spec, spec, spec],
        out_specs=spec,
        out_shape=jax.ShapeDtypeStruct(q.shape, BF16),
        scratch_shapes=[pltpu.VMEM((g, seq, 2 * hd), BF16), pltpu.VMEM((g, seq, 2 * hd), BF16)],
        compiler_params=pltpu.CompilerParams(dimension_semantics=("parallel", "parallel"),
                                             vmem_limit_bytes=VMEM_LIMIT),
        name="moba",
    )(q, k, v)


def _merge_body(x_ref, hg_ref, at_ref, sr_ref, sa_ref, wpr_ref, wpa_ref, wo_ref, gffn_ref, rwt_ref, rb_ref,
                x1_ref, hn_ref, idx_ref, gate_ref, rank_ref, cnt_ref, carry_s):
    tm = x_ref.shape[0]

    @pl.when(pl.program_id(0) == 0)
    def _():
        carry_s[...] = jnp.zeros_like(carry_s)

    y_rnn = jnp.dot(hg_ref[...], wpr_ref[...], preferred_element_type=F32)
    attn = jnp.concatenate([at_ref[h] for h in range(N_HEADS)], axis=1)
    y_attn = jnp.dot(attn, wpa_ref[...], preferred_element_type=F32)
    mixed = sr_ref[...].astype(F32) * y_rnn + sa_ref[...].astype(F32) * y_attn
    x1 = x_ref[...] + jnp.dot(mixed.astype(BF16), wo_ref[...], preferred_element_type=F32)
    x1_ref[...] = x1
    hn = x1 * lax.rsqrt(jnp.mean(x1 * x1, axis=-1, keepdims=True) + EPS) * gffn_ref[...]
    _store_row_tiles(hn_ref, hn)

    logits = _nt_dot(rwt_ref[...], hn.astype(BF16)) + rb_ref[...]
    e_id = lax.broadcasted_iota(jnp.int32, (N_EXPERTS, tm), 0)
    vals = logits
    onehots, top_vals, top_ids = [], [], []
    for _ in range(TOP_K):
        best = jnp.max(vals, axis=0, keepdims=True)
        best_id = jnp.min(jnp.where(vals == best, e_id, N_EXPERTS), axis=0, keepdims=True)
        hit = e_id == best_id
        vals = jnp.where(hit, -jnp.inf, vals)
        onehots.append(hit)
        top_vals.append(best)
        top_ids.append(best_id)
    exps = [jnp.exp(v - top_vals[0]) for v in top_vals]
    denom = exps[0] + exps[1] + exps[2] + exps[3]
    idx_ref[...] = jnp.concatenate(top_ids, axis=0)
    gate_ref[...] = jnp.concatenate([e / denom for e in exps], axis=0)

    chosen = onehots[0] | onehots[1] | onehots[2] | onehots[3]
    t_row = lax.broadcasted_iota(jnp.int32, (tm, tm), 0)
    t_col = lax.broadcasted_iota(jnp.int32, (tm, tm), 1)
    before = (t_row < t_col).astype(BF16)
    chosen_f = chosen.astype(F32)
    prior = jnp.dot(chosen_f.astype(BF16), before, preferred_element_type=F32) + carry_s[:, 0:1]
    ranks = [jnp.sum(jnp.where(hit, prior, 0.0), axis=0, keepdims=True) for hit in onehots]
    rank_ref[...] = jnp.concatenate(ranks, axis=0).astype(jnp.int32)
    carry_s[...] = carry_s[...] + jnp.sum(chosen_f, axis=1, keepdims=True)
    cnt_ref[...] = carry_s[...].astype(jnp.int32)


def _merge(x2, hg, attn, sr, sa, w_proj_rnn, w_proj_attn, w_out, norm_ffn_g, router_w, router_b, seq):
    n_tok = x2.shape[0]
    tm = ROW_TILE
    tiles_per_seq = seq // tm
    row = pl.BlockSpec((tm, D_MODEL), lambda i: (i, 0))
    head = pl.BlockSpec((None, N_HEADS, tm, HEAD_DIM), lambda i: (i // tiles_per_seq, 0, i % tiles_per_seq, 0))
    mat = pl.BlockSpec((D_MODEL, D_MODEL), lambda i: (0, 0))
    topk = pl.BlockSpec((TOP_K, tm), lambda i: (0, i))
    return pl.pallas_call(
        _merge_body,
        grid=(n_tok // tm,),
        in_specs=[row, row, head, row, row, mat, mat, mat,
                  pl.BlockSpec((1, D_MODEL), lambda i: (0, 0)),
                  pl.BlockSpec((N_EXPERTS, D_MODEL), lambda i: (0, 0)),
                  pl.BlockSpec((N_EXPERTS, 1), lambda i: (0, 0))],
        out_specs=[row, pl.BlockSpec((tm * ROW_TILES, LANES), lambda i: (i, 0)), topk, topk, topk,
                   pl.BlockSpec((N_EXPERTS, LANES), lambda i: (0, 0))],
        out_shape=[jax.ShapeDtypeStruct((n_tok, D_MODEL), F32),
                   jax.ShapeDtypeStruct((n_tok * ROW_TILES, LANES), F32),
                   jax.ShapeDtypeStruct((TOP_K, n_tok), jnp.int32), jax.ShapeDtypeStruct((TOP_K, n_tok), F32),
                   jax.ShapeDtypeStruct((TOP_K, n_tok), jnp.int32),
                   jax.ShapeDtypeStruct((N_EXPERTS, LANES), jnp.int32)],
        scratch_shapes=[pltpu.VMEM((N_EXPERTS, LANES), F32)],
        compiler_params=pltpu.CompilerParams(dimension_semantics=("arbitrary",), vmem_limit_bytes=VMEM_LIMIT),
        name="merge_route",
    )(x2, hg, attn, sr, sa, w_proj_rnn.astype(BF16), w_proj_attn.astype(BF16), w_out.astype(BF16),
      norm_ffn_g.reshape(1, D_MODEL), router_w.T.astype(BF16), router_b.reshape(N_EXPERTS, 1))


def _row_copy(src_ref, src_row, dst_ref, dst_row, sem):
    return pltpu.make_async_copy(_row_slab(src_ref, src_row), _row_slab(dst_ref, dst_row), sem)


def _tile_flat(dest, tm):
    return dest.reshape(TOP_K, -1, tm).transpose(1, 0, 2).reshape(-1)


def _for_each_token_group(tm, body):
    def group(g, c):
        body(pl.multiple_of(g * ISSUE_UNROLL, ISSUE_UNROLL))
        return c

    lax.fori_loop(0, tm // ISSUE_UNROLL, group, 0)


def _start_group_copies(copy_of, t0):
    for i in range(ISSUE_UNROLL):
        for k in range(TOP_K):
            copy_of(t0 + i, k).start(priority=(i * TOP_K + k) % DMA_QUEUES)


def _dispatch_body(pad_off_ref, pad_len_ref, dest_ref, hn_hbm, rows_ref, hbuf, load_sems, scat_sems, pad_sem):
    tm = hbuf.shape[1] // ROW_TILES
    tile_rows = tm * ROW_TILES
    step = pl.program_id(0)
    n_steps = pl.num_programs(0)
    slot = step % DISPATCH_SLOTS
    prev_slot = (step + DISPATCH_SLOTS - 1) % DISPATCH_SLOTS

    def load(tile, into):
        start = pl.multiple_of(tile * tile_rows, tile_rows)
        return pltpu.make_async_copy(hn_hbm.at[pl.ds(start, tile_rows), :], hbuf.at[into], load_sems.at[into])

    def wait_scatters(of_slot):
        for _ in range(TOP_K):
            pltpu.make_async_copy(hbuf.at[of_slot], rows_ref.at[pl.ds(0, tile_rows), :], scat_sems.at[of_slot]).wait()

    def for_each_pad_run(act):
        def per_expert(e, c):
            off, length = pad_off_ref[e], pad_len_ref[e]
            for bit in range(EXPERT_BLOCK_ROWS.bit_length() - 1):
                rows = 1 << bit

                @pl.when(((length >> bit) & 1) == 1)
                def _():
                    start = pl.multiple_of((off + (length & (rows - 1))) * ROW_TILES, ROW_TILES)
                    act(pltpu.make_async_copy(hbuf.at[0, pl.ds(0, rows * ROW_TILES), :],
                                              rows_ref.at[pl.ds(start, rows * ROW_TILES), :], pad_sem))
            return c

        lax.fori_loop(0, N_EXPERTS, per_expert, 0)

    @pl.when(step == 0)
    def _():
        load(0, 0).start()

        @pl.when(n_steps > 1)
        def _():
            load(1, 1).start()

    load(step, slot).wait()

    @pl.when(step == 0)
    def _():
        for_each_pad_run(lambda cp: cp.start())

    src = hbuf.at[slot]
    scatter = lambda t, k: _row_copy(src, t, rows_ref, dest_ref[k * tm + t], scat_sems.at[slot])
    _for_each_token_group(tm, lambda t0: _start_group_copies(scatter, t0))

    @pl.when(step == 0)
    def _():
        for_each_pad_run(lambda cp: cp.wait())

    @pl.when(step >= 1)
    def _():
        wait_scatters(prev_slot)

    @pl.when(step + 2 < n_steps)
    def _():
        load(step + 2, prev_slot).start()

    @pl.when(step == n_steps - 1)
    def _():
        wait_scatters(slot)


def _dispatch(dest, pad_off, pad_len, hn, n_rows):
    n_tok = hn.shape[0] // ROW_TILES
    tm = math.gcd(DISPATCH_TILE, n_tok)
    assert tm >= EXPERT_BLOCK_ROWS // 2, "pad runs are sourced from the first token tile"
    grid_spec = pltpu.PrefetchScalarGridSpec(
        num_scalar_prefetch=2,
        grid=(n_tok // tm,),
        in_specs=[pl.BlockSpec((TOP_K * tm,), lambda i, po, pn: (i,), memory_space=pltpu.SMEM),
                  pl.BlockSpec(memory_space=pl.ANY)],
        out_specs=pl.BlockSpec(memory_space=pl.ANY),
        scratch_shapes=[pltpu.VMEM((DISPATCH_SLOTS, tm * ROW_TILES, LANES), F32),
                        pltpu.SemaphoreType.DMA((DISPATCH_SLOTS,)), pltpu.SemaphoreType.DMA((DISPATCH_SLOTS,)),
                        pltpu.SemaphoreType.DMA(())],
    )
    return pl.pallas_call(
        _dispatch_body,
        grid_spec=grid_spec,
        out_shape=jax.ShapeDtypeStruct((n_rows * ROW_TILES, LANES), F32),
        compiler_params=pltpu.CompilerParams(dimension_semantics=("arbitrary",), disable_bounds_checks=True,
                                             vmem_limit_bytes=VMEM_LIMIT),
        name="dispatch",
    )(pad_off, pad_len, _tile_flat(dest, tm), hn)


def _expert_body(blk_expert_ref, n_used_ref, next_expert_ref, slot_ref, x_ref, wgu_hbm, bgu_ref, wd_hbm, bd_ref, y_ref,
                 wgu_buf, wd_buf, wgu_s, wd_s, sems):
    i = pl.program_id(0)
    live = i < n_used_ref[0]
    expert = blk_expert_ref[i]
    new_expert = (i == 0) | (expert != blk_expert_ref[jnp.maximum(i - 1, 0)])

    def weight_copies(e, slot):
        return (pltpu.make_async_copy(wgu_hbm.at[e], wgu_buf.at[slot], sems.at[0, slot]),
                pltpu.make_async_copy(wd_hbm.at[e], wd_buf.at[slot], sems.at[1, slot]))

    @pl.when(live & new_expert)
    def _():
        slot = slot_ref[expert]
        upcoming = next_expert_ref[expert]

        @pl.when(i == 0)
        def _():
            for cp in weight_copies(expert, slot):
                cp.start()

        @pl.when(upcoming >= 0)
        def _():
            for cp in weight_copies(upcoming, 1 - slot):
                cp.start()

        for cp in weight_copies(expert, slot):
            cp.wait()
        for r in range(0, D_MODEL, WEIGHT_CAST_ROWS):
            wgu_s[r:r + WEIGHT_CAST_ROWS, :] = wgu_buf[slot, r:r + WEIGHT_CAST_ROWS, :].astype(BF16)
        for r in range(0, D_EXPERT, WEIGHT_CAST_ROWS):
            wd_s[r:r + WEIGHT_CAST_ROWS, :] = wd_buf[slot, r:r + WEIGHT_CAST_ROWS, :].astype(BF16)

    @pl.when(live)
    def _():
        gu = jnp.dot(_load_row_tiles(x_ref).astype(BF16), wgu_s[...], preferred_element_type=F32) + bgu_ref[...]
        x_glu = jnp.minimum(gu[:, :D_EXPERT], SWIGLU_LIMIT)
        x_lin = jnp.clip(gu[:, D_EXPERT:], -SWIGLU_LIMIT, SWIGLU_LIMIT)
        act = x_glu * jax.nn.sigmoid(SWIGLU_ALPHA * x_glu) * (x_lin + 1.0)
        _store_row_tiles(y_ref, jnp.dot(act.astype(BF16), wd_s[...], preferred_element_type=F32) + bd_ref[...])

    @pl.when(jnp.logical_not(live))
    def _():
        y_ref[...] = jnp.zeros_like(y_ref)


def _experts(blk_expert, n_used, next_expert, slot_of, x_rows, w_gu, b_gu, w_down, b_down):
    n_rows = x_rows.shape[0] // ROW_TILES
    rb = EXPERT_BLOCK_ROWS
    grid_spec = pltpu.PrefetchScalarGridSpec(
        num_scalar_prefetch=4,
        grid=(n_rows // rb,),
        in_specs=[
            pl.BlockSpec((rb * ROW_TILES, LANES), lambda i, be, nu, nx, sl: (jnp.minimum(i, nu[0] - 1), 0)),
            pl.BlockSpec(memory_space=pl.ANY),
            pl.BlockSpec((None, 1, 2 * D_EXPERT), lambda i, be, nu, nx, sl: (be[i], 0, 0)),
            pl.BlockSpec(memory_space=pl.ANY),
            pl.BlockSpec((None, 1, D_MODEL), lambda i, be, nu, nx, sl: (be[i], 0, 0)),
        ],
        out_specs=pl.BlockSpec((rb * ROW_TILES, LANES), lambda i, be, nu, nx, sl: (i, 0)),
        scratch_shapes=[pltpu.VMEM((2, D_MODEL, 2 * D_EXPERT), F32), pltpu.VMEM((2, D_EXPERT, D_MODEL), F32),
                        pltpu.VMEM((D_MODEL, 2 * D_EXPERT), BF16), pltpu.VMEM((D_EXPERT, D_MODEL), BF16),
                        pltpu.SemaphoreType.DMA((2, 2))],
    )
    return pl.pallas_call(
        _expert_body,
        grid_spec=grid_spec,
        out_shape=jax.ShapeDtypeStruct((n_rows * ROW_TILES, LANES), F32),
        compiler_params=pltpu.CompilerParams(dimension_semantics=("arbitrary",), vmem_limit_bytes=VMEM_LIMIT),
        name="experts",
    )(blk_expert, n_used, next_expert, slot_of, x_rows, w_gu, b_gu.reshape(N_EXPERTS, 1, -1), w_down,
      b_down.reshape(N_EXPERTS, 1, -1))


def _combine_body(dest_ref, dest_next_ref, gate_ref, x1_ref, y_ref, gfin_ref, out_ref, ybuf, sems):
    tm = x1_ref.shape[0]
    step = pl.program_id(0)
    slot = step % 2

    def gather(d_ref, into):
        fetch = lambda t, k: _row_copy(y_ref, d_ref[k * tm + t], ybuf.at[into, k], t, sems.at[into])
        _for_each_token_group(tm, lambda t0: _start_group_copies(fetch, t0))

    @pl.when(step == 0)
    def _():
        gather(dest_ref, 0)

    for k in range(TOP_K):
        pltpu.make_async_copy(y_ref.at[pl.ds(0, tm * ROW_TILES), :], ybuf.at[slot, k], sems.at[slot]).wait()
    has_next = step + 1 < pl.num_programs(0)
    fetch_next = lambda t, k: _row_copy(y_ref, dest_next_ref[k * tm + t], ybuf.at[1 - slot, k], t,
                                        sems.at[1 - slot])

    def reduce_group(t0, with_fetch):
        rows = pl.ds(t0, ISSUE_UNROLL)
        gates = gate_ref[rows, :]
        x2 = x1_ref[rows, :]
        picked = [jnp.concatenate([ybuf[slot, k, pl.ds(t0 * ROW_TILES + s, ISSUE_UNROLL, stride=ROW_TILES), :]
                                   for s in range(ROW_TILES)], axis=1) for k in range(TOP_K)]
        if with_fetch:
            _start_group_copies(fetch_next, t0)
        for k in range(TOP_K):
            x2 = x2 + gates[:, k:k + 1] * picked[k]
        out_ref[rows, :] = x2 * lax.rsqrt(jnp.mean(x2 * x2, axis=-1, keepdims=True) + EPS) * gfin_ref[...]

    @pl.when(has_next)
    def _():
        _for_each_token_group(tm, lambda t0: reduce_group(t0, True))

    @pl.when(jnp.logical_not(has_next))
    def _():
        _for_each_token_group(tm, lambda t0: reduce_group(t0, False))


def _combine(dest, gates_tok, x1, y_rows, norm_final_g):
    n_tok = x1.shape[0]
    tm = GATHER_TILE
    row = pl.BlockSpec((tm, D_MODEL), lambda i: (i, 0))
    n_tiles = n_tok // tm
    return pl.pallas_call(
        _combine_body,
        grid=(n_tiles,),
        in_specs=[pl.BlockSpec((TOP_K * tm,), lambda i: (i,), memory_space=pltpu.SMEM),
                  pl.BlockSpec((TOP_K * tm,), lambda i: (jnp.minimum(i + 1, n_tiles - 1),), memory_space=pltpu.SMEM),
                  pl.BlockSpec((tm, TOP_K), lambda i: (i, 0)),
                  row,
                  pl.BlockSpec(memory_space=pl.ANY),
                  pl.BlockSpec((1, D_MODEL), lambda i: (0, 0))],
        out_specs=row,
        out_shape=jax.ShapeDtypeStruct((n_tok, D_MODEL), F32),
        scratch_shapes=[pltpu.VMEM((2, TOP_K, tm * ROW_TILES, LANES), F32), pltpu.SemaphoreType.DMA((2,))],
        compiler_params=pltpu.CompilerParams(dimension_semantics=("arbitrary",), vmem_limit_bytes=VMEM_LIMIT,
                                             disable_bounds_checks=True),
        name="combine",
    )(_tile_flat(dest, tm), _tile_flat(dest, tm), gates_tok, x1, y_rows, norm_final_g.reshape(1, D_MODEL))


def _routing_plan(top_idx, rank, counts, n_tok):
    padded = ((counts + EXPERT_BLOCK_ROWS - 1) // EXPERT_BLOCK_ROWS) * EXPERT_BLOCK_ROWS
    pend = jnp.cumsum(padded)
    pstart = pend - padded
    e_id = jnp.arange(N_EXPERTS, dtype=jnp.int32)[:, None, None]
    dest = jnp.sum(jnp.where(top_idx[None] == e_id, pstart[:, None, None], 0), axis=0) + rank
    n_rows = n_tok * TOP_K + N_EXPERTS * EXPERT_BLOCK_ROWS
    blk_start = jnp.arange(n_rows // EXPERT_BLOCK_ROWS, dtype=jnp.int32) * EXPERT_BLOCK_ROWS
    blk_expert = jnp.minimum(jnp.sum(blk_start[:, None] >= pend[None, :], axis=1), N_EXPERTS - 1)
    n_used = (pend[-1] // EXPERT_BLOCK_ROWS).reshape(1)
    pad_off, pad_len = pstart + counts, padded - counts
    used = counts > 0
    ids = jnp.arange(N_EXPERTS, dtype=jnp.int32)
    later = jnp.where(used[None, :] & (ids[None, :] > ids[:, None]), ids[None, :], N_EXPERTS)
    next_expert = jnp.min(later, axis=1)
    next_expert = jnp.where(next_expert == N_EXPERTS, -1, next_expert)
    slot_of = (jnp.cumsum(used) - used) % 2
    return (dest.astype(jnp.int32), blk_expert.astype(jnp.int32), n_used.astype(jnp.int32),
            next_expert.astype(jnp.int32), slot_of.astype(jnp.int32),
            pad_off.astype(jnp.int32), pad_len.astype(jnp.int32), n_rows)


def kernel(x, norm_mix_g, w_in, conv_w, conv_b, lru_wx, lru_bx, lru_wa, lru_ba, lru_a_param, w_proj_rnn, w_proj_attn, w_out, norm_ffn_g, router_w, router_b, expert_w_gu, expert_b_gu, expert_w_down, expert_b_down, norm_final_g):
    bsz, seq, _ = x.shape
    n_tok = bsz * seq
    assert w_in.shape[0] == 1, "single-layer problem: the final RMSNorm is fused into the combine stage"
    assert seq % ROW_TILE == 0 and seq % MOBA_BLOCK == 0
    layer = 0
    x2 = x.reshape(n_tok, D_MODEL)
    hg, q, k, v, sr, sa = _mixer_in(x2, norm_mix_g[layer], w_in[layer], conv_w[layer], conv_b[layer],
                                    lru_wx[layer], lru_bx[layer], lru_wa[layer], lru_ba[layer],
                                    lru_a_param[layer], bsz, seq)
    attn = _moba(q, k, v)
    x1, hn, top_idx, gates, rank, cnt = _merge(x2, hg, attn, sr, sa, w_proj_rnn[layer], w_proj_attn[layer],
                                               w_out[layer], norm_ffn_g[layer], router_w[layer],
                                               router_b[layer], seq)
    dest, blk_expert, n_used, next_expert, slot_of, pad_off, pad_len, n_rows = _routing_plan(
        top_idx, rank, cnt[:, 0], n_tok)
    x_rows = _dispatch(dest, pad_off, pad_len, hn, n_rows)
    y_rows = _experts(blk_expert, n_used, next_expert, slot_of, x_rows, expert_w_gu[layer], expert_b_gu[layer],
                      expert_w_down[layer], expert_b_down[layer])
    out = _combine(dest, gates.T, x1, y_rows, norm_final_g)
    return out.reshape(bsz, seq, D_MODEL)
```
